```python
import math
import jax
import jax.numpy as jnp
from jax import lax
import numpy as np

D_MODEL = 2048
BATCH = 2
SEQ = 4096
DEPTH = 2

N_EVEN = (DEPTH + 1) // 2
N_ODD = DEPTH // 2

DEEPNORM_ALPHA = (2.0 * DEPTH) ** 0.25
DEEPNORM_BETA = (8.0 * DEPTH) ** -0.25
MACARON_WEIGHT = 0.5
LN_EPS = 1e-5
RMS_EPS = 1e-6
L2_EPS = 1e-6

D_FF = 5504

GLA_HEADS = 4
GLA_DK = 128
GLA_DV = 256
GLA_RANK = 16
GLA_TAU = 16.0
GLA_CHUNK = 64
GLA_KW = GLA_HEADS * GLA_DK
GLA_VW = GLA_HEADS * GLA_DV

S5_WIDTH = 1024
S5_GROUP = 16
S5_GROUPS = S5_WIDTH // S5_GROUP
S5_STATE = 64
S5_DT_MIN = 1e-3
S5_DT_MAX = 1e-1

AB_Q = 0
AB_K = AB_Q + GLA_KW
AB_V = AB_K + GLA_KW
AB_G = AB_V + GLA_VW
AB_A = AB_G + GLA_VW
AB_U = AB_A + GLA_RANK
AB_IN = AB_U + S5_WIDTH
AB_MIX = GLA_VW + S5_WIDTH

GDN_QK_HEADS = 16
GDN_V_HEADS = 32
GDN_DK = 128
GDN_DV = 128
GDN_CONV = 4
GDN_CHUNK = 64
GDN_KW = GDN_QK_HEADS * GDN_DK
GDN_VW = GDN_V_HEADS * GDN_DV
GDN_QKV = 2 * GDN_KW + GDN_VW
GDN_Z = GDN_QKV
GDN_B = GDN_Z + GDN_VW
GDN_A = GDN_B + GDN_V_HEADS
GDN_IN = GDN_A + GDN_V_HEADS

kernel_name = 'hybrid_gla_s5_gdn_macaron_deepnorm'


def layer_norm(x, g, b):
    xf = x.astype(jnp.float32)
    mu = jnp.mean(xf, axis=-1, keepdims=True)
    var = jnp.mean(jnp.square(xf - mu), axis=-1, keepdims=True)
    return (xf - mu) * lax.rsqrt(var + LN_EPS) * g.astype(jnp.float32) + b.astype(jnp.float32)


def rms_norm(x, g):
    xf = x.astype(jnp.float32)
    return xf * lax.rsqrt(jnp.mean(xf * xf, axis=-1, keepdims=True) + RMS_EPS) * g.astype(jnp.float32)


def l2_normalize(x):
    return x * lax.rsqrt(jnp.sum(x * x, axis=-1, keepdims=True) + L2_EPS)


def swiglu(x, w_gate, w_up, w_down):
    return (jax.nn.silu(x @ w_gate) * (x @ w_up)) @ w_down


def _heads(t, n_heads):
    b, l, w = t.shape
    return t.reshape(b, l, n_heads, w // n_heads).transpose(0, 2, 1, 3)


def _merge(t):
    b, h, l, d = t.shape
    return t.transpose(0, 2, 1, 3).reshape(b, l, h * d)


def gla_chunked(q, k, v, log_a):
    bn, h, l, dk = q.shape
    dv = v.shape[-1]
    c = GLA_CHUNK
    n = l // c
    q = (q * dk ** -0.5).reshape(bn, h, n, c, dk)
    k = k.reshape(bn, h, n, c, dk)
    v = v.reshape(bn, h, n, c, dv)
    b = jnp.cumsum(log_a.reshape(bn, h, n, c, dk), axis=3)
    b_last = b[:, :, :, -1:, :]
    q_dec = q * jnp.exp(b)
    k_inv = k * jnp.exp(-b)
    causal = jnp.tril(jnp.ones((c, c), bool))
    scores = jnp.where(causal, jnp.einsum('bhnik,bhnjk->bhnij', q_dec, k_inv), 0.0)
    o_intra = jnp.einsum('bhnij,bhnjv->bhniv', scores, v)
    k_end = k * jnp.exp(b_last - b)
    d_state = jnp.einsum('bhnck,bhncv->bhnkv', k_end, v)
    chunk_decay = jnp.exp(b_last[:, :, :, 0, :])

    def step(s, inp):
        dec, ds = inp
        return dec[..., None] * s + ds, s

    s0 = jnp.zeros((bn, h, dk, dv), jnp.float32)
    _, s_start = lax.scan(step, s0, (jnp.moveaxis(chunk_decay, 2, 0), jnp.moveaxis(d_state, 2, 0)))
    s_start = jnp.moveaxis(s_start, 0, 2)
    o_inter = jnp.einsum('bhnck,bhnkv->bhncv', q_dec, s_start)
    return (o_intra + o_inter).reshape(bn, h, l, dv)


def _complex_affine_combine(e1, e2):
    a1r, a1i, b1r, b1i = e1
    a2r, a2i, b2r, b2i = e2
    return (a2r * a1r - a2i * a1i,
            a2r * a1i + a2i * a1r,
            a2r * b1r - a2i * b1i + b2r,
            a2r * b1i + a2i * b1r + b2i)


def s5_ssm(u, lam_re, lam_im, b_re, b_im, c_re, c_im, d, log_step):
    f32 = jnp.float32
    bn, l, _ = u.shape
    uf = u.astype(f32).reshape(bn, l, S5_GROUPS, S5_GROUP)
    step = jnp.exp(log_step.astype(f32))[:, None]
    lr = lam_re.astype(f32)
    li = lam_im.astype(f32)
    mag = jnp.exp(lr * step)
    a_re = mag * jnp.cos(li * step)
    a_im = mag * jnp.sin(li * step)
    inv_den = 1.0 / (lr * lr + li * li)
    f_re = ((a_re - 1.0) * lr + a_im * li) * inv_den
    f_im = (a_im * lr - (a_re - 1.0) * li) * inv_den
    br = b_re.astype(f32)
    bi = b_im.astype(f32)
    bb_re = f_re[..., None] * br - f_im[..., None] * bi
    bb_im = f_re[..., None] * bi + f_im[..., None] * br
    bu_re = jnp.einsum('blgh,gph->blgp', uf, bb_re)
    bu_im = jnp.einsum('blgh,gph->blgp', uf, bb_im)
    shape = bu_re.shape
    _, _, s_re, s_im = lax.associative_scan(
        _complex_affine_combine,
        (jnp.broadcast_to(a_re, shape), jnp.broadcast_to(a_im, shape), bu_re, bu_im),
        axis=1)
    y = (jnp.einsum('blgp,ghp->blgh', s_re, c_re.astype(f32))
         - jnp.einsum('blgp,ghp->blgh', s_im, c_im.astype(f32)))
    y = y + d.astype(f32).reshape(S5_GROUPS, S5_GROUP) * uf
    return y.reshape(bn, l, S5_WIDTH)


def gla_s5_mixer(x, w_in, gla_w_lr, gla_b_lr, gla_norm_g, lam_re, lam_im, b_re, b_im,
                 c_re, c_im, s5_d, s5_log_step, glu_w, glu_b, w_out):
    f32 = jnp.float32
    proj = x @ w_in
    q = _heads(proj[..., AB_Q:AB_K], GLA_HEADS).astype(f32)
    k = _heads(proj[..., AB_K:AB_V], GLA_HEADS).astype(f32)
    v = _heads(proj[..., AB_V:AB_G], GLA_HEADS).astype(f32)
    g_out = proj[..., AB_G:AB_A].astype(f32)
    gate_lr = proj[..., AB_A:AB_U]
    u = proj[..., AB_U:AB_IN]
    log_a = jax.nn.log_sigmoid((gate_lr @ gla_w_lr + gla_b_lr).astype(f32)) / GLA_TAU
    o = gla_chunked(q, k, v, _heads(log_a, GLA_HEADS))
    o_gla = _merge(rms_norm(o, gla_norm_g)) * jax.nn.silu(g_out)
    y = jax.nn.gelu(s5_ssm(u, lam_re, lam_im, b_re, b_im, c_re, c_im, s5_d, s5_log_step))
    o_s5 = y * jax.nn.sigmoid(y @ glu_w + glu_b)
    return jnp.concatenate([o_gla, o_s5], axis=-1) @ w_out


def causal_depthwise_conv(x, w):
    kw = w.shape[0]
    xp = jnp.pad(x, ((0, 0), (kw - 1, 0), (0, 0)))
    return lax.conv_general_dilated(xp, w[:, None, :], window_strides=(1,), padding='VALID',
                                    dimension_numbers=('NWC', 'WIO', 'NWC'),
                                    feature_group_count=x.shape[-1])


def gdn_chunked(q, k, v, beta, g):
    bn, h, l, dk = q.shape
    dv = v.shape[-1]
    c = GDN_CHUNK
    n = l // c
    q = (q * dk ** -0.5).reshape(bn, h, n, c, dk)
    k = k.reshape(bn, h, n, c, dk)
    v = v.reshape(bn, h, n, c, dv)
    beta = beta.reshape(bn, h, n, c, 1)
    gc = jnp.cumsum(g.reshape(bn, h, n, c), axis=-1)
    incl = jnp.tril(jnp.ones((c, c), bool))
    strict = jnp.tril(jnp.ones((c, c), bool), -1)
    decay = jnp.exp(jnp.where(incl, gc[..., :, None] - gc[..., None, :], -jnp.inf))
    k_beta = k * beta
    a_strict = jnp.where(strict, jnp.einsum('bhnik,bhnjk->bhnij', k_beta, k) * decay, 0.0)
    rhs = jnp.concatenate([v * beta, k_beta * jnp.exp(gc)[..., None]], axis=-1)
    sol = lax.linalg.triangular_solve(a_strict + jnp.eye(c, dtype=jnp.float32), rhs,
                                      left_side=True, lower=True)
    u_val = sol[..., :dv]
    w_dec = sol[..., dv:]
    attn = jnp.einsum('bhnik,bhnjk->bhnij', q, k) * decay
    q_dec = q * jnp.exp(gc)[..., None]
    g_last = gc[..., -1:]
    k_end = k * jnp.exp(g_last - gc)[..., None]
    d_last = jnp.exp(g_last[..., 0])

    def step(s, inp):
        u_n, w_n, qd_n, at_n, ke_n, dl_n = inp
        v_new = u_n - jnp.einsum('bhck,bhkv->bhcv', w_n, s)
        o_n = jnp.einsum('bhck,bhkv->bhcv', qd_n, s) + jnp.einsum('bhij,bhjv->bhiv', at_n, v_new)
        s = dl_n[..., None, None] * s + jnp.einsum('bhck,bhcv->bhkv', ke_n, v_new)
        return s, o_n

    xs = tuple(jnp.moveaxis(t, 2, 0) for t in (u_val, w_dec, q_dec, attn, k_end, d_last))
    s0 = jnp.zeros((bn, h, dk, dv), jnp.float32)
    _, o = lax.scan(step, s0, xs)
    return jnp.moveaxis(o, 0, 2).reshape(bn, h, l, dv)


def gdn_mixer(x, w_in, conv_w, a_log, dt_bias, norm_g, w_out):
    f32 = jnp.float32
    proj = x @ w_in
    qkv = jax.nn.silu(causal_depthwise_conv(proj[..., :GDN_QKV].astype(f32), conv_w.astype(f32)))
    rep = GDN_V_HEADS // GDN_QK_HEADS
    q = jnp.repeat(l2_normalize(_heads(qkv[..., :GDN_KW], GDN_QK_HEADS)), rep, axis=1)
    k = jnp.repeat(l2_normalize(_heads(qkv[..., GDN_KW:2 * GDN_KW], GDN_QK_HEADS)), rep, axis=1)
    v = _heads(qkv[..., 2 * GDN_KW:], GDN_V_HEADS)
    z = _heads(proj[..., GDN_Z:GDN_B].astype(f32), GDN_V_HEADS)
    beta = jax.nn.sigmoid(proj[..., GDN_B:GDN_A].astype(f32)).transpose(0, 2, 1)
    g = (-jnp.exp(a_log.astype(f32))
         * jax.nn.softplus(proj[..., GDN_A:GDN_IN].astype(f32) + dt_bias.astype(f32))).transpose(0, 2, 1)
    o = gdn_chunked(q, k, v, beta, g)
    o = rms_norm(o, norm_g) * jax.nn.silu(z)
    return _merge(o) @ w_out


def setup_inputs(seed: int = 0) -> dict:
    key = jax.random.key(seed)
    ks = jax.random.split(key, 32)
    f32 = jnp.float32
    D = D_MODEL

    def nrm(i, shape, scale):
        return scale * jax.random.normal(ks[i], shape, f32)

    def uni(i, shape, lo, hi):
        return jax.random.uniform(ks[i], shape, f32, lo, hi)

    x = nrm(0, (BATCH, SEQ, D), 1.0)
    ffn_a_gate = nrm(1, (DEPTH, D, D_FF), D ** -0.5)
    ffn_a_up = nrm(2, (DEPTH, D, D_FF), D ** -0.5)
    ffn_a_down = nrm(3, (DEPTH, D_FF, D), DEEPNORM_BETA * D_FF ** -0.5)
    ffn_b_gate = nrm(4, (DEPTH, D, D_FF), D ** -0.5)
    ffn_b_up = nrm(5, (DEPTH, D, D_FF), D ** -0.5)
    ffn_b_down = nrm(6, (DEPTH, D_FF, D), DEEPNORM_BETA * D_FF ** -0.5)
    ln_g = 1.0 + nrm(7, (DEPTH, 3, D), 0.02)
    ln_b = nrm(8, (DEPTH, 3, D), 0.02)
    ab_w_in = nrm(9, (N_EVEN, D, AB_IN), D ** -0.5)
    gla_w_lr = nrm(10, (N_EVEN, GLA_RANK, GLA_KW), GLA_RANK ** -0.5)
    gla_b_lr = nrm(11, (N_EVEN, GLA_KW), 0.1)
    gla_norm_g = 1.0 + nrm(12, (N_EVEN, GLA_DV), 0.02)
    s5_lam_re = -0.5 * jnp.exp(nrm(13, (N_EVEN, S5_GROUPS, S5_STATE), 0.02))
    s5_lam_im = jnp.broadcast_to(jnp.pi * jnp.arange(S5_STATE, dtype=f32), (N_EVEN, S5_GROUPS, S5_STATE))
    s5_b_re = nrm(14, (N_EVEN, S5_GROUPS, S5_STATE, S5_GROUP), (2 * S5_GROUP) ** -0.5)
    s5_b_im = nrm(15, (N_EVEN, S5_GROUPS, S5_STATE, S5_GROUP), (2 * S5_GROUP) ** -0.5)
    s5_c_re = nrm(16, (N_EVEN, S5_GROUPS, S5_GROUP, S5_STATE), (2 * S5_STATE) ** -0.5)
    s5_c_im = nrm(17, (N_EVEN, S5_GROUPS, S5_GROUP, S5_STATE), (2 * S5_STATE) ** -0.5)
    s5_d = nrm(18, (N_EVEN, S5_WIDTH), 1.0)
    s5_log_step = uni(19, (N_EVEN, S5_GROUPS), math.log(S5_DT_MIN), math.log(S5_DT_MAX))
    s5_glu_w = nrm(20, (N_EVEN, S5_WIDTH, S5_WIDTH), S5_WIDTH ** -0.5)
    s5_glu_b = nrm(21, (N_EVEN, S5_WIDTH), 0.02)
    ab_w_out = nrm(22, (N_EVEN, AB_MIX, D), DEEPNORM_BETA * AB_MIX ** -0.5)
    gdn_w_in = nrm(23, (N_ODD, D, GDN_IN), D ** -0.5)
    gdn_conv_w = nrm(24, (N_ODD, GDN_CONV, GDN_QKV), GDN_CONV ** -0.5)
    gdn_a_log = jnp.log(uni(25, (N_ODD, GDN_V_HEADS), 1.0, 16.0))
    dt = jnp.exp(uni(26, (N_ODD, GDN_V_HEADS), math.log(1e-3), math.log(1e-1)))
    gdn_dt_bias = dt + jnp.log(-jnp.expm1(-dt))
    gdn_norm_g = 1.0 + nrm(27, (N_ODD, GDN_DV), 0.02)
    gdn_w_out = nrm(28, (N_ODD, GDN_VW, D), DEEPNORM_BETA * GDN_VW ** -0.5)
    return {'x': x,
            'ffn_a_gate': ffn_a_gate, 'ffn_a_up': ffn_a_up, 'ffn_a_down': ffn_a_down,
            'ffn_b_gate': ffn_b_gate, 'ffn_b_up': ffn_b_up, 'ffn_b_down': ffn_b_down,
            'ln_g': ln_g, 'ln_b': ln_b,
            'ab_w_in': ab_w_in, 'gla_w_lr': gla_w_lr, 'gla_b_lr': gla_b_lr, 'gla_norm_g': gla_norm_g,
            's5_lam_re': s5_lam_re, 's5_lam_im': s5_lam_im, 's5_b_re': s5_b_re, 's5_b_im': s5_b_im,
            's5_c_re': s5_c_re, 's5_c_im': s5_c_im, 's5_d': s5_d, 's5_log_step': s5_log_step,
            's5_glu_w': s5_glu_w, 's5_glu_b': s5_glu_b, 'ab_w_out': ab_w_out,
            'gdn_w_in': gdn_w_in, 'gdn_conv_w': gdn_conv_w, 'gdn_a_log': gdn_a_log,
            'gdn_dt_bias': gdn_dt_bias, 'gdn_norm_g': gdn_norm_g, 'gdn_w_out': gdn_w_out}


def reference(x, ffn_a_gate, ffn_a_up, ffn_a_down, ffn_b_gate, ffn_b_up, ffn_b_down, ln_g, ln_b,
              ab_w_in, gla_w_lr, gla_b_lr, gla_norm_g, s5_lam_re, s5_lam_im, s5_b_re, s5_b_im,
              s5_c_re, s5_c_im, s5_d, s5_log_step, s5_glu_w, s5_glu_b, ab_w_out,
              gdn_w_in, gdn_conv_w, gdn_a_log, gdn_dt_bias, gdn_norm_g, gdn_w_out):
    for layer in range(DEPTH):
        ffn = swiglu(x, ffn_a_gate[layer], ffn_a_up[layer], ffn_a_down[layer])
        x = layer_norm(DEEPNORM_ALPHA * x + MACARON_WEIGHT * ffn, ln_g[layer, 0], ln_b[layer, 0])
        i = layer // 2
        if layer % 2 == 0:
            mix = gla_s5_mixer(x, ab_w_in[i], gla_w_lr[i], gla_b_lr[i], gla_norm_g[i],
                               s5_lam_re[i], s5_lam_im[i], s5_b_re[i], s5_b_im[i],
                               s5_c_re[i], s5_c_im[i], s5_d[i], s5_log_step[i],
                               s5_glu_w[i], s5_glu_b[i], ab_w_out[i])
        else:
            mix = gdn_mixer(x, gdn_w_in[i], gdn_conv_w[i], gdn_a_log[i], gdn_dt_bias[i],
                            gdn_norm_g[i], gdn_w_out[i])
        x = layer_norm(DEEPNORM_ALPHA * x + mix, ln_g[layer, 1], ln_b[layer, 1])
        ffn = swiglu(x, ffn_b_gate[layer], ffn_b_up[layer], ffn_b_down[layer])
        x = layer_norm(DEEPNORM_ALPHA * x + MACARON_WEIGHT * ffn, ln_g[layer, 2], ln_b[layer, 2])
    return x
```

```python
import functools
import math

import jax
import jax.numpy as jnp
import numpy as np
from jax import lax
from jax.experimental import pallas as pl
from jax.experimental.pallas import tpu as pltpu

F32 = jnp.float32
BF16 = jnp.bfloat16

LANES = 128
SUBLANES = 8
VMEM_LIMIT_BYTES = 60 * 1024 * 1024

D_MODEL = 2048
DEPTH = 2
ALPHA = (2.0 * DEPTH) ** 0.25
MACARON = 0.5
LN_EPS = 1e-5
RMS_EPS = 1e-6
L2_EPS = 1e-6
D_FF = 5504
D_FF_PAD = 5632

GLA_HEADS = 4
GLA_DK = 128
GLA_DV = 256
GLA_RANK = 16
GLA_TAU = 16.0
CHUNK = 64
GLA_KW = GLA_HEADS * GLA_DK
GLA_VW = GLA_HEADS * GLA_DV

S5_WIDTH = 1024
S5_GROUP = 16
S5_GROUPS = 64
S5_STATE = 64
S5_TILE_GROUPS = LANES // S5_GROUP
S5_TILES = S5_WIDTH // LANES
S5_TILE_STATES = S5_TILE_GROUPS * S5_STATE

AB_K = GLA_KW
AB_V = AB_K + GLA_KW
AB_G = AB_V + GLA_VW
AB_A = AB_G + GLA_VW
AB_U = AB_A + GLA_RANK
AB_IN = AB_U + S5_WIDTH

GDN_QK_HEADS = 16
GDN_V_HEADS = 32
GDN_DK = 128
GDN_DV = 128
GDN_KW = GDN_QK_HEADS * GDN_DK
GDN_VW = GDN_V_HEADS * GDN_DV
GDN_QKV = 2 * GDN_KW + GDN_VW
GDN_Z = GDN_QKV
GDN_B = GDN_Z + GDN_VW
GDN_A = GDN_B + GDN_V_HEADS
GDN_IN = GDN_A + GDN_V_HEADS


def _params(*sem):
    return pltpu.CompilerParams(dimension_semantics=sem, vmem_limit_bytes=VMEM_LIMIT_BYTES)


def _dot(a, b):
    return jnp.dot(a, b, preferred_element_type=F32)


def _dot_nt(a, b):
    return lax.dot_general(a, b, (((1,), (1,)), ((), ())), preferred_element_type=F32)


def _dot_tn(a, b):
    return lax.dot_general(a, b, (((0,), (0,)), ((), ())), preferred_element_type=F32)


def _layer_norm(y, g, b):
    mu = jnp.mean(y, axis=-1, keepdims=True)
    yc = y - mu
    var = jnp.mean(yc * yc, axis=-1, keepdims=True)
    return yc * lax.rsqrt(var + LN_EPS) * g + b


def _silu(v):
    return v * jax.nn.sigmoid(v)


def _softplus(v):
    return jnp.maximum(v, 0.0) + jnp.log1p(jnp.exp(-jnp.abs(v)))


def _split3(v):
    hi = v.astype(BF16)
    r1 = v - hi.astype(F32)
    mid = r1.astype(BF16)
    lo = (r1 - mid.astype(F32)).astype(BF16)
    return hi, mid, lo


def _tril_mask(n, strict=False):
    r = lax.broadcasted_iota(jnp.int32, (n, n), 0)
    c = lax.broadcasted_iota(jnp.int32, (n, n), 1)
    return (c < r) if strict else (c <= r)


def _chunk_cumsum(v, rows):
    r = lax.broadcasted_iota(jnp.int32, (rows, rows), 0)
    c = lax.broadcasted_iota(jnp.int32, (rows, rows), 1)
    tri = ((c <= r) & ((c // CHUNK) == (r // CHUNK))).astype(BF16)
    hi, mid, lo = _split3(v)
    return _dot(tri, hi) + _dot(tri, mid) + _dot(tri, lo)


def _ffn_body(x_ref, wg_ref, wu_ref, wd_ref, g_ref, b_ref, o_ref, ob_ref, xb_ref, *, nj):
    j = pl.program_id(1)

    @pl.when(j == 0)
    def _():
        xb_ref[...] = x_ref[...].astype(BF16)
        o_ref[...] = jnp.zeros_like(o_ref)

    xb = xb_ref[...]
    hg = _dot(xb, wg_ref[...])
    hu = _dot(xb, wu_ref[...])
    h = (_silu(hg) * hu).astype(BF16)
    o_ref[...] += _dot(h, wd_ref[...])

    @pl.when(j == nj - 1)
    def _():
        y = ALPHA * x_ref[...] + MACARON * o_ref[...]
        yn = _layer_norm(y, g_ref[...], b_ref[...])
        o_ref[...] = yn
        ob_ref[...] = yn.astype(BF16)


def _ffn_ln(x, wg, wu, wd, g, b, *, tm, tf):
    t, d = x.shape
    f = wg.shape[1]
    nj = f // tf
    return pl.pallas_call(
        functools.partial(_ffn_body, nj=nj),
        grid=(t // tm, nj),
        in_specs=[
            pl.BlockSpec((tm, d), lambda i, j: (i, 0)),
            pl.BlockSpec((d, tf), lambda i, j: (0, j)),
            pl.BlockSpec((d, tf), lambda i, j: (0, j)),
            pl.BlockSpec((tf, d), lambda i, j: (j, 0)),
            pl.BlockSpec((1, d), lambda i, j: (0, 0)),
            pl.BlockSpec((1, d), lambda i, j: (0, 0)),
        ],
        out_specs=[
            pl.BlockSpec((tm, d), lambda i, j: (i, 0)),
            pl.BlockSpec((tm, d), lambda i, j: (i, 0)),
        ],
        out_shape=[jax.ShapeDtypeStruct((t, d), F32), jax.ShapeDtypeStruct((t, d), BF16)],
        scratch_shapes=[pltpu.VMEM((tm, d), BF16)],
        compiler_params=_params("arbitrary", "arbitrary"),
        name="ffn_ln",
    )(x, wg, wu, wd, g, b)


def _proj_body(x_ref, w_ref, o_ref):
    o_ref[...] = _dot(x_ref[...], w_ref[...])


def _proj(xb, w, *, tm, tn):
    t, k = xb.shape
    n = w.shape[1]
    return pl.pallas_call(
        _proj_body,
        grid=(n // tn, t // tm),
        in_specs=[
            pl.BlockSpec((tm, k), lambda j, i: (i, 0)),
            pl.BlockSpec((k, tn), lambda j, i: (0, j)),
        ],
        out_specs=pl.BlockSpec((tm, tn), lambda j, i: (i, j)),
        out_shape=jax.ShapeDtypeStruct((t, n), F32),
        compiler_params=_params("arbitrary", "arbitrary"),
        name="proj",
    )(xb, w)


def _conv_proj_body(x_ref, w_ref, cw_ref, o_ref, carry_ref, *, tm, tn, tiles_per_seq,
                    l2_heads, n_scaled_tiles, scale):
    jt = pl.program_id(0)
    i = pl.program_id(1)

    @pl.when(i % tiles_per_seq == 0)
    def _():
        carry_ref[...] = jnp.zeros_like(carry_ref)

    raw = _dot(x_ref[...], w_ref[...])
    cw = cw_ref[...]
    prev = carry_ref[...]
    acc = raw * cw[3:4]
    for d in (1, 2, 3):
        acc = acc + pltpu.roll(raw, d, 0) * cw[3 - d:4 - d]
    head = raw[0:SUBLANES]
    row = lax.broadcasted_iota(jnp.int32, (SUBLANES, tn), 0)
    first = head * cw[3:4]
    for d in (1, 2, 3):
        shifted = jnp.where(row < d, pltpu.roll(prev, d, 0), pltpu.roll(head, d, 0))
        first = first + shifted * cw[3 - d:4 - d]
    carry_ref[...] = raw[tm - SUBLANES:tm]

    if l2_heads:
        fac = jnp.where(jt < n_scaled_tiles, jnp.float32(scale), jnp.float32(1.0))

    def post(v, rows):
        v = _silu(v)
        if not l2_heads:
            return [(0, tn, v)]
        out = []
        for h in range(tn // LANES):
            vh = v[:, h * LANES:(h + 1) * LANES]
            ss = jnp.sum(vh * vh, axis=-1, keepdims=True)
            out.append((h * LANES, LANES, vh * lax.rsqrt(ss + L2_EPS) * fac))
        return out

    for off, width, val in post(acc, tm):
        o_ref[:, off:off + width] = val
    for off, width, val in post(first, SUBLANES):
        o_ref[0:SUBLANES, off:off + width] = val


def _conv_proj(xb, w, cw, *, tm, tn, seq, l2_heads=False, n_scaled_tiles=0, scale=1.0):
    t, k = xb.shape
    n = w.shape[1]
    body = functools.partial(_conv_proj_body, tm=tm, tn=tn, tiles_per_seq=seq // tm,
                             l2_heads=l2_heads, n_scaled_tiles=n_scaled_tiles, scale=scale)
    return pl.pallas_call(
        body,
        grid=(n // tn, t // tm),
        in_specs=[
            pl.BlockSpec((tm, k), lambda j, i: (i, 0)),
            pl.BlockSpec((k, tn), lambda j, i: (0, j)),
            pl.BlockSpec((4, tn), lambda j, i: (0, j)),
        ],
        out_specs=pl.BlockSpec((tm, tn), lambda j, i: (i, j)),
        out_shape=jax.ShapeDtypeStruct((t, n), F32),
        scratch_shapes=[pltpu.VMEM((SUBLANES, tn), F32)],
        compiler_params=_params("arbitrary", "arbitrary"),
        name="conv_proj",
    )(xb, w, cw)


def _gate_proj_body(x_ref, w_ref, alog_ref, dtb_ref, bg_ref, gc_ref, *, tm):
    raw = _dot(x_ref[...], w_ref[...])
    lane = lax.broadcasted_iota(jnp.int32, raw.shape, 1)
    beta = jax.nn.sigmoid(raw)
    g = -jnp.exp(alog_ref[...]) * _softplus(raw + dtb_ref[...])
    g = jnp.where((lane >= GDN_V_HEADS) & (lane < 2 * GDN_V_HEADS), g, 0.0)
    bg_ref[...] = jnp.where(lane < GDN_V_HEADS, beta, g)
    gc_ref[...] = _chunk_cumsum(g, tm)


def _gate_proj(xb, w, alog, dtb, *, tm):
    t, k = xb.shape
    return pl.pallas_call(
        functools.partial(_gate_proj_body, tm=tm),
        grid=(t // tm,),
        in_specs=[
            pl.BlockSpec((tm, k), lambda i: (i, 0)),
            pl.BlockSpec((k, LANES), lambda i: (0, 0)),
            pl.BlockSpec((1, LANES), lambda i: (0, 0)),
            pl.BlockSpec((1, LANES), lambda i: (0, 0)),
        ],
        out_specs=[pl.BlockSpec((tm, LANES), lambda i: (i, 0))] * 2,
        out_shape=[jax.ShapeDtypeStruct((t, LANES), F32)] * 2,
        compiler_params=_params("arbitrary"),
        name="gate_proj",
    )(xb, w, alog, dtb)


def _gla_body(q_ref, k_ref, v_ref, go_ref, a_ref, wlr_ref, blr_ref, ng_ref, o_ref, st_ref, *, rows):
    @pl.when(pl.program_id(2) == 0)
    def _():
        st_ref[...] = jnp.zeros_like(st_ref)

    causal = _tril_mask(CHUNK)
    wlr = wlr_ref[...]
    blr = blr_ref[...]
    ng = ng_ref[...]
    for c in range(rows // CHUNK):
        sl = slice(c * CHUNK, (c + 1) * CHUNK)
        z = _dot(a_ref[sl, :].astype(BF16), wlr) + blr
        log_a = -_softplus(-z) / GLA_TAU
        b = _chunk_cumsum(log_a, CHUNK)
        b_last = b[CHUNK - 1:CHUNK, :]
        q = q_ref[sl, :] * (GLA_DK ** -0.5)
        k = k_ref[sl, :]
        v = v_ref[sl, :].astype(BF16)
        q_dec = (q * jnp.exp(b)).astype(BF16)
        k_inv = (k * jnp.exp(-b)).astype(BF16)
        k_end = (k * jnp.exp(b_last - b)).astype(BF16)
        scores = jnp.where(causal, _dot_nt(q_dec, k_inv), 0.0)
        st = st_ref[...]
        o = _dot(scores.astype(BF16), v) + _dot_nt(q_dec, st.astype(BF16))
        st_ref[...] = st * jnp.exp(b_last) + _dot_tn(v, k_end)
        on = o * lax.rsqrt(jnp.mean(o * o, axis=-1, keepdims=True) + RMS_EPS) * ng
        o_ref[sl, :] = (on * _silu(go_ref[sl, :])).astype(BF16)


def _gla(proj, gate, wlr, blr, ng, *, batch, seq, rows):
    t = batch * seq
    nblk = seq // rows
    row = lambda b, h, n: b * nblk + n
    nk = GLA_KW // GLA_DK
    return pl.pallas_call(
        functools.partial(_gla_body, rows=rows),
        grid=(batch, GLA_HEADS, nblk),
        in_specs=[
            pl.BlockSpec((rows, GLA_DK), lambda b, h, n: (row(b, h, n), h)),
            pl.BlockSpec((rows, GLA_DK), lambda b, h, n: (row(b, h, n), nk + h)),
            pl.BlockSpec((rows, GLA_DV), lambda b, h, n: (row(b, h, n), AB_V // GLA_DV + h)),
            pl.BlockSpec((rows, GLA_DV), lambda b, h, n: (row(b, h, n), AB_G // GLA_DV + h)),
            pl.BlockSpec((rows, LANES), lambda b, h, n: (row(b, h, n), 0)),
            pl.BlockSpec((LANES, GLA_DK), lambda b, h, n: (0, h)),
            pl.BlockSpec((1, GLA_DK), lambda b, h, n: (0, h)),
            pl.BlockSpec((1, GLA_DV), lambda b, h, n: (0, 0)),
        ],
        out_specs=pl.BlockSpec((rows, GLA_DV), lambda b, h, n: (row(b, h, n), h)),
        out_shape=jax.ShapeDtypeStruct((t, GLA_VW), BF16),
        scratch_shapes=[pltpu.VMEM((GLA_DV, GLA_DK), F32)],
        compiler_params=_params("arbitrary", "arbitrary", "arbitrary"),
        name="gla",
    )(proj, proj, proj, proj, gate, wlr, blr, ng)


def _s5_body(u_ref, bbd_ref, cbd_ref, pw_ref, d_ref, y_ref, xs_ref, cr_ref, ci_ref, *, rows):
    @pl.when(pl.program_id(2) == 0)
    def _():
        cr_ref[...] = jnp.zeros_like(cr_ref)
        ci_ref[...] = jnp.zeros_like(ci_ref)

    ns = S5_TILE_STATES
    u = u_ref[...]
    bu = _dot(u.astype(BF16), bbd_ref[0])
    levels = [(pw_ref[0, 2 * l], pw_ref[0, 2 * l + 1], 1 << l) for l in range(3)]
    qr, qi = pw_ref[0, 6], pw_ref[0, 7]
    cr = cr_ref[...]
    ci = ci_ref[...]
    for blk in range(rows // SUBLANES):
        sl = slice(blk * SUBLANES, (blk + 1) * SUBLANES)
        xr = bu[sl, :ns]
        xi = bu[sl, ns:]
        for pr, pi, s in levels:
            sr = pltpu.roll(xr, s, 0)
            si = pltpu.roll(xi, s, 0)
            xr, xi = xr + (pr * sr - pi * si), xi + (pr * si + pi * sr)
        xr, xi = xr + (qr * cr - qi * ci), xi + (qr * ci + qi * cr)
        cr = jnp.broadcast_to(xr[SUBLANES - 1:SUBLANES], (SUBLANES, ns))
        ci = jnp.broadcast_to(xi[SUBLANES - 1:SUBLANES], (SUBLANES, ns))
        xs_ref[sl, :ns] = xr
        xs_ref[sl, ns:] = xi
    cr_ref[...] = cr
    ci_ref[...] = ci
    y = _dot(xs_ref[...].astype(BF16), cbd_ref[0]) + d_ref[...] * u
    y3 = y * y * y
    y_ref[...] = y * (0.5 * (1.0 + jnp.tanh(np.float32(math.sqrt(2.0 / math.pi)) * (y + 0.044715 * y3))))


def _s5(proj, bbd, cbd, pw, d, *, batch, seq, rows, u_col0):
    t = batch * seq
    nblk = seq // rows
    ns = S5_TILE_STATES
    return pl.pallas_call(
        functools.partial(_s5_body, rows=rows),
        grid=(batch, S5_TILES, nblk),
        in_specs=[
            pl.BlockSpec((rows, LANES), lambda b, j, n: (b * nblk + n, u_col0 + j)),
            pl.BlockSpec((1, LANES, 2 * ns), lambda b, j, n: (j, 0, 0)),
            pl.BlockSpec((1, 2 * ns, LANES), lambda b, j, n: (j, 0, 0)),
            pl.BlockSpec((1, 8, SUBLANES, ns), lambda b, j, n: (j, 0, 0, 0)),
            pl.BlockSpec((1, LANES), lambda b, j, n: (0, j)),
        ],
        out_specs=pl.BlockSpec((rows, LANES), lambda b, j, n: (b * nblk + n, j)),
        out_shape=jax.ShapeDtypeStruct((t, S5_WIDTH), F32),
        scratch_shapes=[pltpu.VMEM((rows, 2 * ns), F32),
                        pltpu.VMEM((SUBLANES, ns), F32),
                        pltpu.VMEM((SUBLANES, ns), F32)],
        compiler_params=_params("arbitrary", "arbitrary", "arbitrary"),
        name="s5",
    )(proj, bbd, cbd, pw, d)


def _s5_tables(lam_re, lam_im, b_re, b_im, c_re, c_im, log_step):
    step = jnp.exp(log_step)[:, None]
    mag = jnp.exp(lam_re * step)
    a_re = mag * jnp.cos(lam_im * step)
    a_im = mag * jnp.sin(lam_im * step)
    inv_den = 1.0 / (lam_re * lam_re + lam_im * lam_im)
    f_re = ((a_re - 1.0) * lam_re + a_im * lam_im) * inv_den
    f_im = (a_im * lam_re - (a_re - 1.0) * lam_im) * inv_den
    bb_re = f_re[..., None] * b_re - f_im[..., None] * b_im
    bb_im = f_re[..., None] * b_im + f_im[..., None] * b_re
    tg, nt = S5_TILE_GROUPS, S5_TILES
    eye = jnp.eye(tg, dtype=F32)

    def in_table(bb):
        bb = bb.reshape(nt, tg, S5_STATE, S5_GROUP)
        return jnp.einsum('jgph,gk->jghkp', bb, eye).reshape(nt, LANES, S5_TILE_STATES)

    def out_table(cc):
        cc = cc.reshape(nt, tg, S5_GROUP, S5_STATE)
        return jnp.einsum('jghp,gk->jgpkh', cc, eye).reshape(nt, S5_TILE_STATES, LANES)

    bbd = jnp.concatenate([in_table(bb_re), in_table(bb_im)], axis=2).astype(BF16)
    cbd = jnp.concatenate([out_table(c_re), -out_table(c_im)], axis=1).astype(BF16)

    def cmul(x, y):
        return x[0] * y[0] - x[1] * y[1], x[0] * y[1] + x[1] * y[0]

    a1 = (a_re.reshape(nt, 1, S5_TILE_STATES), a_im.reshape(nt, 1, S5_TILE_STATES))
    powers = [a1]
    for _ in range(SUBLANES - 1):
        powers.append(cmul(powers[-1], a1))
    row = jnp.arange(SUBLANES)[None, :, None]
    tabs = []
    for s in (1, 2, 4):
        pr, pi = powers[s - 1]
        tabs.append(jnp.where(row >= s, pr, 0.0))
        tabs.append(jnp.where(row >= s, pi, 0.0))
    tabs.append(jnp.concatenate([p[0] for p in powers], axis=1))
    tabs.append(jnp.concatenate([p[1] for p in powers], axis=1))
    pw = jnp.stack([jnp.broadcast_to(tb, (nt, SUBLANES, S5_TILE_STATES)) for tb in tabs], axis=1)
    return bbd, cbd, pw.astype(F32)


def _mix0_out_body(og_ref, ys_ref, gw_ref, gb_ref, w1_ref, w2_ref, x_ref, g_ref, b_ref, o_ref):
    ys = ys_ref[...]
    gate = jax.nn.sigmoid(_dot(ys.astype(BF16), gw_ref[...]) + gb_ref[...])
    os5 = (ys * gate).astype(BF16)
    mix = _dot(og_ref[...], w1_ref[...]) + _dot(os5, w2_ref[...])
    o_ref[...] = _layer_norm(ALPHA * x_ref[...] + mix, g_ref[...], b_ref[...])


def _mix0_out(og, ys, gw, gb, w1, w2, x, g, b, *, tm):
    t, d = x.shape
    const = lambda shape: pl.BlockSpec(shape, lambda i: (0, 0))
    return pl.pallas_call(
        _mix0_out_body,
        grid=(t // tm,),
        in_specs=[
            pl.BlockSpec((tm, GLA_VW), lambda i: (i, 0)),
            pl.BlockSpec((tm, S5_WIDTH), lambda i: (i, 0)),
            const((S5_WIDTH, S5_WIDTH)), const((1, S5_WIDTH)),
            const((GLA_VW, d)), const((S5_WIDTH, d)),
            pl.BlockSpec((tm, d), lambda i: (i, 0)),
            const((1, d)), const((1, d)),
        ],
        out_specs=pl.BlockSpec((tm, d), lambda i: (i, 0)),
        out_shape=jax.ShapeDtypeStruct((t, d), F32),
        compiler_params=_params("arbitrary"),
        name="mix0_out",
    )(og, ys, gw, gb, w1, w2, x, g, b)


GDN_REP = GDN_V_HEADS // GDN_QK_HEADS


def _unit_lower_inverse(a):
    n = a.shape[0]
    r = lax.broadcasted_iota(jnp.int32, (n, n), 0)
    c = lax.broadcasted_iota(jnp.int32, (n, n), 1)
    x = jnp.where(r == c, 1.0, 0.0) - a
    p = a
    k = 1
    while 2 * k < n:
        pb = p.astype(BF16)
        p = _dot(pb, pb)
        x = x + _dot(x.astype(BF16), p.astype(BF16))
        k *= 2
    return x


def _gdn_body(q_ref, k_ref, v_ref, z_ref, bg_ref, gc_ref, gr_ref, ng_ref, o_ref, st_ref, *, rows):
    hq = pl.program_id(1)

    @pl.when(pl.program_id(2) == 0)
    def _():
        st_ref[...] = jnp.zeros_like(st_ref)

    incl = _tril_mask(CHUNK)
    strict = _tril_mask(CHUNK, strict=True)
    lane = lax.broadcasted_iota(jnp.int32, (CHUNK, LANES), 1)
    ng = ng_ref[...]

    def chunk(c, carry):
        r0 = pl.multiple_of(c * CHUNK, CHUNK)
        k = k_ref[pl.ds(r0, CHUNK), :]
        q = q_ref[pl.ds(r0, CHUNK), :]
        kb = k.astype(BF16)
        prod = _dot_nt(jnp.concatenate([kb, q.astype(BF16)], axis=0), kb)
        kk = prod[:CHUNK]
        qk = prod[CHUNK:]
        bg = bg_ref[pl.ds(r0, CHUNK), :]
        gcb = gc_ref[pl.ds(r0, CHUNK), :]
        for hh in range(GDN_REP):
            hv = hq * GDN_REP + hh
            beta = jnp.sum(jnp.where(lane == hv, bg, 0.0), axis=-1, keepdims=True)
            gc = jnp.sum(jnp.where(lane == hv + GDN_V_HEADS, gcb, 0.0), axis=-1, keepdims=True)
            gc_row = gr_ref[hh, pl.ds(c, 1), :]
            g_last = gc[CHUNK - 1:CHUNK, :]
            decay = jnp.exp(jnp.where(incl, gc - gc_row, -jnp.inf))
            a = jnp.where(strict, kk * beta * decay, 0.0)
            tinv = _unit_lower_inverse(a)
            egc = jnp.exp(gc)
            v = v_ref[pl.ds(r0, CHUNK), hh * GDN_DV:(hh + 1) * GDN_DV]
            kbeta = k * beta
            rhs = jnp.concatenate([v * beta, kbeta * egc], axis=1).astype(BF16)
            sol = _dot(tinv.astype(BF16), rhs)
            u_val = sol[:, :GDN_DV]
            w_dec = sol[:, GDN_DV:]
            st = st_ref[hh]
            lhs = jnp.concatenate([w_dec, q * egc], axis=0).astype(BF16)
            ws = _dot(lhs, st.astype(BF16))
            v_new = (u_val - ws[:CHUNK]).astype(BF16)
            o = ws[CHUNK:] + _dot((qk * decay).astype(BF16), v_new)
            k_end = (k * jnp.exp(g_last - gc)).astype(BF16)
            st_ref[hh] = jnp.exp(g_last) * st + _dot_tn(k_end, v_new)
            on = o * lax.rsqrt(jnp.mean(o * o, axis=-1, keepdims=True) + RMS_EPS) * ng
            z = z_ref[pl.ds(r0, CHUNK), hh * GDN_DV:(hh + 1) * GDN_DV]
            o_ref[pl.ds(r0, CHUNK), hh * GDN_DV:(hh + 1) * GDN_DV] = (on * _silu(z)).astype(BF16)
        return carry

    lax.fori_loop(0, rows // CHUNK, chunk, 0)


def _gdn(qk, v, z, bg, gc, gc_rows, ng, *, batch, seq, rows):
    t = batch * seq
    nblk = seq // rows
    cpb = rows // CHUNK
    row = lambda b, h, n: b * nblk + n
    vw = GDN_REP * GDN_DV
    return pl.pallas_call(
        functools.partial(_gdn_body, rows=rows),
        grid=(batch, GDN_QK_HEADS, nblk),
        in_specs=[
            pl.BlockSpec((rows, GDN_DK), lambda b, h, n: (row(b, h, n), h)),
            pl.BlockSpec((rows, GDN_DK), lambda b, h, n: (row(b, h, n), GDN_QK_HEADS + h)),
            pl.BlockSpec((rows, vw), lambda b, h, n: (row(b, h, n), h)),
            pl.BlockSpec((rows, vw), lambda b, h, n: (row(b, h, n), h)),
            pl.BlockSpec((rows, LANES), lambda b, h, n: (row(b, h, n), 0)),
            pl.BlockSpec((rows, LANES), lambda b, h, n: (row(b, h, n), 0)),
            pl.BlockSpec((GDN_REP, cpb, CHUNK), lambda b, h, n: (h, row(b, h, n), 0)),
            pl.BlockSpec((1, GDN_DV), lambda b, h, n: (0, 0)),
        ],
        out_specs=pl.BlockSpec((rows, vw), lambda b, h, n: (row(b, h, n), h)),
        out_shape=jax.ShapeDtypeStruct((t, GDN_VW), BF16),
        scratch_shapes=[pltpu.VMEM((GDN_REP, GDN_DK, GDN_DV), F32)],
        compiler_params=_params("arbitrary", "arbitrary", "arbitrary"),
        name="gdn",
    )(qk, qk, v, z, bg, gc, gc_rows, ng)


def _out_ln_body(a_ref, w_ref, x_ref, g_ref, b_ref, o_ref, *, nk):
    kk = pl.program_id(1)

    @pl.when(kk == 0)
    def _():
        o_ref[...] = jnp.zeros_like(o_ref)

    o_ref[...] += _dot(a_ref[...], w_ref[...])

    @pl.when(kk == nk - 1)
    def _():
        o_ref[...] = _layer_norm(ALPHA * x_ref[...] + o_ref[...], g_ref[...], b_ref[...])


def _out_ln(a, w, x, g, b, *, tm, tk):
    t, d = x.shape
    nk = a.shape[1] // tk
    return pl.pallas_call(
        functools.partial(_out_ln_body, nk=nk),
        grid=(t // tm, nk),
        in_specs=[
            pl.BlockSpec((tm, tk), lambda i, k: (i, k)),
            pl.BlockSpec((tk, d), lambda i, k: (k, 0)),
            pl.BlockSpec((tm, d), lambda i, k: (i, 0)),
            pl.BlockSpec((1, d), lambda i, k: (0, 0)),
            pl.BlockSpec((1, d), lambda i, k: (0, 0)),
        ],
        out_specs=pl.BlockSpec((tm, d), lambda i, k: (i, 0)),
        out_shape=jax.ShapeDtypeStruct((t, d), F32),
        compiler_params=_params("arbitrary", "arbitrary"),
        name="out_ln",
    )(a, w, x, g, b)


def _tiles(batch, seq):
    tm = min(512, seq)
    return dict(tm=tm, gla_rows=min(256, seq), s5_rows=min(256, seq), gdn_rows=min(512, seq))


def _ffn(x, wg, wu, wd, g, b, tm):
    pad = D_FF_PAD - D_FF
    wg = jnp.pad(wg, ((0, 0), (0, pad))).astype(BF16)
    wu = jnp.pad(wu, ((0, 0), (0, pad))).astype(BF16)
    wd = jnp.pad(wd, ((0, pad), (0, 0))).astype(BF16)
    return _ffn_ln(x, wg, wu, wd, g[None], b[None], tm=tm, tf=512)


def _pad_cols(w, n):
    return jnp.pad(w, ((0, 0), (0, n - w.shape[1])))


def _gla_s5_layer(x, xb, p, i, g, b, batch, seq, tl):
    w_in = p['ab_w_in'][i]
    w_main = jnp.concatenate([w_in[:, :AB_A], w_in[:, AB_U:]], axis=1).astype(BF16)
    w_gate = _pad_cols(w_in[:, AB_A:AB_U], LANES).astype(BF16)
    proj = _proj(xb, w_main, tm=tl['tm'], tn=1024)
    gate = _proj(xb, w_gate, tm=tl['tm'], tn=LANES)
    wlr = jnp.pad(p['gla_w_lr'][i], ((0, LANES - GLA_RANK), (0, 0))).astype(BF16)
    o_gla = _gla(proj, gate, wlr, p['gla_b_lr'][i][None], p['gla_norm_g'][i][None],
                 batch=batch, seq=seq, rows=tl['gla_rows'])
    bbd, cbd, pw = _s5_tables(p['s5_lam_re'][i], p['s5_lam_im'][i], p['s5_b_re'][i], p['s5_b_im'][i],
                              p['s5_c_re'][i], p['s5_c_im'][i], p['s5_log_step'][i])
    ys = _s5(proj, bbd, cbd, pw, p['s5_d'][i][None], batch=batch, seq=seq, rows=tl['s5_rows'],
             u_col0=AB_A // LANES)
    w_out = p['ab_w_out'][i].astype(BF16)
    return _mix0_out(o_gla, ys, p['s5_glu_w'][i].astype(BF16), p['s5_glu_b'][i][None],
                     w_out[:GLA_VW], w_out[GLA_VW:], x, g[None], b[None], tm=tl['tm'])


def _gdn_layer(x, xb, p, i, g, b, batch, seq, tl):
    t = batch * seq
    w_in = p['gdn_w_in'][i]
    conv_w = p['gdn_conv_w'][i]
    tm = tl['tm']
    qk = _conv_proj(xb, w_in[:, :2 * GDN_KW].astype(BF16), conv_w[:, :2 * GDN_KW], tm=tm, tn=1024,
                    seq=seq, l2_heads=True, n_scaled_tiles=GDN_KW // 1024, scale=GDN_DK ** -0.5)
    v = _conv_proj(xb, w_in[:, 2 * GDN_KW:GDN_QKV].astype(BF16), conv_w[:, 2 * GDN_KW:], tm=tm,
                   tn=1024, seq=seq)
    z = _proj(xb, w_in[:, GDN_Z:GDN_B].astype(BF16), tm=tm, tn=1024)
    w_gate = _pad_cols(w_in[:, GDN_B:GDN_IN], LANES).astype(BF16)
    lane_pad = lambda a: jnp.pad(a, (GDN_V_HEADS, LANES - 2 * GDN_V_HEADS))[None]
    bg, gc = _gate_proj(xb, w_gate, lane_pad(p['gdn_a_log'][i]), lane_pad(p['gdn_dt_bias'][i]), tm=tm)
    gc_rows = gc[:, GDN_V_HEADS:2 * GDN_V_HEADS].reshape(t // CHUNK, CHUNK, GDN_V_HEADS).transpose(2, 0, 1)
    o = _gdn(qk, v, z, bg, gc, gc_rows, p['gdn_norm_g'][i][None], batch=batch, seq=seq,
             rows=tl['gdn_rows'])
    return _out_ln(o, p['gdn_w_out'][i].astype(BF16), x, g[None], b[None], tm=tm, tk=1024)


def kernel(x, ffn_a_gate, ffn_a_up, ffn_a_down, ffn_b_gate, ffn_b_up, ffn_b_down, ln_g, ln_b, ab_w_in, gla_w_lr, gla_b_lr, gla_norm_g, s5_lam_re, s5_lam_im, s5_b_re, s5_b_im, s5_c_re, s5_c_im, s5_d, s5_log_step, s5_glu_w, s5_glu_b, ab_w_out, gdn_w_in, gdn_conv_w, gdn_a_log, gdn_dt_bias, gdn_norm_g, gdn_w_out):
    batch, seq, d = x.shape
    assert d == D_MODEL and seq % CHUNK == 0
    p = dict(ab_w_in=ab_w_in, gla_w_lr=gla_w_lr, gla_b_lr=gla_b_lr, gla_norm_g=gla_norm_g,
             s5_lam_re=s5_lam_re, s5_lam_im=s5_lam_im, s5_b_re=s5_b_re, s5_b_im=s5_b_im,
             s5_c_re=s5_c_re, s5_c_im=s5_c_im, s5_d=s5_d, s5_log_step=s5_log_step,
             s5_glu_w=s5_glu_w, s5_glu_b=s5_glu_b, ab_w_out=ab_w_out, gdn_w_in=gdn_w_in,
             gdn_conv_w=gdn_conv_w, gdn_a_log=gdn_a_log, gdn_dt_bias=gdn_dt_bias,
             gdn_norm_g=gdn_norm_g, gdn_w_out=gdn_w_out)
    tl = _tiles(batch, seq)
    h = x.reshape(batch * seq, d).astype(F32)
    for layer in range(DEPTH):
        h, hb = _ffn(h, ffn_a_gate[layer], ffn_a_up[layer], ffn_a_down[layer],
                     ln_g[layer, 0], ln_b[layer, 0], tl['tm'])
        i = layer // 2
        mixer = _gla_s5_layer if layer % 2 == 0 else _gdn_layer
        h = mixer(h, hb, p, i, ln_g[layer, 1], ln_b[layer, 1], batch, seq, tl)
        h, _ = _ffn(h, ffn_b_gate[layer], ffn_b_up[layer], ffn_b_down[layer],
                    ln_g[layer, 2], ln_b[layer, 2], tl['tm'])
    return h.reshape(batch, seq, d)
```

```python
import functools
import math

import jax
import jax.numpy as jnp
import numpy as np
from jax import lax
from jax.experimental import pallas as pl
from jax.experimental.pallas import tpu as pltpu

F32 = jnp.float32
BF16 = jnp.bfloat16

LANES = 128
SUBLANES = 8
VMEM_LIMIT_BYTES = 60 * 1024 * 1024
LN_ROWS = 64

D_MODEL = 2048
DEPTH = 2
ALPHA = (2.0 * DEPTH) ** 0.25
MACARON = 0.5
LN_EPS = 1e-5
RMS_EPS = 1e-6
L2_EPS = 1e-6
D_FF = 5504
D_FF_PAD = 5632

GLA_HEADS = 4
GLA_DK = 128
GLA_DV = 256
GLA_RANK = 16
GLA_TAU = 16.0
CHUNK = 64
GLA_KW = GLA_HEADS * GLA_DK
GLA_VW = GLA_HEADS * GLA_DV

S5_WIDTH = 1024
S5_GROUP = 16
S5_GROUPS = 64
S5_STATE = 64
S5_TILE_GROUPS = LANES // S5_GROUP
S5_TILES = S5_WIDTH // LANES
S5_TILE_STATES = S5_TILE_GROUPS * S5_STATE

AB_K = GLA_KW
AB_V = AB_K + GLA_KW
AB_G = AB_V + GLA_VW
AB_A = AB_G + GLA_VW
AB_U = AB_A + GLA_RANK
AB_IN = AB_U + S5_WIDTH

GDN_QK_HEADS = 16
GDN_V_HEADS = 32
GDN_DK = 128
GDN_DV = 128
GDN_KW = GDN_QK_HEADS * GDN_DK
GDN_VW = GDN_V_HEADS * GDN_DV
GDN_QKV = 2 * GDN_KW + GDN_VW
GDN_Z = GDN_QKV
GDN_B = GDN_Z + GDN_VW
GDN_A = GDN_B + GDN_V_HEADS
GDN_IN = GDN_A + GDN_V_HEADS


def _params(*sem):
    return pltpu.CompilerParams(dimension_semantics=sem, vmem_limit_bytes=VMEM_LIMIT_BYTES)


def _dot(a, b):
    return jnp.dot(a, b, preferred_element_type=F32)


def _dot_nt(a, b):
    return lax.dot_general(a, b, (((1,), (1,)), ((), ())), preferred_element_type=F32)


def _dot_tn(a, b):
    return lax.dot_general(a, b, (((0,), (0,)), ((), ())), preferred_element_type=F32)


def _layer_norm(y, g, b):
    mu = jnp.mean(y, axis=-1, keepdims=True)
    yc = y - mu
    var = jnp.mean(yc * yc, axis=-1, keepdims=True)
    return yc * lax.rsqrt(var + LN_EPS) * g + b


def _silu(v):
    return v * jax.nn.sigmoid(v)


def _softplus(v):
    return jnp.maximum(v, 0.0) + jnp.log1p(jnp.exp(-jnp.abs(v)))


def _split3(v):
    hi = v.astype(BF16)
    r1 = v - hi.astype(F32)
    mid = r1.astype(BF16)
    lo = (r1 - mid.astype(F32)).astype(BF16)
    return hi, mid, lo


def _tril_mask(n, strict=False):
    r = lax.broadcasted_iota(jnp.int32, (n, n), 0)
    c = lax.broadcasted_iota(jnp.int32, (n, n), 1)
    return (c < r) if strict else (c <= r)


def _chunk_cumsum(v, rows):
    r = lax.broadcasted_iota(jnp.int32, (rows, rows), 0)
    c = lax.broadcasted_iota(jnp.int32, (rows, rows), 1)
    tri = ((c <= r) & ((c // CHUNK) == (r // CHUNK))).astype(BF16)
    hi, mid, lo = _split3(v)
    return _dot(tri, hi) + _dot(tri, mid) + _dot(tri, lo)


def _ffn_body(x_ref, wg_ref, wu_ref, wd_ref, g_ref, b_ref, o_ref, ob_ref, xb_ref, *, nj):
    j = pl.program_id(1)

    @pl.when(j == 0)
    def _():
        xb_ref[...] = x_ref[...].astype(BF16)
        o_ref[...] = jnp.zeros_like(o_ref)

    xb = xb_ref[...]
    hg = _dot(xb, wg_ref[...])
    hu = _dot(xb, wu_ref[...])
    h = (_silu(hg) * hu).astype(BF16)
    o_ref[...] += _dot(h, wd_ref[...])

    @pl.when(j == nj - 1)
    def _():
        g = g_ref[...]
        b = b_ref[...]

        def rows(r, carry):
            sl = pl.ds(pl.multiple_of(r * LN_ROWS, LN_ROWS), LN_ROWS)
            yn = _layer_norm(ALPHA * x_ref[sl, :] + MACARON * o_ref[sl, :], g, b)
            o_ref[sl, :] = yn
            ob_ref[sl, :] = yn.astype(BF16)
            return carry

        lax.fori_loop(0, o_ref.shape[0] // LN_ROWS, rows, 0)


def _ffn_ln(x, wg, wu, wd, g, b, *, tm, tf):
    t, d = x.shape
    f = wg.shape[1]
    nj = f // tf
    return pl.pallas_call(
        functools.partial(_ffn_body, nj=nj),
        grid=(t // tm, nj),
        in_specs=[
            pl.BlockSpec((tm, d), lambda i, j: (i, 0), pipeline_mode=pl.Buffered(1)),
            pl.BlockSpec((d, tf), lambda i, j: (0, j)),
            pl.BlockSpec((d, tf), lambda i, j: (0, j)),
            pl.BlockSpec((tf, d), lambda i, j: (j, 0)),
            pl.BlockSpec((1, d), lambda i, j: (0, 0)),
            pl.BlockSpec((1, d), lambda i, j: (0, 0)),
        ],
        out_specs=[
            pl.BlockSpec((tm, d), lambda i, j: (i, 0)),
            pl.BlockSpec((tm, d), lambda i, j: (i, 0)),
        ],
        out_shape=[jax.ShapeDtypeStruct((t, d), F32), jax.ShapeDtypeStruct((t, d), BF16)],
        scratch_shapes=[pltpu.VMEM((tm, d), BF16)],
        compiler_params=_params("arbitrary", "arbitrary"),
        name="ffn_ln",
    )(x, wg, wu, wd, g, b)


def _proj_body(x_ref, w_ref, o_ref):
    o_ref[...] = _dot(x_ref[...], w_ref[...])


def _proj(xb, w, *, tm, tn):
    t, k = xb.shape
    n = w.shape[1]
    return pl.pallas_call(
        _proj_body,
        grid=(n // tn, t // tm),
        in_specs=[
            pl.BlockSpec((tm, k), lambda j, i: (i, 0)),
            pl.BlockSpec((k, tn), lambda j, i: (0, j)),
        ],
        out_specs=pl.BlockSpec((tm, tn), lambda j, i: (i, j)),
        out_shape=jax.ShapeDtypeStruct((t, n), F32),
        compiler_params=_params("arbitrary", "arbitrary"),
        name="proj",
    )(xb, w)


def _conv_proj_body(x_ref, w_ref, cw_ref, o_ref, carry_ref, *, tm, tn, tiles_per_seq,
                    l2_heads, n_scaled_tiles, scale):
    jt = pl.program_id(0)
    i = pl.program_id(1)

    @pl.when(i % tiles_per_seq == 0)
    def _():
        carry_ref[...] = jnp.zeros_like(carry_ref)

    raw = _dot(x_ref[...], w_ref[...])
    cw = cw_ref[...]
    prev = carry_ref[...]
    acc = raw * cw[3:4]
    for d in (1, 2, 3):
        acc = acc + pltpu.roll(raw, d, 0) * cw[3 - d:4 - d]
    head = raw[0:SUBLANES]
    row = lax.broadcasted_iota(jnp.int32, (SUBLANES, tn), 0)
    first = head * cw[3:4]
    for d in (1, 2, 3):
        shifted = jnp.where(row < d, pltpu.roll(prev, d, 0), pltpu.roll(head, d, 0))
        first = first + shifted * cw[3 - d:4 - d]
    carry_ref[...] = raw[tm - SUBLANES:tm]

    if l2_heads:
        fac = jnp.where(jt < n_scaled_tiles, jnp.float32(scale), jnp.float32(1.0))

    def post(v, rows):
        v = _silu(v)
        if not l2_heads:
            return [(0, tn, v)]
        out = []
        for h in range(tn // LANES):
            vh = v[:, h * LANES:(h + 1) * LANES]
            ss = jnp.sum(vh * vh, axis=-1, keepdims=True)
            out.append((h * LANES, LANES, vh * lax.rsqrt(ss + L2_EPS) * fac))
        return out

    for off, width, val in post(acc, tm):
        o_ref[:, off:off + width] = val
    for off, width, val in post(first, SUBLANES):
        o_ref[0:SUBLANES, off:off + width] = val


def _conv_proj(xb, w, cw, *, tm, tn, seq, l2_heads=False, n_scaled_tiles=0, scale=1.0):
    t, k = xb.shape
    n = w.shape[1]
    body = functools.partial(_conv_proj_body, tm=tm, tn=tn, tiles_per_seq=seq // tm,
                             l2_heads=l2_heads, n_scaled_tiles=n_scaled_tiles, scale=scale)
    return pl.pallas_call(
        body,
        grid=(n // tn, t // tm),
        in_specs=[
            pl.BlockSpec((tm, k), lambda j, i: (i, 0)),
            pl.BlockSpec((k, tn), lambda j, i: (0, j)),
            pl.BlockSpec((4, tn), lambda j, i: (0, j)),
        ],
        out_specs=pl.BlockSpec((tm, tn), lambda j, i: (i, j)),
        out_shape=jax.ShapeDtypeStruct((t, n), F32),
        scratch_shapes=[pltpu.VMEM((SUBLANES, tn), F32)],
        compiler_params=_params("arbitrary", "arbitrary"),
        name="conv_proj",
    )(xb, w, cw)


def _gate_proj_body(x_ref, w_ref, alog_ref, dtb_ref, bg_ref, gc_ref, *, tm):
    raw = _dot(x_ref[...], w_ref[...])
    lane = lax.broadcasted_iota(jnp.int32, raw.shape, 1)
    beta = jax.nn.sigmoid(raw)
    g = -jnp.exp(alog_ref[...]) * _softplus(raw + dtb_ref[...])
    g = jnp.where((lane >= GDN_V_HEADS) & (lane < 2 * GDN_V_HEADS), g, 0.0)
    bg_ref[...] = jnp.where(lane < GDN_V_HEADS, beta, g)
    gc_ref[...] = _chunk_cumsum(g, tm)


def _gate_proj(xb, w, alog, dtb, *, tm):
    t, k = xb.shape
    return pl.pallas_call(
        functools.partial(_gate_proj_body, tm=tm),
        grid=(t // tm,),
        in_specs=[
            pl.BlockSpec((tm, k), lambda i: (i, 0)),
            pl.BlockSpec((k, LANES), lambda i: (0, 0)),
            pl.BlockSpec((1, LANES), lambda i: (0, 0)),
            pl.BlockSpec((1, LANES), lambda i: (0, 0)),
        ],
        out_specs=[pl.BlockSpec((tm, LANES), lambda i: (i, 0))] * 2,
        out_shape=[jax.ShapeDtypeStruct((t, LANES), F32)] * 2,
        compiler_params=_params("arbitrary"),
        name="gate_proj",
    )(xb, w, alog, dtb)


def _gla_body(q_ref, k_ref, v_ref, go_ref, a_ref, wlr_ref, blr_ref, ng_ref, o_ref, st_ref, *, rows):
    @pl.when(pl.program_id(2) == 0)
    def _():
        st_ref[...] = jnp.zeros_like(st_ref)

    causal = _tril_mask(CHUNK)
    wlr = wlr_ref[...]
    blr = blr_ref[...]
    ng = ng_ref[...]
    for c in range(rows // CHUNK):
        sl = slice(c * CHUNK, (c + 1) * CHUNK)
        z = _dot(a_ref[sl, :].astype(BF16), wlr) + blr
        log_a = -_softplus(-z) / GLA_TAU
        b = _chunk_cumsum(log_a, CHUNK)
        b_last = b[CHUNK - 1:CHUNK, :]
        q = q_ref[sl, :] * (GLA_DK ** -0.5)
        k = k_ref[sl, :]
        v = v_ref[sl, :].astype(BF16)
        q_dec = (q * jnp.exp(b)).astype(BF16)
        k_inv = (k * jnp.exp(-b)).astype(BF16)
        k_end = (k * jnp.exp(b_last - b)).astype(BF16)
        scores = jnp.where(causal, _dot_nt(q_dec, k_inv), 0.0)
        st = st_ref[...]
        o = _dot(scores.astype(BF16), v) + _dot_nt(q_dec, st.astype(BF16))
        st_ref[...] = st * jnp.exp(b_last) + _dot_tn(v, k_end)
        on = o * lax.rsqrt(jnp.mean(o * o, axis=-1, keepdims=True) + RMS_EPS) * ng
        o_ref[sl, :] = (on * _silu(go_ref[sl, :])).astype(BF16)


def _gla(proj, gate, wlr, blr, ng, *, batch, seq, rows):
    t = batch * seq
    nblk = seq // rows
    row = lambda b, h, n: b * nblk + n
    nk = GLA_KW // GLA_DK
    return pl.pallas_call(
        functools.partial(_gla_body, rows=rows),
        grid=(batch, GLA_HEADS, nblk),
        in_specs=[
            pl.BlockSpec((rows, GLA_DK), lambda b, h, n: (row(b, h, n), h)),
            pl.BlockSpec((rows, GLA_DK), lambda b, h, n: (row(b, h, n), nk + h)),
            pl.BlockSpec((rows, GLA_DV), lambda b, h, n: (row(b, h, n), AB_V // GLA_DV + h)),
            pl.BlockSpec((rows, GLA_DV), lambda b, h, n: (row(b, h, n), AB_G // GLA_DV + h)),
            pl.BlockSpec((rows, LANES), lambda b, h, n: (row(b, h, n), 0)),
            pl.BlockSpec((LANES, GLA_DK), lambda b, h, n: (0, h)),
            pl.BlockSpec((1, GLA_DK), lambda b, h, n: (0, h)),
            pl.BlockSpec((1, GLA_DV), lambda b, h, n: (0, 0)),
        ],
        out_specs=pl.BlockSpec((rows, GLA_DV), lambda b, h, n: (row(b, h, n), h)),
        out_shape=jax.ShapeDtypeStruct((t, GLA_VW), BF16),
        scratch_shapes=[pltpu.VMEM((GLA_DV, GLA_DK), F32)],
        compiler_params=_params("arbitrary", "arbitrary", "arbitrary"),
        name="gla",
    )(proj, proj, proj, proj, gate, wlr, blr, ng)


def _s5_body(u_ref, bbd_ref, cbd_ref, pw_ref, d_ref, y_ref, xs_ref, cr_ref, ci_ref, *, rows):
    @pl.when(pl.program_id(2) == 0)
    def _():
        cr_ref[...] = jnp.zeros_like(cr_ref)
        ci_ref[...] = jnp.zeros_like(ci_ref)

    ns = S5_TILE_STATES
    u = u_ref[...]
    bu = _dot(u.astype(BF16), bbd_ref[0])
    levels = [(pw_ref[0, 2 * l], pw_ref[0, 2 * l + 1], 1 << l) for l in range(3)]
    qr, qi = pw_ref[0, 6], pw_ref[0, 7]
    cr = cr_ref[...]
    ci = ci_ref[...]
    for blk in range(rows // SUBLANES):
        sl = slice(blk * SUBLANES, (blk + 1) * SUBLANES)
        xr = bu[sl, :ns]
        xi = bu[sl, ns:]
        for pr, pi, s in levels:
            sr = pltpu.roll(xr, s, 0)
            si = pltpu.roll(xi, s, 0)
            xr, xi = xr + (pr * sr - pi * si), xi + (pr * si + pi * sr)
        xr, xi = xr + (qr * cr - qi * ci), xi + (qr * ci + qi * cr)
        cr = jnp.broadcast_to(xr[SUBLANES - 1:SUBLANES], (SUBLANES, ns))
        ci = jnp.broadcast_to(xi[SUBLANES - 1:SUBLANES], (SUBLANES, ns))
        xs_ref[sl, :ns] = xr
        xs_ref[sl, ns:] = xi
    cr_ref[...] = cr
    ci_ref[...] = ci
    y = _dot(xs_ref[...].astype(BF16), cbd_ref[0]) + d_ref[...] * u
    y3 = y * y * y
    y_ref[...] = y * (0.5 * (1.0 + jnp.tanh(np.float32(math.sqrt(2.0 / math.pi)) * (y + 0.044715 * y3))))


def _s5(proj, bbd, cbd, pw, d, *, batch, seq, rows, u_col0):
    t = batch * seq
    nblk = seq // rows
    ns = S5_TILE_STATES
    return pl.pallas_call(
        functools.partial(_s5_body, rows=rows),
        grid=(batch, S5_TILES, nblk),
        in_specs=[
            pl.BlockSpec((rows, LANES), lambda b, j, n: (b * nblk + n, u_col0 + j)),
            pl.BlockSpec((1, LANES, 2 * ns), lambda b, j, n: (j, 0, 0)),
            pl.BlockSpec((1, 2 * ns, LANES), lambda b, j, n: (j, 0, 0)),
            pl.BlockSpec((1, 8, SUBLANES, ns), lambda b, j, n: (j, 0, 0, 0)),
            pl.BlockSpec((1, LANES), lambda b, j, n: (0, j)),
        ],
        out_specs=pl.BlockSpec((rows, LANES), lambda b, j, n: (b * nblk + n, j)),
        out_shape=jax.ShapeDtypeStruct((t, S5_WIDTH), F32),
        scratch_shapes=[pltpu.VMEM((rows, 2 * ns), F32),
                        pltpu.VMEM((SUBLANES, ns), F32),
                        pltpu.VMEM((SUBLANES, ns), F32)],
        compiler_params=_params("arbitrary", "arbitrary", "arbitrary"),
        name="s5",
    )(proj, bbd, cbd, pw, d)


def _s5_tables(lam_re, lam_im, b_re, b_im, c_re, c_im, log_step):
    step = jnp.exp(log_step)[:, None]
    mag = jnp.exp(lam_re * step)
    a_re = mag * jnp.cos(lam_im * step)
    a_im = mag * jnp.sin(lam_im * step)
    inv_den = 1.0 / (lam_re * lam_re + lam_im * lam_im)
    f_re = ((a_re - 1.0) * lam_re + a_im * lam_im) * inv_den
    f_im = (a_im * lam_re - (a_re - 1.0) * lam_im) * inv_den
    bb_re = f_re[..., None] * b_re - f_im[..., None] * b_im
    bb_im = f_re[..., None] * b_im + f_im[..., None] * b_re
    tg, nt = S5_TILE_GROUPS, S5_TILES
    eye = jnp.eye(tg, dtype=F32)

    def in_table(bb):
        bb = bb.reshape(nt, tg, S5_STATE, S5_GROUP)
        return jnp.einsum('jgph,gk->jghkp', bb, eye).reshape(nt, LANES, S5_TILE_STATES)

    def out_table(cc):
        cc = cc.reshape(nt, tg, S5_GROUP, S5_STATE)
        return jnp.einsum('jghp,gk->jgpkh', cc, eye).reshape(nt, S5_TILE_STATES, LANES)

    bbd = jnp.concatenate([in_table(bb_re), in_table(bb_im)], axis=2).astype(BF16)
    cbd = jnp.concatenate([out_table(c_re), -out_table(c_im)], axis=1).astype(BF16)

    def cmul(x, y):
        return x[0] * y[0] - x[1] * y[1], x[0] * y[1] + x[1] * y[0]

    a1 = (a_re.reshape(nt, 1, S5_TILE_STATES), a_im.reshape(nt, 1, S5_TILE_STATES))
    powers = [a1]
    for _ in range(SUBLANES - 1):
        powers.append(cmul(powers[-1], a1))
    row = jnp.arange(SUBLANES)[None, :, None]
    tabs = []
    for s in (1, 2, 4):
        pr, pi = powers[s - 1]
        tabs.append(jnp.where(row >= s, pr, 0.0))
        tabs.append(jnp.where(row >= s, pi, 0.0))
    tabs.append(jnp.concatenate([p[0] for p in powers], axis=1))
    tabs.append(jnp.concatenate([p[1] for p in powers], axis=1))
    pw = jnp.stack([jnp.broadcast_to(tb, (nt, SUBLANES, S5_TILE_STATES)) for tb in tabs], axis=1)
    return bbd, cbd, pw.astype(F32)


def _mix0_out_body(og_ref, ys_ref, gw_ref, gb_ref, w1_ref, w2_ref, x_ref, g_ref, b_ref, o_ref):
    ys = ys_ref[...]
    gate = jax.nn.sigmoid(_dot(ys.astype(BF16), gw_ref[...]) + gb_ref[...])
    os5 = (ys * gate).astype(BF16)
    mix = _dot(og_ref[...], w1_ref[...]) + _dot(os5, w2_ref[...])
    o_ref[...] = _layer_norm(ALPHA * x_ref[...] + mix, g_ref[...], b_ref[...])


def _mix0_out(og, ys, gw, gb, w1, w2, x, g, b, *, tm):
    t, d = x.shape
    const = lambda shape: pl.BlockSpec(shape, lambda i: (0, 0))
    return pl.pallas_call(
        _mix0_out_body,
        grid=(t // tm,),
        in_specs=[
            pl.BlockSpec((tm, GLA_VW), lambda i: (i, 0)),
            pl.BlockSpec((tm, S5_WIDTH), lambda i: (i, 0)),
            const((S5_WIDTH, S5_WIDTH)), const((1, S5_WIDTH)),
            const((GLA_VW, d)), const((S5_WIDTH, d)),
            pl.BlockSpec((tm, d), lambda i: (i, 0)),
            const((1, d)), const((1, d)),
        ],
        out_specs=pl.BlockSpec((tm, d), lambda i: (i, 0)),
        out_shape=jax.ShapeDtypeStruct((t, d), F32),
        compiler_params=_params("arbitrary"),
        name="mix0_out",
    )(og, ys, gw, gb, w1, w2, x, g, b)


GDN_REP = GDN_V_HEADS // GDN_QK_HEADS


def _block_diag2(m, second):
    zero = jnp.zeros_like(m)
    return jnp.concatenate([jnp.where(second, zero, m), jnp.where(second, m, zero)], axis=0)


def _gdn_body(q_ref, k_ref, v_ref, z_ref, bg_ref, gc_ref, gr_ref, ng_ref, o_ref, st_ref, *, rows, heads):
    hg = pl.program_id(1)
    c2 = 2 * CHUNK
    vw = GDN_REP * GDN_DV

    @pl.when(pl.program_id(2) == 0)
    def _():
        st_ref[...] = jnp.zeros_like(st_ref)

    lane2 = lax.broadcasted_iota(jnp.int32, (CHUNK, c2), 1)
    row2 = lax.broadcasted_iota(jnp.int32, (CHUNK, c2), 0)
    second = lane2 >= CHUNK
    pos = jnp.where(second, lane2 - CHUNK, lane2)
    incl2 = pos <= row2
    strict2 = pos < row2
    eye2 = pos == row2
    lane_bg = lax.broadcasted_iota(jnp.int32, (CHUNK, LANES), 1)
    second_v = lax.broadcasted_iota(jnp.int32, (CHUNK, vw), 1) >= GDN_DV
    st_r = lax.broadcasted_iota(jnp.int32, (GDN_REP * GDN_DK, vw), 0) >= GDN_DK
    st_c = lax.broadcasted_iota(jnp.int32, (GDN_REP * GDN_DK, vw), 1) >= GDN_DV
    same_head = st_r == st_c
    ng = ng_ref[...]
    zero_rhs = jnp.zeros((CHUNK, GDN_DV + GDN_DK), BF16)

    def chunk(c, carry):
        r0 = pl.multiple_of(c * CHUNK, CHUNK)
        bg = bg_ref[pl.ds(r0, CHUNK), :]
        gcb = gc_ref[pl.ds(r0, CHUNK), :]

        def column(arr, idx):
            return jnp.sum(jnp.where(lane_bg == idx, arr, 0.0), axis=-1, keepdims=True)

        hs = range(heads)
        ks, qs, prods, cols, decays, xs, ps, pbds = [], [], [], [], [], [], [], []
        for g in hs:
            k = k_ref[pl.ds(r0, CHUNK), g * GDN_DK:(g + 1) * GDN_DK]
            q = q_ref[pl.ds(r0, CHUNK), g * GDN_DK:(g + 1) * GDN_DK]
            kb = k.astype(BF16)
            ks.append(k)
            qs.append(q)
            prods.append(_dot_nt(jnp.concatenate([kb, q.astype(BF16)], axis=0),
                                 jnp.concatenate([kb, kb], axis=0)))
        for g in hs:
            hq = hg * heads + g
            b0 = column(bg, GDN_REP * hq)
            b1 = column(bg, GDN_REP * hq + 1)
            g0 = column(gcb, GDN_V_HEADS + GDN_REP * hq)
            g1 = column(gcb, GDN_V_HEADS + GDN_REP * hq + 1)
            cols.append((b0, b1, g0, g1, g0[CHUNK - 1:CHUNK, :], g1[CHUNK - 1:CHUNK, :]))
            gc_row = gr_ref[g, pl.ds(c, 1), :]
            decay2 = jnp.exp(jnp.where(incl2, jnp.where(second, g1, g0) - gc_row, -jnp.inf))
            a2 = jnp.where(strict2, prods[g][:CHUNK] * jnp.where(second, b1, b0) * decay2, 0.0)
            decays.append(decay2)
            xs.append(jnp.where(eye2, 1.0, 0.0) - a2)
            ps.append(a2.astype(BF16))
            pbds.append(_block_diag2(ps[g], second))
        power = 1
        while 2 * power < CHUNK:
            for g in hs:
                ps[g] = _dot(ps[g], pbds[g]).astype(BF16)
                pbds[g] = _block_diag2(ps[g], second)
            for g in hs:
                xs[g] = xs[g] + _dot(xs[g].astype(BF16), pbds[g])
            power *= 2
        sols, exps = [], []
        for g in hs:
            b0, b1, g0, g1, gl0, gl1 = cols[g]
            e0 = jnp.exp(g0)
            e1 = jnp.exp(g1)
            exps.append((e0, e1))
            v2 = v_ref[pl.ds(r0, CHUNK), g * vw:(g + 1) * vw]
            rhs0 = jnp.concatenate([v2[:, :GDN_DV] * b0, (ks[g] * b0) * e0], axis=1).astype(BF16)
            rhs1 = jnp.concatenate([v2[:, GDN_DV:] * b1, (ks[g] * b1) * e1], axis=1).astype(BF16)
            rhs_bd = jnp.concatenate([jnp.concatenate([rhs0, zero_rhs], axis=1),
                                      jnp.concatenate([zero_rhs, rhs1], axis=1)], axis=0)
            sols.append(_dot(xs[g].astype(BF16), rhs_bd))
        sts, wss = [], []
        for g in hs:
            sol = sols[g]
            e0, e1 = exps[g]
            w2 = jnp.concatenate([sol[:, GDN_DV:vw], sol[:, vw + GDN_DV:]], axis=1)
            qd2 = jnp.concatenate([qs[g] * e0, qs[g] * e1], axis=1)
            sts.append(st_ref[g])
            wss.append(_dot(jnp.concatenate([w2, qd2], axis=0).astype(BF16), sts[g].astype(BF16)))
        new_states, results = [], []
        for g in hs:
            b0, b1, g0, g1, gl0, gl1 = cols[g]
            sol = sols[g]
            u2 = jnp.concatenate([sol[:, :GDN_DV], sol[:, vw:vw + GDN_DV]], axis=1)
            v_new = (u2 - wss[g][:CHUNK]).astype(BF16)
            attn2 = (prods[g][CHUNK:] * decays[g]).astype(BF16)
            o2 = wss[g][CHUNK:] + _dot(attn2, _block_diag2(v_new, second_v))
            k_end2 = jnp.concatenate([ks[g] * jnp.exp(gl0 - g0), ks[g] * jnp.exp(gl1 - g1)],
                                     axis=1).astype(BF16)
            upd = _dot_tn(k_end2, v_new)
            new_states.append(jnp.where(st_r, jnp.exp(gl1), jnp.exp(gl0)) * sts[g]
                              + jnp.where(same_head, upd, 0.0))
            outs = []
            for hh in range(GDN_REP):
                o = o2[:, hh * GDN_DV:(hh + 1) * GDN_DV]
                outs.append(o * lax.rsqrt(jnp.mean(o * o, axis=-1, keepdims=True) + RMS_EPS) * ng)
            z2 = z_ref[pl.ds(r0, CHUNK), g * vw:(g + 1) * vw]
            results.append((jnp.concatenate(outs, axis=1) * _silu(z2)).astype(BF16))
        for g in hs:
            st_ref[g] = new_states[g]
            o_ref[pl.ds(r0, CHUNK), g * vw:(g + 1) * vw] = results[g]
        return carry

    lax.fori_loop(0, rows // CHUNK, chunk, 0)


def _gdn(qk, v, z, bg, gc, gc_rows, ng, *, batch, seq, rows, heads):
    t = batch * seq
    nblk = seq // rows
    cpb = rows // CHUNK
    row = lambda b, h, n: b * nblk + n
    vw = heads * GDN_REP * GDN_DV
    kw = heads * GDN_DK
    return pl.pallas_call(
        functools.partial(_gdn_body, rows=rows, heads=heads),
        grid=(batch, GDN_QK_HEADS // heads, nblk),
        in_specs=[
            pl.BlockSpec((rows, kw), lambda b, h, n: (row(b, h, n), h)),
            pl.BlockSpec((rows, kw), lambda b, h, n: (row(b, h, n), GDN_QK_HEADS // heads + h)),
            pl.BlockSpec((rows, vw), lambda b, h, n: (row(b, h, n), h)),
            pl.BlockSpec((rows, vw), lambda b, h, n: (row(b, h, n), h)),
            pl.BlockSpec((rows, LANES), lambda b, h, n: (row(b, h, n), 0)),
            pl.BlockSpec((rows, LANES), lambda b, h, n: (row(b, h, n), 0)),
            pl.BlockSpec((heads, cpb, GDN_REP * CHUNK), lambda b, h, n: (h, row(b, h, n), 0)),
            pl.BlockSpec((1, GDN_DV), lambda b, h, n: (0, 0)),
        ],
        out_specs=pl.BlockSpec((rows, vw), lambda b, h, n: (row(b, h, n), h)),
        out_shape=jax.ShapeDtypeStruct((t, GDN_VW), BF16),
        scratch_shapes=[pltpu.VMEM((heads, GDN_REP * GDN_DK, GDN_REP * GDN_DV), F32)],
        compiler_params=_params("arbitrary", "arbitrary", "arbitrary"),
        name="gdn",
    )(qk, qk, v, z, bg, gc, gc_rows, ng)


def _out_ln_body(a_ref, w_ref, x_ref, g_ref, b_ref, o_ref, *, nk):
    kk = pl.program_id(1)

    @pl.when(kk == 0)
    def _():
        o_ref[...] = jnp.zeros_like(o_ref)

    o_ref[...] += _dot(a_ref[...], w_ref[...])

    @pl.when(kk == nk - 1)
    def _():
        o_ref[...] = _layer_norm(ALPHA * x_ref[...] + o_ref[...], g_ref[...], b_ref[...])


def _out_ln(a, w, x, g, b, *, tm, tk):
    t, d = x.shape
    nk = a.shape[1] // tk
    return pl.pallas_call(
        functools.partial(_out_ln_body, nk=nk),
        grid=(t // tm, nk),
        in_specs=[
            pl.BlockSpec((tm, tk), lambda i, k: (i, k)),
            pl.BlockSpec((tk, d), lambda i, k: (k, 0)),
            pl.BlockSpec((tm, d), lambda i, k: (i, 0)),
            pl.BlockSpec((1, d), lambda i, k: (0, 0)),
            pl.BlockSpec((1, d), lambda i, k: (0, 0)),
        ],
        out_specs=pl.BlockSpec((tm, d), lambda i, k: (i, 0)),
        out_shape=jax.ShapeDtypeStruct((t, d), F32),
        compiler_params=_params("arbitrary", "arbitrary"),
        name="out_ln",
    )(a, w, x, g, b)


def _tiles(batch, seq):
    tm = min(512, seq)
    return dict(tm=tm, ffn_tm=min(1024, seq),gla_rows=min(256, seq), s5_rows=min(256, seq), gdn_rows=min(512, seq), gdn_heads=8)


def _ffn(x, wg, wu, wd, g, b, tm):
    pad = D_FF_PAD - D_FF
    wg = jnp.pad(wg, ((0, 0), (0, pad))).astype(BF16)
    wu = jnp.pad(wu, ((0, 0), (0, pad))).astype(BF16)
    wd = jnp.pad(wd, ((0, pad), (0, 0))).astype(BF16)
    return _ffn_ln(x, wg, wu, wd, g[None], b[None], tm=tm, tf=512)


def _pad_cols(w, n):
    return jnp.pad(w, ((0, 0), (0, n - w.shape[1])))


def _gla_s5_layer(x, xb, p, i, g, b, batch, seq, tl):
    w_in = p['ab_w_in'][i]
    w_main = jnp.concatenate([w_in[:, :AB_A], w_in[:, AB_U:]], axis=1).astype(BF16)
    w_gate = _pad_cols(w_in[:, AB_A:AB_U], LANES).astype(BF16)
    proj = _proj(xb, w_main, tm=tl['tm'], tn=1024)
    gate = _proj(xb, w_gate, tm=tl['tm'], tn=LANES)
    wlr = jnp.pad(p['gla_w_lr'][i], ((0, LANES - GLA_RANK), (0, 0))).astype(BF16)
    o_gla = _gla(proj, gate, wlr, p['gla_b_lr'][i][None], p['gla_norm_g'][i][None],
                 batch=batch, seq=seq, rows=tl['gla_rows'])
    bbd, cbd, pw = _s5_tables(p['s5_lam_re'][i], p['s5_lam_im'][i], p['s5_b_re'][i], p['s5_b_im'][i],
                              p['s5_c_re'][i], p['s5_c_im'][i], p['s5_log_step'][i])
    ys = _s5(proj, bbd, cbd, pw, p['s5_d'][i][None], batch=batch, seq=seq, rows=tl['s5_rows'],
             u_col0=AB_A // LANES)
    w_out = p['ab_w_out'][i].astype(BF16)
    return _mix0_out(o_gla, ys, p['s5_glu_w'][i].astype(BF16), p['s5_glu_b'][i][None],
                     w_out[:GLA_VW], w_out[GLA_VW:], x, g[None], b[None], tm=tl['tm'])


def _gdn_layer(x, xb, p, i, g, b, batch, seq, tl):
    t = batch * seq
    w_in = p['gdn_w_in'][i]
    conv_w = p['gdn_conv_w'][i]
    tm = tl['tm']
    qk = _conv_proj(xb, w_in[:, :2 * GDN_KW].astype(BF16), conv_w[:, :2 * GDN_KW], tm=tm, tn=1024,
                    seq=seq, l2_heads=True, n_scaled_tiles=GDN_KW // 1024, scale=GDN_DK ** -0.5)
    v = _conv_proj(xb, w_in[:, 2 * GDN_KW:GDN_QKV].astype(BF16), conv_w[:, 2 * GDN_KW:], tm=tm,
                   tn=1024, seq=seq)
    z = _proj(xb, w_in[:, GDN_Z:GDN_B].astype(BF16), tm=tm, tn=1024)
    w_gate = _pad_cols(w_in[:, GDN_B:GDN_IN], LANES).astype(BF16)
    lane_pad = lambda a: jnp.pad(a, (GDN_V_HEADS, LANES - 2 * GDN_V_HEADS))[None]
    bg, gc = _gate_proj(xb, w_gate, lane_pad(p['gdn_a_log'][i]), lane_pad(p['gdn_dt_bias'][i]), tm=tm)
    gc_rows = gc[:, GDN_V_HEADS:2 * GDN_V_HEADS].reshape(t // CHUNK, CHUNK, GDN_QK_HEADS, GDN_REP)
    gc_rows = gc_rows.transpose(2, 0, 3, 1).reshape(GDN_QK_HEADS, t // CHUNK, GDN_REP * CHUNK)
    o = _gdn(qk, v, z, bg, gc, gc_rows, p['gdn_norm_g'][i][None], batch=batch, seq=seq,
             rows=tl['gdn_rows'], heads=tl['gdn_heads'])
    return _out_ln(o, p['gdn_w_out'][i].astype(BF16), x, g[None], b[None], tm=tm, tk=1024)


def kernel(x, ffn_a_gate, ffn_a_up, ffn_a_down, ffn_b_gate, ffn_b_up, ffn_b_down, ln_g, ln_b, ab_w_in, gla_w_lr, gla_b_lr, gla_norm_g, s5_lam_re, s5_lam_im, s5_b_re, s5_b_im, s5_c_re, s5_c_im, s5_d, s5_log_step, s5_glu_w, s5_glu_b, ab_w_out, gdn_w_in, gdn_conv_w, gdn_a_log, gdn_dt_bias, gdn_norm_g, gdn_w_out):
    batch, seq, d = x.shape
    assert d == D_MODEL and seq % CHUNK == 0
    p = dict(ab_w_in=ab_w_in, gla_w_lr=gla_w_lr, gla_b_lr=gla_b_lr, gla_norm_g=gla_norm_g,
             s5_lam_re=s5_lam_re, s5_lam_im=s5_lam_im, s5_b_re=s5_b_re, s5_b_im=s5_b_im,
             s5_c_re=s5_c_re, s5_c_im=s5_c_im, s5_d=s5_d, s5_log_step=s5_log_step,
             s5_glu_w=s5_glu_w, s5_glu_b=s5_glu_b, ab_w_out=ab_w_out, gdn_w_in=gdn_w_in,
             gdn_conv_w=gdn_conv_w, gdn_a_log=gdn_a_log, gdn_dt_bias=gdn_dt_bias,
             gdn_norm_g=gdn_norm_g, gdn_w_out=gdn_w_out)
    tl = _tiles(batch, seq)
    h = x.reshape(batch * seq, d).astype(F32)
    for layer in range(DEPTH):
        h, hb = _ffn(h, ffn_a_gate[layer], ffn_a_up[layer], ffn_a_down[layer],
                     ln_g[layer, 0], ln_b[layer, 0], tl['ffn_tm'])
        i = layer // 2
        mixer = _gla_s5_layer if layer % 2 == 0 else _gdn_layer
        h = mixer(h, hb, p, i, ln_g[layer, 1], ln_b[layer, 1], batch, seq, tl)
        h, _ = _ffn(h, ffn_b_gate[layer], ffn_b_up[layer], ffn_b_down[layer],
                    ln_g[layer, 2], ln_b[layer, 2], tl['ffn_tm'])
    return h.reshape(batch, seq, d)
```

```python
import functools
import math

import jax
import jax.numpy as jnp
import numpy as np
from jax import lax
from jax.experimental import pallas as pl
from jax.experimental.pallas import tpu as pltpu

F32 = jnp.float32
BF16 = jnp.bfloat16

LANES = 128
SUBLANES = 8
VMEM_LIMIT_BYTES = 60 * 1024 * 1024
LN_ROWS = 64
CONV_SUB = 256

D_MODEL = 2048
DEPTH = 2
ALPHA = (2.0 * DEPTH) ** 0.25
MACARON = 0.5
LN_EPS = 1e-5
RMS_EPS = 1e-6
L2_EPS = 1e-6
D_FF = 5504

GLA_HEADS = 4
GLA_DK = 128
GLA_DV = 256
GLA_RANK = 16
GLA_TAU = 16.0
CHUNK = 64
GLA_KW = GLA_HEADS * GLA_DK
GLA_VW = GLA_HEADS * GLA_DV

S5_WIDTH = 1024
S5_GROUP = 16
S5_GROUPS = 64
S5_STATE = 64
S5_TILE_GROUPS = LANES // S5_GROUP
S5_TILES = S5_WIDTH // LANES
S5_TILE_STATES = S5_TILE_GROUPS * S5_STATE

AB_K = GLA_KW
AB_V = AB_K + GLA_KW
AB_G = AB_V + GLA_VW
AB_A = AB_G + GLA_VW
AB_U = AB_A + GLA_RANK
AB_IN = AB_U + S5_WIDTH

GDN_QK_HEADS = 16
GDN_V_HEADS = 32
GDN_DK = 128
GDN_DV = 128
GDN_KW = GDN_QK_HEADS * GDN_DK
GDN_VW = GDN_V_HEADS * GDN_DV
GDN_QKV = 2 * GDN_KW + GDN_VW
GDN_Z = GDN_QKV
GDN_B = GDN_Z + GDN_VW
GDN_A = GDN_B + GDN_V_HEADS
GDN_IN = GDN_A + GDN_V_HEADS


def _params(*sem):
    return pltpu.CompilerParams(dimension_semantics=sem, vmem_limit_bytes=VMEM_LIMIT_BYTES)


def _dot(a, b):
    return jnp.dot(a, b, preferred_element_type=F32)


def _dot_nt(a, b):
    return lax.dot_general(a, b, (((1,), (1,)), ((), ())), preferred_element_type=F32)


def _dot_tn(a, b):
    return lax.dot_general(a, b, (((0,), (0,)), ((), ())), preferred_element_type=F32)


def _layer_norm(y, g, b):
    mu = jnp.mean(y, axis=-1, keepdims=True)
    yc = y - mu
    var = jnp.mean(yc * yc, axis=-1, keepdims=True)
    return yc * lax.rsqrt(var + LN_EPS) * g + b


def _silu(v):
    return v * jax.nn.sigmoid(v)


def _softplus(v):
    return jnp.maximum(v, 0.0) + jnp.log1p(jnp.exp(-jnp.abs(v)))


def _split3(v):
    hi = v.astype(BF16)
    r1 = v - hi.astype(F32)
    mid = r1.astype(BF16)
    lo = (r1 - mid.astype(F32)).astype(BF16)
    return hi, mid, lo


def _tril_mask(n, strict=False):
    r = lax.broadcasted_iota(jnp.int32, (n, n), 0)
    c = lax.broadcasted_iota(jnp.int32, (n, n), 1)
    return (c < r) if strict else (c <= r)


def _chunk_cumsum(v, rows):
    r = lax.broadcasted_iota(jnp.int32, (rows, rows), 0)
    c = lax.broadcasted_iota(jnp.int32, (rows, rows), 1)
    tri = ((c <= r) & ((c // CHUNK) == (r // CHUNK))).astype(BF16)
    hi, mid, lo = _split3(v)
    return _dot(tri, hi) + _dot(tri, mid) + _dot(tri, lo)


def _ffn_body(x_ref, wg_ref, wu_ref, wd_ref, wgt_ref, wut_ref, wdt_ref, g_ref, b_ref, o_ref, ob_ref,
              xb_ref, *, nj):
    j = pl.program_id(1)

    @pl.when(j == 0)
    def _():
        xb_ref[...] = x_ref[...].astype(BF16)
        o_ref[...] = jnp.zeros_like(o_ref)

    def down(wg, wu, wd):
        xb = xb_ref[...]
        h = (_silu(_dot(xb, wg)) * _dot(xb, wu)).astype(BF16)
        return _dot(h, wd)

    @pl.when(j < nj - 1)
    def _():
        o_ref[...] += down(wg_ref[...], wu_ref[...], wd_ref[...])

    @pl.when(j == nj - 1)
    def _():
        o_ref[...] += down(wgt_ref[...], wut_ref[...], wdt_ref[...])
        g = g_ref[...]
        b = b_ref[...]

        def rows(r, carry):
            sl = pl.ds(pl.multiple_of(r * LN_ROWS, LN_ROWS), LN_ROWS)
            yn = _layer_norm(ALPHA * x_ref[sl, :] + MACARON * o_ref[sl, :], g, b)
            o_ref[sl, :] = yn
            ob_ref[sl, :] = yn.astype(BF16)
            return carry

        lax.fori_loop(0, o_ref.shape[0] // LN_ROWS, rows, 0)


def _ffn_ln(x, wg, wu, wd, g, b, *, tm, tf):
    t, d = x.shape
    f = wg.shape[1]
    n_full = f // tf
    f_tail = f - n_full * tf
    assert f_tail > 0 and f_tail % LANES == 0
    nj = n_full + 1
    last = n_full - 1
    const = lambda shape: pl.BlockSpec(shape, lambda i, j: (0, 0))
    return pl.pallas_call(
        functools.partial(_ffn_body, nj=nj),
        grid=(t // tm, nj),
        in_specs=[
            pl.BlockSpec((tm, d), lambda i, j: (i, 0)),
            pl.BlockSpec((d, tf), lambda i, j: (0, jnp.minimum(j, last))),
            pl.BlockSpec((d, tf), lambda i, j: (0, jnp.minimum(j, last))),
            pl.BlockSpec((tf, d), lambda i, j: (jnp.minimum(j, last), 0)),
            const((d, f_tail)), const((d, f_tail)), const((f_tail, d)),
            const((1, d)), const((1, d)),
        ],
        out_specs=[
            pl.BlockSpec((tm, d), lambda i, j: (i, 0)),
            pl.BlockSpec((tm, d), lambda i, j: (i, 0)),
        ],
        out_shape=[jax.ShapeDtypeStruct((t, d), F32), jax.ShapeDtypeStruct((t, d), BF16)],
        scratch_shapes=[pltpu.VMEM((tm, d), BF16)],
        compiler_params=_params("arbitrary", "arbitrary"),
        name="ffn_ln",
    )(x, wg, wu, wd, wg[:, f - f_tail:], wu[:, f - f_tail:], wd[f - f_tail:], g, b)


def _proj_body(x_ref, w_ref, o_ref):
    o_ref[...] = _dot(x_ref[...], w_ref[...])


def _proj(xb, w, *, tm, tn):
    t, k = xb.shape
    n = w.shape[1]
    return pl.pallas_call(
        _proj_body,
        grid=(n // tn, t // tm),
        in_specs=[
            pl.BlockSpec((tm, k), lambda j, i: (i, 0)),
            pl.BlockSpec((k, tn), lambda j, i: (0, j)),
        ],
        out_specs=pl.BlockSpec((tm, tn), lambda j, i: (i, j)),
        out_shape=jax.ShapeDtypeStruct((t, n), F32),
        compiler_params=_params("arbitrary", "arbitrary"),
        name="proj",
    )(xb, w)


def _conv_proj_body(x_ref, w_ref, cw_ref, o_ref, carry_ref, *, tm, tn, tiles_per_seq,
                    l2_heads, n_scaled_tiles, scale):
    jt = pl.program_id(0)
    i = pl.program_id(1)

    @pl.when(i % tiles_per_seq == 0)
    def _():
        carry_ref[0:SUBLANES, :] = jnp.zeros((SUBLANES, tn), F32)

    if l2_heads:
        fac = jnp.where(jt < n_scaled_tiles, jnp.float32(scale), jnp.float32(1.0))

    def post(v):
        v = _silu(v)
        if not l2_heads:
            return v
        out = []
        for h in range(CONV_SUB // LANES):
            vh = v[:, h * LANES:(h + 1) * LANES]
            ss = jnp.sum(vh * vh, axis=-1, keepdims=True)
            out.append(vh * lax.rsqrt(ss + L2_EPS) * fac)
        return jnp.concatenate(out, axis=1)

    def epilogue(s, raw):
        cols = slice(s * CONV_SUB, (s + 1) * CONV_SUB)
        cw = cw_ref[:, cols]
        carry_ref[SUBLANES:SUBLANES + tm, cols] = raw
        acc = raw * cw[3:4]
        for d in (1, 2, 3):
            acc = acc + carry_ref[SUBLANES - d:SUBLANES - d + tm, cols] * cw[3 - d:4 - d]
        carry_ref[0:SUBLANES, cols] = raw[tm - SUBLANES:tm]
        o_ref[:, cols] = post(acc)

    x = x_ref[...]
    raws = [_dot(x, w_ref[:, s * CONV_SUB:(s + 1) * CONV_SUB]) for s in range(tn // CONV_SUB)]
    for s, raw in enumerate(raws):
        epilogue(s, raw)


def _conv_proj(xb, w, cw, *, tm, tn, seq, l2_heads=False, n_scaled_tiles=0, scale=1.0):
    t, k = xb.shape
    n = w.shape[1]
    body = functools.partial(_conv_proj_body, tm=tm, tn=tn, tiles_per_seq=seq // tm,
                             l2_heads=l2_heads, n_scaled_tiles=n_scaled_tiles, scale=scale)
    return pl.pallas_call(
        body,
        grid=(n // tn, t // tm),
        in_specs=[
            pl.BlockSpec((tm, k), lambda j, i: (i, 0)),
            pl.BlockSpec((k, tn), lambda j, i: (0, j)),
            pl.BlockSpec((4, tn), lambda j, i: (0, j)),
        ],
        out_specs=pl.BlockSpec((tm, tn), lambda j, i: (i, j)),
        out_shape=jax.ShapeDtypeStruct((t, n), F32),
        scratch_shapes=[pltpu.VMEM((SUBLANES + tm, tn), F32)],
        compiler_params=_params("arbitrary", "arbitrary"),
        name="conv_proj",
    )(xb, w, cw)


def _gate_proj_body(x_ref, w_ref, alog_ref, dtb_ref, bg_ref, gc_ref, *, tm):
    raw = _dot(x_ref[...], w_ref[...])
    lane = lax.broadcasted_iota(jnp.int32, raw.shape, 1)
    beta = jax.nn.sigmoid(raw)
    g = -jnp.exp(alog_ref[...]) * _softplus(raw + dtb_ref[...])
    g = jnp.where((lane >= GDN_V_HEADS) & (lane < 2 * GDN_V_HEADS), g, 0.0)
    bg_ref[...] = jnp.where(lane < GDN_V_HEADS, beta, g)
    gc_ref[...] = _chunk_cumsum(g, tm)


def _gate_proj(xb, w, alog, dtb, *, tm):
    t, k = xb.shape
    return pl.pallas_call(
        functools.partial(_gate_proj_body, tm=tm),
        grid=(t // tm,),
        in_specs=[
            pl.BlockSpec((tm, k), lambda i: (i, 0)),
            pl.BlockSpec((k, LANES), lambda i: (0, 0)),
            pl.BlockSpec((1, LANES), lambda i: (0, 0)),
            pl.BlockSpec((1, LANES), lambda i: (0, 0)),
        ],
        out_specs=[pl.BlockSpec((tm, LANES), lambda i: (i, 0))] * 2,
        out_shape=[jax.ShapeDtypeStruct((t, LANES), F32)] * 2,
        compiler_params=_params("arbitrary"),
        name="gate_proj",
    )(xb, w, alog, dtb)


def _gla_body(q_ref, k_ref, v_ref, go_ref, a_ref, wlr_ref, blr_ref, ng_ref, o_ref, st_ref, *, rows):
    @pl.when(pl.program_id(1) == 0)
    def _():
        st_ref[...] = jnp.zeros_like(st_ref)

    causal = _tril_mask(CHUNK)
    ng = ng_ref[...]
    heads = range(GLA_HEADS)
    chunks = range(rows // CHUNK)
    head = lambda a, h: a[:, h * GLA_DK:(h + 1) * GLA_DK]

    z = _dot(a_ref[...].astype(BF16), wlr_ref[...]) + blr_ref[...]
    b = _chunk_cumsum(-_softplus(-z) / GLA_TAU, rows)
    k = k_ref[...]
    q_dec = (q_ref[...] * (GLA_DK ** -0.5) * jnp.exp(b)).astype(BF16)
    k_inv = (k * jnp.exp(-b)).astype(BF16)
    rows_of = lambda a, c: a[c * CHUNK:(c + 1) * CHUNK]
    b_last = [rows_of(b, c)[CHUNK - 1:CHUNK] for c in chunks]
    k_end = [(rows_of(k, c) * jnp.exp(b_last[c] - rows_of(b, c))).astype(BF16) for c in chunks]
    scores = [[jnp.where(causal, _dot_nt(head(rows_of(q_dec, c), h), head(rows_of(k_inv, c), h)),
                         0.0).astype(BF16) for h in heads] for c in chunks]
    for c in chunks:
        sl = slice(c * CHUNK, (c + 1) * CHUNK)
        sts = [st_ref[h] for h in heads]
        vs = [v_ref[sl, h * GLA_DV:(h + 1) * GLA_DV].astype(BF16) for h in heads]
        qd = rows_of(q_dec, c)
        outs = [_dot(scores[c][h], vs[h]) + _dot_nt(head(qd, h), sts[h].astype(BF16)) for h in heads]
        decay = jnp.exp(b_last[c])
        new = [sts[h] * head(decay, h) + _dot_tn(vs[h], head(k_end[c], h)) for h in heads]
        for h in heads:
            st_ref[h] = new[h]
        normed = [o * lax.rsqrt(jnp.mean(o * o, axis=-1, keepdims=True) + RMS_EPS) * ng for o in outs]
        o_ref[sl, :] = (jnp.concatenate(normed, axis=1) * _silu(go_ref[sl, :])).astype(BF16)


def _gla(proj, gate, wlr, blr, ng, *, batch, seq, rows):
    t = batch * seq
    nblk = seq // rows
    row = lambda b, n: b * nblk + n
    return pl.pallas_call(
        functools.partial(_gla_body, rows=rows),
        grid=(batch, nblk),
        in_specs=[
            pl.BlockSpec((rows, GLA_KW), lambda b, n: (row(b, n), 0)),
            pl.BlockSpec((rows, GLA_KW), lambda b, n: (row(b, n), 1)),
            pl.BlockSpec((rows, GLA_VW), lambda b, n: (row(b, n), AB_V // GLA_VW)),
            pl.BlockSpec((rows, GLA_VW), lambda b, n: (row(b, n), AB_G // GLA_VW)),
            pl.BlockSpec((rows, LANES), lambda b, n: (row(b, n), 0)),
            pl.BlockSpec((LANES, GLA_KW), lambda b, n: (0, 0)),
            pl.BlockSpec((1, GLA_KW), lambda b, n: (0, 0)),
            pl.BlockSpec((1, GLA_DV), lambda b, n: (0, 0)),
        ],
        out_specs=pl.BlockSpec((rows, GLA_VW), lambda b, n: (row(b, n), 0)),
        out_shape=jax.ShapeDtypeStruct((t, GLA_VW), BF16),
        scratch_shapes=[pltpu.VMEM((GLA_HEADS, GLA_DV, GLA_DK), F32)],
        compiler_params=_params("arbitrary", "arbitrary"),
        name="gla",
    )(proj, proj, proj, proj, gate, wlr, blr, ng)


def _s5_body(u_ref, bbd_ref, cbd_ref, pw_ref, d_ref, y_ref, xs_ref, cr_ref, ci_ref, *, rows):
    @pl.when(pl.program_id(2) == 0)
    def _():
        cr_ref[...] = jnp.zeros_like(cr_ref)
        ci_ref[...] = jnp.zeros_like(ci_ref)

    ns = S5_TILE_STATES
    u = u_ref[...]
    bu = _dot(u.astype(BF16), bbd_ref[0])
    levels = [(pw_ref[0, 2 * l], pw_ref[0, 2 * l + 1], 1 << l) for l in range(3)]
    qr, qi = pw_ref[0, 6], pw_ref[0, 7]
    cr = cr_ref[...]
    ci = ci_ref[...]
    for blk in range(rows // SUBLANES):
        sl = slice(blk * SUBLANES, (blk + 1) * SUBLANES)
        xr = bu[sl, :ns]
        xi = bu[sl, ns:]
        for pr, pi, s in levels:
            sr = pltpu.roll(xr, s, 0)
            si = pltpu.roll(xi, s, 0)
            xr, xi = xr + (pr * sr - pi * si), xi + (pr * si + pi * sr)
        xr, xi = xr + (qr * cr - qi * ci), xi + (qr * ci + qi * cr)
        cr = jnp.broadcast_to(xr[SUBLANES - 1:SUBLANES], (SUBLANES, ns))
        ci = jnp.broadcast_to(xi[SUBLANES - 1:SUBLANES], (SUBLANES, ns))
        xs_ref[sl, :ns] = xr
        xs_ref[sl, ns:] = xi
    cr_ref[...] = cr
    ci_ref[...] = ci
    y = _dot(xs_ref[...].astype(BF16), cbd_ref[0]) + d_ref[...] * u
    y3 = y * y * y
    y_ref[...] = y * (0.5 * (1.0 + jnp.tanh(np.float32(math.sqrt(2.0 / math.pi)) * (y + 0.044715 * y3))))


def _s5(proj, bbd, cbd, pw, d, *, batch, seq, rows, u_col0):
    t = batch * seq
    nblk = seq // rows
    ns = S5_TILE_STATES
    return pl.pallas_call(
        functools.partial(_s5_body, rows=rows),
        grid=(batch, S5_TILES, nblk),
        in_specs=[
            pl.BlockSpec((rows, LANES), lambda b, j, n: (b * nblk + n, u_col0 + j)),
            pl.BlockSpec((1, LANES, 2 * ns), lambda b, j, n: (j, 0, 0)),
            pl.BlockSpec((1, 2 * ns, LANES), lambda b, j, n: (j, 0, 0)),
            pl.BlockSpec((1, 8, SUBLANES, ns), lambda b, j, n: (j, 0, 0, 0)),
            pl.BlockSpec((1, LANES), lambda b, j, n: (0, j)),
        ],
        out_specs=pl.BlockSpec((rows, LANES), lambda b, j, n: (b * nblk + n, j)),
        out_shape=jax.ShapeDtypeStruct((t, S5_WIDTH), F32),
        scratch_shapes=[pltpu.VMEM((rows, 2 * ns), F32),
                        pltpu.VMEM((SUBLANES, ns), F32),
                        pltpu.VMEM((SUBLANES, ns), F32)],
        compiler_params=_params("arbitrary", "arbitrary", "arbitrary"),
        name="s5",
    )(proj, bbd, cbd, pw, d)


def _s5_tables(lam_re, lam_im, b_re, b_im, c_re, c_im, log_step):
    step = jnp.exp(log_step)[:, None]
    mag = jnp.exp(lam_re * step)
    a_re = mag * jnp.cos(lam_im * step)
    a_im = mag * jnp.sin(lam_im * step)
    inv_den = 1.0 / (lam_re * lam_re + lam_im * lam_im)
    f_re = ((a_re - 1.0) * lam_re + a_im * lam_im) * inv_den
    f_im = (a_im * lam_re - (a_re - 1.0) * lam_im) * inv_den
    bb_re = f_re[..., None] * b_re - f_im[..., None] * b_im
    bb_im = f_re[..., None] * b_im + f_im[..., None] * b_re
    tg, nt = S5_TILE_GROUPS, S5_TILES
    eye = jnp.eye(tg, dtype=F32)

    def in_table(bb):
        bb = bb.reshape(nt, tg, S5_STATE, S5_GROUP)
        return jnp.einsum('jgph,gk->jghkp', bb, eye).reshape(nt, LANES, S5_TILE_STATES)

    def out_table(cc):
        cc = cc.reshape(nt, tg, S5_GROUP, S5_STATE)
        return jnp.einsum('jghp,gk->jgpkh', cc, eye).reshape(nt, S5_TILE_STATES, LANES)

    bbd = jnp.concatenate([in_table(bb_re), in_table(bb_im)], axis=2).astype(BF16)
    cbd = jnp.concatenate([out_table(c_re), -out_table(c_im)], axis=1).astype(BF16)

    def cmul(x, y):
        return x[0] * y[0] - x[1] * y[1], x[0] * y[1] + x[1] * y[0]

    a1 = (a_re.reshape(nt, 1, S5_TILE_STATES), a_im.reshape(nt, 1, S5_TILE_STATES))
    powers = [a1]
    for _ in range(SUBLANES - 1):
        powers.append(cmul(powers[-1], a1))
    row = jnp.arange(SUBLANES)[None, :, None]
    tabs = []
    for s in (1, 2, 4):
        pr, pi = powers[s - 1]
        tabs.append(jnp.where(row >= s, pr, 0.0))
        tabs.append(jnp.where(row >= s, pi, 0.0))
    tabs.append(jnp.concatenate([p[0] for p in powers], axis=1))
    tabs.append(jnp.concatenate([p[1] for p in powers], axis=1))
    pw = jnp.stack([jnp.broadcast_to(tb, (nt, SUBLANES, S5_TILE_STATES)) for tb in tabs], axis=1)
    return bbd, cbd, pw.astype(F32)


def _mix0_out_body(og_ref, ys_ref, gw_ref, gb_ref, w1_ref, w2_ref, x_ref, g_ref, b_ref, o_ref):
    ys = ys_ref[...]
    gate = jax.nn.sigmoid(_dot(ys.astype(BF16), gw_ref[...]) + gb_ref[...])
    os5 = (ys * gate).astype(BF16)
    mix = _dot(og_ref[...], w1_ref[...]) + _dot(os5, w2_ref[...])
    o_ref[...] = _layer_norm(ALPHA * x_ref[...] + mix, g_ref[...], b_ref[...])


def _mix0_out(og, ys, gw, gb, w1, w2, x, g, b, *, tm):
    t, d = x.shape
    const = lambda shape: pl.BlockSpec(shape, lambda i: (0, 0))
    return pl.pallas_call(
        _mix0_out_body,
        grid=(t // tm,),
        in_specs=[
            pl.BlockSpec((tm, GLA_VW), lambda i: (i, 0)),
            pl.BlockSpec((tm, S5_WIDTH), lambda i: (i, 0)),
            const((S5_WIDTH, S5_WIDTH)), const((1, S5_WIDTH)),
            const((GLA_VW, d)), const((S5_WIDTH, d)),
            pl.BlockSpec((tm, d), lambda i: (i, 0)),
            const((1, d)), const((1, d)),
        ],
        out_specs=pl.BlockSpec((tm, d), lambda i: (i, 0)),
        out_shape=jax.ShapeDtypeStruct((t, d), F32),
        compiler_params=_params("arbitrary"),
        name="mix0_out",
    )(og, ys, gw, gb, w1, w2, x, g, b)


GDN_REP = GDN_V_HEADS // GDN_QK_HEADS


def _block_diag2(m, second):
    zero = jnp.zeros_like(m)
    return jnp.concatenate([jnp.where(second, zero, m), jnp.where(second, m, zero)], axis=0)


def _gdn_body(q_ref, k_ref, v_ref, z_ref, bg_ref, gc_ref, gr_ref, ng_ref, o_ref, st_ref, *, rows, heads):
    hg = pl.program_id(1)
    c2 = 2 * CHUNK
    vw = GDN_REP * GDN_DV

    @pl.when(pl.program_id(2) == 0)
    def _():
        st_ref[...] = jnp.zeros_like(st_ref)

    lane2 = lax.broadcasted_iota(jnp.int32, (CHUNK, c2), 1)
    row2 = lax.broadcasted_iota(jnp.int32, (CHUNK, c2), 0)
    second = lane2 >= CHUNK
    pos = jnp.where(second, lane2 - CHUNK, lane2)
    incl2 = pos <= row2
    strict2 = pos < row2
    eye2 = pos == row2
    lane_bg = lax.broadcasted_iota(jnp.int32, (CHUNK, LANES), 1)
    second_v = lax.broadcasted_iota(jnp.int32, (CHUNK, vw), 1) >= GDN_DV
    st_r = lax.broadcasted_iota(jnp.int32, (GDN_REP * GDN_DK, vw), 0) >= GDN_DK
    st_c = lax.broadcasted_iota(jnp.int32, (GDN_REP * GDN_DK, vw), 1) >= GDN_DV
    same_head = st_r == st_c
    ng = ng_ref[...]
    zero_rhs = jnp.zeros((CHUNK, GDN_DV + GDN_DK), BF16)

    def chunk(c, carry):
        r0 = pl.multiple_of(c * CHUNK, CHUNK)
        bg = bg_ref[pl.ds(r0, CHUNK), :]
        gcb = gc_ref[pl.ds(r0, CHUNK), :]

        def column(arr, idx):
            return jnp.sum(jnp.where(lane_bg == idx, arr, 0.0), axis=-1, keepdims=True)

        hs = range(heads)
        ks, qs, prods, cols, decays, xs, ps, pbds = [], [], [], [], [], [], [], []
        for g in hs:
            k = k_ref[pl.ds(r0, CHUNK), g * GDN_DK:(g + 1) * GDN_DK]
            q = q_ref[pl.ds(r0, CHUNK), g * GDN_DK:(g + 1) * GDN_DK]
            kb = k.astype(BF16)
            ks.append(k)
            qs.append(q)
            prods.append(_dot_nt(jnp.concatenate([kb, q.astype(BF16)], axis=0),
                                 jnp.concatenate([kb, kb], axis=0)))
        for g in hs:
            hq = hg * heads + g
            b0 = column(bg, GDN_REP * hq)
            b1 = column(bg, GDN_REP * hq + 1)
            g0 = column(gcb, GDN_V_HEADS + GDN_REP * hq)
            g1 = column(gcb, GDN_V_HEADS + GDN_REP * hq + 1)
            cols.append((b0, b1, g0, g1, g0[CHUNK - 1:CHUNK, :], g1[CHUNK - 1:CHUNK, :]))
            gc_row = gr_ref[g, pl.ds(c, 1), :]
            decay2 = jnp.exp(jnp.where(incl2, jnp.where(second, g1, g0) - gc_row, -jnp.inf))
            a2 = jnp.where(strict2, prods[g][:CHUNK] * jnp.where(second, b1, b0) * decay2, 0.0)
            decays.append(decay2)
            xs.append(jnp.where(eye2, 1.0, 0.0) - a2)
            ps.append(a2.astype(BF16))
            pbds.append(_block_diag2(ps[g], second))
        power = 1
        while 2 * power < CHUNK:
            for g in hs:
                ps[g] = _dot(ps[g], pbds[g]).astype(BF16)
                pbds[g] = _block_diag2(ps[g], second)
            for g in hs:
                xs[g] = xs[g] + _dot(xs[g].astype(BF16), pbds[g])
            power *= 2
        sols, exps = [], []
        for g in hs:
            b0, b1, g0, g1, gl0, gl1 = cols[g]
            e0 = jnp.exp(g0)
            e1 = jnp.exp(g1)
            exps.append((e0, e1))
            v2 = v_ref[pl.ds(r0, CHUNK), g * vw:(g + 1) * vw]
            rhs0 = jnp.concatenate([v2[:, :GDN_DV] * b0, (ks[g] * b0) * e0], axis=1).astype(BF16)
            rhs1 = jnp.concatenate([v2[:, GDN_DV:] * b1, (ks[g] * b1) * e1], axis=1).astype(BF16)
            rhs_bd = jnp.concatenate([jnp.concatenate([rhs0, zero_rhs], axis=1),
                                      jnp.concatenate([zero_rhs, rhs1], axis=1)], axis=0)
            sols.append(_dot(xs[g].astype(BF16), rhs_bd))
        sts, wss = [], []
        for g in hs:
            sol = sols[g]
            e0, e1 = exps[g]
            w2 = jnp.concatenate([sol[:, GDN_DV:vw], sol[:, vw + GDN_DV:]], axis=1)
            qd2 = jnp.concatenate([qs[g] * e0, qs[g] * e1], axis=1)
            sts.append(st_ref[g])
            wss.append(_dot(jnp.concatenate([w2, qd2], axis=0).astype(BF16), sts[g].astype(BF16)))
        new_states, results = [], []
        for g in hs:
            b0, b1, g0, g1, gl0, gl1 = cols[g]
            sol = sols[g]
            u2 = jnp.concatenate([sol[:, :GDN_DV], sol[:, vw:vw + GDN_DV]], axis=1)
            v_new = (u2 - wss[g][:CHUNK]).astype(BF16)
            attn2 = (prods[g][CHUNK:] * decays[g]).astype(BF16)
            o2 = wss[g][CHUNK:] + _dot(attn2, _block_diag2(v_new, second_v))
            k_end2 = jnp.concatenate([ks[g] * jnp.exp(gl0 - g0), ks[g] * jnp.exp(gl1 - g1)],
                                     axis=1).astype(BF16)
            upd = _dot_tn(k_end2, v_new)
            new_states.append(jnp.where(st_r, jnp.exp(gl1), jnp.exp(gl0)) * sts[g]
                              + jnp.where(same_head, upd, 0.0))
            outs = []
            for hh in range(GDN_REP):
                o = o2[:, hh * GDN_DV:(hh + 1) * GDN_DV]
                outs.append(o * lax.rsqrt(jnp.mean(o * o, axis=-1, keepdims=True) + RMS_EPS) * ng)
            z2 = z_ref[pl.ds(r0, CHUNK), g * vw:(g + 1) * vw]
            results.append((jnp.concatenate(outs, axis=1) * _silu(z2)).astype(BF16))
        for g in hs:
            st_ref[g] = new_states[g]
            o_ref[pl.ds(r0, CHUNK), g * vw:(g + 1) * vw] = results[g]
        return carry

    lax.fori_loop(0, rows // CHUNK, chunk, 0)


def _gdn(qk, v, z, bg, gc, gc_rows, ng, *, batch, seq, rows, heads):
    t = batch * seq
    nblk = seq // rows
    cpb = rows // CHUNK
    row = lambda b, h, n: b * nblk + n
    vw = heads * GDN_REP * GDN_DV
    kw = heads * GDN_DK
    return pl.pallas_call(
        functools.partial(_gdn_body, rows=rows, heads=heads),
        grid=(batch, GDN_QK_HEADS // heads, nblk),
        in_specs=[
            pl.BlockSpec((rows, kw), lambda b, h, n: (row(b, h, n), h)),
            pl.BlockSpec((rows, kw), lambda b, h, n: (row(b, h, n), GDN_QK_HEADS // heads + h)),
            pl.BlockSpec((rows, vw), lambda b, h, n: (row(b, h, n), h)),
            pl.BlockSpec((rows, vw), lambda b, h, n: (row(b, h, n), h)),
            pl.BlockSpec((rows, LANES), lambda b, h, n: (row(b, h, n), 0)),
            pl.BlockSpec((rows, LANES), lambda b, h, n: (row(b, h, n), 0)),
            pl.BlockSpec((heads, cpb, GDN_REP * CHUNK), lambda b, h, n: (h, row(b, h, n), 0)),
            pl.BlockSpec((1, GDN_DV), lambda b, h, n: (0, 0)),
        ],
        out_specs=pl.BlockSpec((rows, vw), lambda b, h, n: (row(b, h, n), h)),
        out_shape=jax.ShapeDtypeStruct((t, GDN_VW), BF16),
        scratch_shapes=[pltpu.VMEM((heads, GDN_REP * GDN_DK, GDN_REP * GDN_DV), F32)],
        compiler_params=_params("arbitrary", "arbitrary", "arbitrary"),
        name="gdn",
    )(qk, qk, v, z, bg, gc, gc_rows, ng)


def _out_ln_body(a_ref, w_ref, x_ref, g_ref, b_ref, o_ref):
    mix = _dot(a_ref[...], w_ref[...])
    o_ref[...] = _layer_norm(ALPHA * x_ref[...] + mix, g_ref[...], b_ref[...])


def _out_ln(a, w, x, g, b, *, tm):
    t, d = x.shape
    kdim = a.shape[1]
    const = lambda shape: pl.BlockSpec(shape, lambda i: (0, 0), pipeline_mode=pl.Buffered(1))
    return pl.pallas_call(
        _out_ln_body,
        grid=(t // tm,),
        in_specs=[
            pl.BlockSpec((tm, kdim), lambda i: (i, 0)),
            const((kdim, d)),
            pl.BlockSpec((tm, d), lambda i: (i, 0)),
            const((1, d)), const((1, d)),
        ],
        out_specs=pl.BlockSpec((tm, d), lambda i: (i, 0)),
        out_shape=jax.ShapeDtypeStruct((t, d), F32),
        compiler_params=_params("arbitrary"),
        name="out_ln",
    )(a, w, x, g, b)


def _tiles(batch, seq):
    tm = min(512, seq)
    return dict(tm=tm, ffn_tm=tm, gla_rows=min(256, seq), s5_rows=min(512, seq), gdn_rows=min(512, seq),
                gdn_heads=8)


def _ffn(x, wg, wu, wd, g, b, tm):
    return _ffn_ln(x, wg.astype(BF16), wu.astype(BF16), wd.astype(BF16), g[None], b[None], tm=tm, tf=512)


def _pad_cols(w, n):
    return jnp.pad(w, ((0, 0), (0, n - w.shape[1])))


def _gla_s5_layer(x, xb, p, i, g, b, batch, seq, tl):
    w_in = p['ab_w_in'][i]
    w_gate = _pad_cols(w_in[:, AB_A:AB_U], LANES).astype(BF16)
    proj = _proj(xb, w_in[:, :AB_A].astype(BF16), tm=tl['tm'], tn=1024)
    u = _proj(xb, w_in[:, AB_U:].astype(BF16), tm=tl['tm'], tn=1024)
    gate = _proj(xb, w_gate, tm=tl['tm'], tn=LANES)
    wlr = jnp.pad(p['gla_w_lr'][i], ((0, LANES - GLA_RANK), (0, 0))).astype(BF16)
    o_gla = _gla(proj, gate, wlr, p['gla_b_lr'][i][None], p['gla_norm_g'][i][None],
                 batch=batch, seq=seq, rows=tl['gla_rows'])
    bbd, cbd, pw = _s5_tables(p['s5_lam_re'][i], p['s5_lam_im'][i], p['s5_b_re'][i], p['s5_b_im'][i],
                              p['s5_c_re'][i], p['s5_c_im'][i], p['s5_log_step'][i])
    ys = _s5(u, bbd, cbd, pw, p['s5_d'][i][None], batch=batch, seq=seq, rows=tl['s5_rows'], u_col0=0)
    w_out = p['ab_w_out'][i].astype(BF16)
    return _mix0_out(o_gla, ys, p['s5_glu_w'][i].astype(BF16), p['s5_glu_b'][i][None],
                     w_out[:GLA_VW], w_out[GLA_VW:], x, g[None], b[None], tm=tl['tm'])


def _gdn_layer(x, xb, p, i, g, b, batch, seq, tl):
    t = batch * seq
    w_in = p['gdn_w_in'][i]
    conv_w = p['gdn_conv_w'][i]
    tm = tl['tm']
    qk = _conv_proj(xb, w_in[:, :2 * GDN_KW].astype(BF16), conv_w[:, :2 * GDN_KW], tm=tm, tn=1024,
                    seq=seq, l2_heads=True, n_scaled_tiles=GDN_KW // 1024, scale=GDN_DK ** -0.5)
    v = _conv_proj(xb, w_in[:, 2 * GDN_KW:GDN_QKV].astype(BF16), conv_w[:, 2 * GDN_KW:], tm=tm,
                   tn=1024, seq=seq)
    z = _proj(xb, w_in[:, GDN_Z:GDN_B].astype(BF16), tm=tm, tn=1024)
    w_gate = _pad_cols(w_in[:, GDN_B:GDN_IN], LANES).astype(BF16)
    lane_pad = lambda a: jnp.pad(a, (GDN_V_HEADS, LANES - 2 * GDN_V_HEADS))[None]
    bg, gc = _gate_proj(xb, w_gate, lane_pad(p['gdn_a_log'][i]), lane_pad(p['gdn_dt_bias'][i]), tm=tm)
    gc_rows = gc[:, GDN_V_HEADS:2 * GDN_V_HEADS].reshape(t // CHUNK, CHUNK, GDN_QK_HEADS, GDN_REP)
    gc_rows = gc_rows.transpose(2, 0, 3, 1).reshape(GDN_QK_HEADS, t // CHUNK, GDN_REP * CHUNK)
    o = _gdn(qk, v, z, bg, gc, gc_rows, p['gdn_norm_g'][i][None], batch=batch, seq=seq,
             rows=tl['gdn_rows'], heads=tl['gdn_heads'])
    return _out_ln(o, p['gdn_w_out'][i].astype(BF16), x, g[None], b[None], tm=min(256, tm))


def kernel(x, ffn_a_gate, ffn_a_up, ffn_a_down, ffn_b_gate, ffn_b_up, ffn_b_down, ln_g, ln_b, ab_w_in, gla_w_lr, gla_b_lr, gla_norm_g, s5_lam_re, s5_lam_im, s5_b_re, s5_b_im, s5_c_re, s5_c_im, s5_d, s5_log_step, s5_glu_w, s5_glu_b, ab_w_out, gdn_w_in, gdn_conv_w, gdn_a_log, gdn_dt_bias, gdn_norm_g, gdn_w_out):
    batch, seq, d = x.shape
    assert d == D_MODEL and seq % CHUNK == 0
    p = dict(ab_w_in=ab_w_in, gla_w_lr=gla_w_lr, gla_b_lr=gla_b_lr, gla_norm_g=gla_norm_g,
             s5_lam_re=s5_lam_re, s5_lam_im=s5_lam_im, s5_b_re=s5_b_re, s5_b_im=s5_b_im,
             s5_c_re=s5_c_re, s5_c_im=s5_c_im, s5_d=s5_d, s5_log_step=s5_log_step,
             s5_glu_w=s5_glu_w, s5_glu_b=s5_glu_b, ab_w_out=ab_w_out, gdn_w_in=gdn_w_in,
             gdn_conv_w=gdn_conv_w, gdn_a_log=gdn_a_log, gdn_dt_bias=gdn_dt_bias,
             gdn_norm_g=gdn_norm_g, gdn_w_out=gdn_w_out)
    tl = _tiles(batch, seq)
    h = x.reshape(batch * seq, d).astype(F32)
    for layer in range(DEPTH):
        h, hb = _ffn(h, ffn_a_gate[layer], ffn_a_up[layer], ffn_a_down[layer],
                     ln_g[layer, 0], ln_b[layer, 0], tl['ffn_tm'])
        i = layer // 2
        mixer = _gla_s5_layer if layer % 2 == 0 else _gdn_layer
        h = mixer(h, hb, p, i, ln_g[layer, 1], ln_b[layer, 1], batch, seq, tl)
        h, _ = _ffn(h, ffn_b_gate[layer], ffn_b_up[layer], ffn_b_down[layer],
                    ln_g[layer, 2], ln_b[layer, 2], tl['ffn_tm'])
    return h.reshape(batch, seq, d)
```

```python
import functools
import math

import jax
import jax.numpy as jnp
import numpy as np
from jax import lax
from jax.experimental import pallas as pl
from jax.experimental.pallas import tpu as pltpu

F32 = jnp.float32
BF16 = jnp.bfloat16

LANES = 128
SUBLANES = 8
VMEM_LIMIT_BYTES = 60 * 1024 * 1024
LN_ROWS = 64
CONV_SUB = 256

D_MODEL = 2048
DEPTH = 2
ALPHA = (2.0 * DEPTH) ** 0.25
MACARON = 0.5
LN_EPS = 1e-5
RMS_EPS = 1e-6
L2_EPS = 1e-6
D_FF = 5504

GLA_HEADS = 4
GLA_DK = 128
GLA_DV = 256
GLA_RANK = 16
GLA_TAU = 16.0
CHUNK = 64
GLA_KW = GLA_HEADS * GLA_DK
GLA_VW = GLA_HEADS * GLA_DV

S5_WIDTH = 1024
S5_GROUP = 16
S5_GROUPS = 64
S5_STATE = 64
S5_TILE_GROUPS = LANES // S5_GROUP
S5_TILES = S5_WIDTH // LANES
S5_TILE_STATES = S5_TILE_GROUPS * S5_STATE

AB_K = GLA_KW
AB_V = AB_K + GLA_KW
AB_G = AB_V + GLA_VW
AB_A = AB_G + GLA_VW
AB_U = AB_A + GLA_RANK
AB_IN = AB_U + S5_WIDTH

GDN_QK_HEADS = 16
GDN_V_HEADS = 32
GDN_DK = 128
GDN_DV = 128
GDN_KW = GDN_QK_HEADS * GDN_DK
GDN_VW = GDN_V_HEADS * GDN_DV
GDN_QKV = 2 * GDN_KW + GDN_VW
GDN_Z = GDN_QKV
GDN_B = GDN_Z + GDN_VW
GDN_A = GDN_B + GDN_V_HEADS
GDN_IN = GDN_A + GDN_V_HEADS


def _params(*sem):
    return pltpu.CompilerParams(dimension_semantics=sem, vmem_limit_bytes=VMEM_LIMIT_BYTES)


def _dot(a, b):
    return jnp.dot(a, b, preferred_element_type=F32)


def _dot_nt(a, b):
    return lax.dot_general(a, b, (((1,), (1,)), ((), ())), preferred_element_type=F32)


def _dot_tn(a, b):
    return lax.dot_general(a, b, (((0,), (0,)), ((), ())), preferred_element_type=F32)


def _layer_norm(y, g, b):
    mu = jnp.mean(y, axis=-1, keepdims=True)
    yc = y - mu
    var = jnp.mean(yc * yc, axis=-1, keepdims=True)
    return yc * lax.rsqrt(var + LN_EPS) * g + b


def _silu(v):
    return v * jax.nn.sigmoid(v)


def _softplus(v):
    return jnp.maximum(v, 0.0) + jnp.log1p(jnp.exp(-jnp.abs(v)))


def _split3(v):
    hi = v.astype(BF16)
    r1 = v - hi.astype(F32)
    mid = r1.astype(BF16)
    lo = (r1 - mid.astype(F32)).astype(BF16)
    return hi, mid, lo


def _tril_mask(n, strict=False):
    r = lax.broadcasted_iota(jnp.int32, (n, n), 0)
    c = lax.broadcasted_iota(jnp.int32, (n, n), 1)
    return (c < r) if strict else (c <= r)


def _chunk_cumsum(v, rows):
    r = lax.broadcasted_iota(jnp.int32, (rows, rows), 0)
    c = lax.broadcasted_iota(jnp.int32, (rows, rows), 1)
    tri = ((c <= r) & ((c // CHUNK) == (r // CHUNK))).astype(BF16)
    hi, mid, lo = _split3(v)
    return _dot(tri, hi) + _dot(tri, mid) + _dot(tri, lo)


def _ffn_body(x_ref, wg_ref, wu_ref, wd_ref, wgt_ref, wut_ref, wdt_ref, g_ref, b_ref, o_ref, *rest, nj):
    ob_ref, xb_ref = rest if len(rest) == 2 else (None, rest[0])
    j = pl.program_id(1)

    @pl.when(j == 0)
    def _():
        xb_ref[...] = x_ref[...].astype(BF16)
        o_ref[...] = jnp.zeros_like(o_ref)

    def down(wg, wu, wd):
        xb = xb_ref[...]
        h = (_silu(_dot(xb, wg.astype(BF16))) * _dot(xb, wu.astype(BF16))).astype(BF16)
        return _dot(h, wd.astype(BF16))

    @pl.when(j < nj - 1)
    def _():
        o_ref[...] += down(wg_ref[...], wu_ref[...], wd_ref[...])

    @pl.when(j == nj - 1)
    def _():
        o_ref[...] += down(wgt_ref[...], wut_ref[...], wdt_ref[...])
        g = g_ref[...]
        b = b_ref[...]

        def rows(r, carry):
            sl = pl.ds(pl.multiple_of(r * LN_ROWS, LN_ROWS), LN_ROWS)
            yn = _layer_norm(ALPHA * x_ref[sl, :] + MACARON * o_ref[sl, :], g, b)
            o_ref[sl, :] = yn
            if ob_ref is not None:
                ob_ref[sl, :] = yn.astype(BF16)
            return carry

        lax.fori_loop(0, o_ref.shape[0] // LN_ROWS, rows, 0)


def _ffn_ln(x, wg, wu, wd, layer, g, b, *, tm, tf, emit_bf16):
    t, d = x.shape
    f = wg.shape[2]
    n_full = f // tf
    f_tail = f - n_full * tf
    assert f_tail > 0 and f_tail % LANES == 0
    nj = n_full + 1
    last = n_full - 1
    const = lambda shape: pl.BlockSpec(shape, lambda i, j: (0, 0), pipeline_mode=pl.Buffered(1))
    tile = pl.BlockSpec((tm, d), lambda i, j: (i, 0))
    out_specs = [tile, tile] if emit_bf16 else [tile]
    out_shape = [jax.ShapeDtypeStruct((t, d), F32), jax.ShapeDtypeStruct((t, d), BF16)][:len(out_specs)]
    return pl.pallas_call(
        functools.partial(_ffn_body, nj=nj),
        grid=(t // tm, nj),
        in_specs=[
            pl.BlockSpec((tm, d), lambda i, j: (i, 0), pipeline_mode=pl.Buffered(1)),
            pl.BlockSpec((None, d, tf), lambda i, j: (layer, 0, jnp.minimum(j, last))),
            pl.BlockSpec((None, d, tf), lambda i, j: (layer, 0, jnp.minimum(j, last))),
            pl.BlockSpec((None, tf, d), lambda i, j: (layer, jnp.minimum(j, last), 0)),
            const((d, f_tail)), const((d, f_tail)), const((f_tail, d)),
            const((1, d)), const((1, d)),
        ],
        out_specs=out_specs,
        out_shape=out_shape,
        scratch_shapes=[pltpu.VMEM((tm, d), BF16)],
        compiler_params=_params("arbitrary", "arbitrary"),
        name="ffn_ln",
    )(x, wg, wu, wd, wg[layer, :, f - f_tail:], wu[layer, :, f - f_tail:], wd[layer, f - f_tail:], g, b)


def _proj_body(x_ref, w_ref, o_ref, wb_ref):
    @pl.when(pl.program_id(1) == 0)
    def _():
        wb_ref[...] = w_ref[...].astype(BF16)

    o_ref[...] = _dot(x_ref[...], wb_ref[...])


def _proj(xb, w, *, tm, tn, col0=0, n=None):
    t, k = xb.shape
    n = w.shape[1] if n is None else n
    assert col0 % tn == 0 and n % tn == 0
    j0 = col0 // tn
    return pl.pallas_call(
        _proj_body,
        grid=(n // tn, t // tm),
        in_specs=[
            pl.BlockSpec((tm, k), lambda j, i: (i, 0)),
            pl.BlockSpec((k, tn), lambda j, i: (0, j0 + j)),
        ],
        out_specs=pl.BlockSpec((tm, tn), lambda j, i: (i, j)),
        out_shape=jax.ShapeDtypeStruct((t, n), F32),
        scratch_shapes=[pltpu.VMEM((k, tn), BF16)],
        compiler_params=_params("arbitrary", "arbitrary"),
        name="proj",
    )(xb, w)


def _conv_proj_body(x_ref, w_ref, cw_ref, o_ref, wb_ref, buf_a, buf_b, *, tm, tn, tiles_per_seq,
                    l2_heads, n_scaled_tiles, scale):
    jt = pl.program_id(0)
    i = pl.program_id(1)

    @pl.when(i == 0)
    def _():
        wb_ref[...] = w_ref[...].astype(BF16)
        buf_b[...] = jnp.zeros_like(buf_b)

    if l2_heads:
        fac = jnp.where(jt < n_scaled_tiles, jnp.float32(scale), jnp.float32(1.0))

    def post(v):
        v = _silu(v)
        if not l2_heads:
            return v
        out = []
        for h in range(CONV_SUB // LANES):
            vh = v[:, h * LANES:(h + 1) * LANES]
            ss = jnp.sum(vh * vh, axis=-1, keepdims=True)
            out.append(vh * lax.rsqrt(ss + L2_EPS) * fac)
        return jnp.concatenate(out, axis=1)

    subs = [slice(s * CONV_SUB, (s + 1) * CONV_SUB) for s in range(tn // CONV_SUB)]

    def step(stage_ref, prev_ref):
        x = x_ref[...]
        keep = jnp.where(i % tiles_per_seq == 0, jnp.float32(0.0), jnp.float32(1.0))
        for cols in subs:
            raw = _dot(x, wb_ref[:, cols])
            cw = cw_ref[:, cols]
            acc = prev_ref[SUBLANES:SUBLANES + tm, cols] * cw[3:4]
            for d in (1, 2, 3):
                acc = acc + prev_ref[SUBLANES - d:SUBLANES - d + tm, cols] * cw[3 - d:4 - d]
            o_ref[:, cols] = post(acc)
            stage_ref[0:SUBLANES, cols] = prev_ref[tm:tm + SUBLANES, cols] * keep
            stage_ref[SUBLANES:SUBLANES + tm, cols] = raw

    @pl.when(i % 2 == 0)
    def _():
        step(buf_a, buf_b)

    @pl.when(i % 2 == 1)
    def _():
        step(buf_b, buf_a)


def _conv_proj(xb, w, cw, *, tm, tn, seq, col0, n, l2_heads=False, n_scaled_tiles=0, scale=1.0):
    t, k = xb.shape
    nt = t // tm
    assert col0 % tn == 0 and n % tn == 0
    j0 = col0 // tn
    body = functools.partial(_conv_proj_body, tm=tm, tn=tn, tiles_per_seq=seq // tm,
                             l2_heads=l2_heads, n_scaled_tiles=n_scaled_tiles, scale=scale)
    return pl.pallas_call(
        body,
        grid=(n // tn, nt + 1),
        in_specs=[
            pl.BlockSpec((tm, k), lambda j, i: (jnp.minimum(i, nt - 1), 0)),
            pl.BlockSpec((k, tn), lambda j, i: (0, j0 + j)),
            pl.BlockSpec((4, tn), lambda j, i: (0, j0 + j)),
        ],
        out_specs=pl.BlockSpec((tm, tn), lambda j, i: (jnp.maximum(i - 1, 0), j)),
        out_shape=jax.ShapeDtypeStruct((t, n), F32),
        scratch_shapes=[pltpu.VMEM((k, tn), BF16),
                        pltpu.VMEM((SUBLANES + tm, tn), F32),
                        pltpu.VMEM((SUBLANES + tm, tn), F32)],
        compiler_params=_params("arbitrary", "arbitrary"),
        name="conv_proj",
    )(xb, w, cw)


def _gate_proj_body(x_ref, w_ref, alog_ref, dtb_ref, bg_ref, gc_ref, *, tm):
    raw = _dot(x_ref[...], w_ref[...])
    lane = lax.broadcasted_iota(jnp.int32, raw.shape, 1)
    beta = jax.nn.sigmoid(raw)
    g = -jnp.exp(alog_ref[...]) * _softplus(raw + dtb_ref[...])
    g = jnp.where((lane >= GDN_V_HEADS) & (lane < 2 * GDN_V_HEADS), g, 0.0)
    bg_ref[...] = jnp.where(lane < GDN_V_HEADS, beta, g)
    gc_ref[...] = _chunk_cumsum(g, tm)


def _gate_proj(xb, w, alog, dtb, *, tm):
    t, k = xb.shape
    return pl.pallas_call(
        functools.partial(_gate_proj_body, tm=tm),
        grid=(t // tm,),
        in_specs=[
            pl.BlockSpec((tm, k), lambda i: (i, 0)),
            pl.BlockSpec((k, LANES), lambda i: (0, 0)),
            pl.BlockSpec((1, LANES), lambda i: (0, 0)),
            pl.BlockSpec((1, LANES), lambda i: (0, 0)),
        ],
        out_specs=[pl.BlockSpec((tm, LANES), lambda i: (i, 0))] * 2,
        out_shape=[jax.ShapeDtypeStruct((t, LANES), F32)] * 2,
        compiler_params=_params("arbitrary"),
        name="gate_proj",
    )(xb, w, alog, dtb)


def _gla_body(q_ref, k_ref, v_ref, go_ref, a_ref, wlr_ref, blr_ref, ng_ref, o_ref, st_ref, *, rows):
    @pl.when(pl.program_id(1) == 0)
    def _():
        st_ref[...] = jnp.zeros_like(st_ref)

    causal = _tril_mask(CHUNK)
    ng = ng_ref[...]
    heads = range(GLA_HEADS)
    chunks = range(rows // CHUNK)
    head = lambda a, h: a[:, h * GLA_DK:(h + 1) * GLA_DK]

    z = _dot(a_ref[...].astype(BF16), wlr_ref[...]) + blr_ref[...]
    b = _chunk_cumsum(-_softplus(-z) / GLA_TAU, rows)
    k = k_ref[...]
    q_dec = (q_ref[...] * (GLA_DK ** -0.5) * jnp.exp(b)).astype(BF16)
    k_inv = (k * jnp.exp(-b)).astype(BF16)
    rows_of = lambda a, c: a[c * CHUNK:(c + 1) * CHUNK]
    b_last = [rows_of(b, c)[CHUNK - 1:CHUNK] for c in chunks]
    k_end = [(rows_of(k, c) * jnp.exp(b_last[c] - rows_of(b, c))).astype(BF16) for c in chunks]
    scores = [[jnp.where(causal, _dot_nt(head(rows_of(q_dec, c), h), head(rows_of(k_inv, c), h)),
                         0.0).astype(BF16) for h in heads] for c in chunks]
    for c in chunks:
        sl = slice(c * CHUNK, (c + 1) * CHUNK)
        sts = [st_ref[h] for h in heads]
        vs = [v_ref[sl, h * GLA_DV:(h + 1) * GLA_DV].astype(BF16) for h in heads]
        qd = rows_of(q_dec, c)
        outs = [_dot(scores[c][h], vs[h]) + _dot_nt(head(qd, h), sts[h].astype(BF16)) for h in heads]
        decay = jnp.exp(b_last[c])
        new = [sts[h] * head(decay, h) + _dot_tn(vs[h], head(k_end[c], h)) for h in heads]
        for h in heads:
            st_ref[h] = new[h]
        normed = [o * lax.rsqrt(jnp.mean(o * o, axis=-1, keepdims=True) + RMS_EPS) * ng for o in outs]
        o_ref[sl, :] = (jnp.concatenate(normed, axis=1) * _silu(go_ref[sl, :])).astype(BF16)


def _gla(proj, gate, wlr, blr, ng, *, batch, seq, rows):
    t = batch * seq
    nblk = seq // rows
    row = lambda b, n: b * nblk + n
    return pl.pallas_call(
        functools.partial(_gla_body, rows=rows),
        grid=(batch, nblk),
        in_specs=[
            pl.BlockSpec((rows, GLA_KW), lambda b, n: (row(b, n), 0)),
            pl.BlockSpec((rows, GLA_KW), lambda b, n: (row(b, n), 1)),
            pl.BlockSpec((rows, GLA_VW), lambda b, n: (row(b, n), AB_V // GLA_VW)),
            pl.BlockSpec((rows, GLA_VW), lambda b, n: (row(b, n), AB_G // GLA_VW)),
            pl.BlockSpec((rows, LANES), lambda b, n: (row(b, n), 0)),
            pl.BlockSpec((LANES, GLA_KW), lambda b, n: (0, 0)),
            pl.BlockSpec((1, GLA_KW), lambda b, n: (0, 0)),
            pl.BlockSpec((1, GLA_DV), lambda b, n: (0, 0)),
        ],
        out_specs=pl.BlockSpec((rows, GLA_VW), lambda b, n: (row(b, n), 0)),
        out_shape=jax.ShapeDtypeStruct((t, GLA_VW), BF16),
        scratch_shapes=[pltpu.VMEM((GLA_HEADS, GLA_DV, GLA_DK), F32)],
        compiler_params=_params("arbitrary", "arbitrary"),
        name="gla",
    )(proj, proj, proj, proj, gate, wlr, blr, ng)


def _s5_body(u_ref, bbd_ref, cbd_ref, pw_ref, d_ref, y_ref, xs_ref, cr_ref, ci_ref, *, rows):
    @pl.when(pl.program_id(2) == 0)
    def _():
        cr_ref[...] = jnp.zeros_like(cr_ref)
        ci_ref[...] = jnp.zeros_like(ci_ref)

    ns = S5_TILE_STATES
    u = u_ref[...]
    bu = _dot(u.astype(BF16), bbd_ref[0])
    levels = [(pw_ref[0, 2 * l], pw_ref[0, 2 * l + 1], 1 << l) for l in range(3)]
    qr, qi = pw_ref[0, 6], pw_ref[0, 7]
    cr = cr_ref[...]
    ci = ci_ref[...]
    for blk in range(rows // SUBLANES):
        sl = slice(blk * SUBLANES, (blk + 1) * SUBLANES)
        xr = bu[sl, :ns]
        xi = bu[sl, ns:]
        for pr, pi, s in levels:
            sr = pltpu.roll(xr, s, 0)
            si = pltpu.roll(xi, s, 0)
            xr, xi = xr + (pr * sr - pi * si), xi + (pr * si + pi * sr)
        xr, xi = xr + (qr * cr - qi * ci), xi + (qr * ci + qi * cr)
        cr = jnp.broadcast_to(xr[SUBLANES - 1:SUBLANES], (SUBLANES, ns))
        ci = jnp.broadcast_to(xi[SUBLANES - 1:SUBLANES], (SUBLANES, ns))
        xs_ref[sl, :ns] = xr
        xs_ref[sl, ns:] = xi
    cr_ref[...] = cr
    ci_ref[...] = ci
    y = _dot(xs_ref[...].astype(BF16), cbd_ref[0]) + d_ref[...] * u
    y3 = y * y * y
    y_ref[...] = y * (0.5 * (1.0 + jnp.tanh(np.float32(math.sqrt(2.0 / math.pi)) * (y + 0.044715 * y3))))


def _s5(proj, bbd, cbd, pw, d, *, batch, seq, rows, u_col0):
    t = batch * seq
    nblk = seq // rows
    ns = S5_TILE_STATES
    return pl.pallas_call(
        functools.partial(_s5_body, rows=rows),
        grid=(batch, S5_TILES, nblk),
        in_specs=[
            pl.BlockSpec((rows, LANES), lambda b, j, n: (b * nblk + n, u_col0 + j)),
            pl.BlockSpec((1, LANES, 2 * ns), lambda b, j, n: (j, 0, 0)),
            pl.BlockSpec((1, 2 * ns, LANES), lambda b, j, n: (j, 0, 0)),
            pl.BlockSpec((1, 8, SUBLANES, ns), lambda b, j, n: (j, 0, 0, 0)),
            pl.BlockSpec((1, LANES), lambda b, j, n: (0, j)),
        ],
        out_specs=pl.BlockSpec((rows, LANES), lambda b, j, n: (b * nblk + n, j)),
        out_shape=jax.ShapeDtypeStruct((t, S5_WIDTH), F32),
        scratch_shapes=[pltpu.VMEM((rows, 2 * ns), F32),
                        pltpu.VMEM((SUBLANES, ns), F32),
                        pltpu.VMEM((SUBLANES, ns), F32)],
        compiler_params=_params("arbitrary", "arbitrary", "arbitrary"),
        name="s5",
    )(proj, bbd, cbd, pw, d)


def _s5_tables(lam_re, lam_im, b_re, b_im, c_re, c_im, log_step):
    step = jnp.exp(log_step)[:, None]
    mag = jnp.exp(lam_re * step)
    a_re = mag * jnp.cos(lam_im * step)
    a_im = mag * jnp.sin(lam_im * step)
    inv_den = 1.0 / (lam_re * lam_re + lam_im * lam_im)
    f_re = ((a_re - 1.0) * lam_re + a_im * lam_im) * inv_den
    f_im = (a_im * lam_re - (a_re - 1.0) * lam_im) * inv_den
    bb_re = f_re[..., None] * b_re - f_im[..., None] * b_im
    bb_im = f_re[..., None] * b_im + f_im[..., None] * b_re
    tg, nt = S5_TILE_GROUPS, S5_TILES
    eye = jnp.eye(tg, dtype=F32)

    def in_table(bb):
        bb = bb.reshape(nt, tg, S5_STATE, S5_GROUP)
        return jnp.einsum('jgph,gk->jghkp', bb, eye).reshape(nt, LANES, S5_TILE_STATES)

    def out_table(cc):
        cc = cc.reshape(nt, tg, S5_GROUP, S5_STATE)
        return jnp.einsum('jghp,gk->jgpkh', cc, eye).reshape(nt, S5_TILE_STATES, LANES)

    bbd = jnp.concatenate([in_table(bb_re), in_table(bb_im)], axis=2).astype(BF16)
    cbd = jnp.concatenate([out_table(c_re), -out_table(c_im)], axis=1).astype(BF16)

    def cmul(x, y):
        return x[0] * y[0] - x[1] * y[1], x[0] * y[1] + x[1] * y[0]

    a1 = (a_re.reshape(nt, 1, S5_TILE_STATES), a_im.reshape(nt, 1, S5_TILE_STATES))
    powers = [a1]
    for _ in range(SUBLANES - 1):
        powers.append(cmul(powers[-1], a1))
    row = jnp.arange(SUBLANES)[None, :, None]
    tabs = []
    for s in (1, 2, 4):
        pr, pi = powers[s - 1]
        tabs.append(jnp.where(row >= s, pr, 0.0))
        tabs.append(jnp.where(row >= s, pi, 0.0))
    tabs.append(jnp.concatenate([p[0] for p in powers], axis=1))
    tabs.append(jnp.concatenate([p[1] for p in powers], axis=1))
    pw = jnp.stack([jnp.broadcast_to(tb, (nt, SUBLANES, S5_TILE_STATES)) for tb in tabs], axis=1)
    return bbd, cbd, pw.astype(F32)


def _mix0_out_body(og_ref, ys_ref, gw_ref, gb_ref, w1_ref, w2_ref, x_ref, g_ref, b_ref, o_ref):
    ys = ys_ref[...]
    gate = jax.nn.sigmoid(_dot(ys.astype(BF16), gw_ref[...]) + gb_ref[...])
    os5 = (ys * gate).astype(BF16)
    mix = _dot(og_ref[...], w1_ref[...]) + _dot(os5, w2_ref[...])
    o_ref[...] = _layer_norm(ALPHA * x_ref[...] + mix, g_ref[...], b_ref[...])


def _mix0_out(og, ys, gw, gb, w1, w2, x, g, b, *, tm):
    t, d = x.shape
    const = lambda shape: pl.BlockSpec(shape, lambda i: (0, 0))
    return pl.pallas_call(
        _mix0_out_body,
        grid=(t // tm,),
        in_specs=[
            pl.BlockSpec((tm, GLA_VW), lambda i: (i, 0)),
            pl.BlockSpec((tm, S5_WIDTH), lambda i: (i, 0)),
            const((S5_WIDTH, S5_WIDTH)), const((1, S5_WIDTH)),
            const((GLA_VW, d)), const((S5_WIDTH, d)),
            pl.BlockSpec((tm, d), lambda i: (i, 0)),
            const((1, d)), const((1, d)),
        ],
        out_specs=pl.BlockSpec((tm, d), lambda i: (i, 0)),
        out_shape=jax.ShapeDtypeStruct((t, d), F32),
        compiler_params=_params("arbitrary"),
        name="mix0_out",
    )(og, ys, gw, gb, w1, w2, x, g, b)


GDN_REP = GDN_V_HEADS // GDN_QK_HEADS


def _block_diag2(m, second):
    zero = jnp.zeros_like(m)
    return jnp.concatenate([jnp.where(second, zero, m), jnp.where(second, m, zero)], axis=0)


def _gdn_body(q_ref, k_ref, v_ref, z_ref, bg_ref, gc_ref, gr_ref, ng_ref, o_ref, st_ref, *, rows, heads):
    hg = pl.program_id(1)
    c2 = 2 * CHUNK
    vw = GDN_REP * GDN_DV

    @pl.when(pl.program_id(2) == 0)
    def _():
        st_ref[...] = jnp.zeros_like(st_ref)

    lane2 = lax.broadcasted_iota(jnp.int32, (CHUNK, c2), 1)
    row2 = lax.broadcasted_iota(jnp.int32, (CHUNK, c2), 0)
    second = lane2 >= CHUNK
    pos = jnp.where(second, lane2 - CHUNK, lane2)
    incl2 = pos <= row2
    strict2 = pos < row2
    eye2 = pos == row2
    lane_bg = lax.broadcasted_iota(jnp.int32, (CHUNK, LANES), 1)
    second_v = lax.broadcasted_iota(jnp.int32, (CHUNK, vw), 1) >= GDN_DV
    st_r = lax.broadcasted_iota(jnp.int32, (GDN_REP * GDN_DK, vw), 0) >= GDN_DK
    st_c = lax.broadcasted_iota(jnp.int32, (GDN_REP * GDN_DK, vw), 1) >= GDN_DV
    same_head = st_r == st_c
    ng = ng_ref[...]
    zero_rhs = jnp.zeros((CHUNK, GDN_DV + GDN_DK), BF16)

    def chunk(c, carry):
        r0 = pl.multiple_of(c * CHUNK, CHUNK)
        bg = bg_ref[pl.ds(r0, CHUNK), :]
        gcb = gc_ref[pl.ds(r0, CHUNK), :]

        def column(arr, idx):
            return jnp.sum(jnp.where(lane_bg == idx, arr, 0.0), axis=-1, keepdims=True)

        hs = range(heads)
        ks, qs, prods, cols, decays, xs, ps, pbds = [], [], [], [], [], [], [], []
        for g in hs:
            k = k_ref[pl.ds(r0, CHUNK), g * GDN_DK:(g + 1) * GDN_DK]
            q = q_ref[pl.ds(r0, CHUNK), g * GDN_DK:(g + 1) * GDN_DK]
            kb = k.astype(BF16)
            ks.append(k)
            qs.append(q)
            prods.append(_dot_nt(jnp.concatenate([kb, q.astype(BF16)], axis=0),
                                 jnp.concatenate([kb, kb], axis=0)))
        for g in hs:
            hq = hg * heads + g
            b0 = column(bg, GDN_REP * hq)
            b1 = column(bg, GDN_REP * hq + 1)
            g0 = column(gcb, GDN_V_HEADS + GDN_REP * hq)
            g1 = column(gcb, GDN_V_HEADS + GDN_REP * hq + 1)
            cols.append((b0, b1, g0, g1, g0[CHUNK - 1:CHUNK, :], g1[CHUNK - 1:CHUNK, :]))
            gc_row = gr_ref[g, pl.ds(c, 1), :]
            decay2 = jnp.exp(jnp.where(incl2, jnp.where(second, g1, g0) - gc_row, -jnp.inf))
            a2 = jnp.where(strict2, prods[g][:CHUNK] * jnp.where(second, b1, b0) * decay2, 0.0)
            decays.append(decay2)
            xs.append(jnp.where(eye2, 1.0, 0.0) - a2)
            ps.append(a2.astype(BF16))
            pbds.append(_block_diag2(ps[g], second))
        power = 1
        while 2 * power < CHUNK:
            for g in hs:
                ps[g] = _dot(ps[g], pbds[g]).astype(BF16)
                pbds[g] = _block_diag2(ps[g], second)
            for g in hs:
                xs[g] = xs[g] + _dot(xs[g].astype(BF16), pbds[g])
            power *= 2
        sols, exps = [], []
        for g in hs:
            b0, b1, g0, g1, gl0, gl1 = cols[g]
            e0 = jnp.exp(g0)
            e1 = jnp.exp(g1)
            exps.append((e0, e1))
            v2 = v_ref[pl.ds(r0, CHUNK), g * vw:(g + 1) * vw]
            rhs0 = jnp.concatenate([v2[:, :GDN_DV] * b0, (ks[g] * b0) * e0], axis=1).astype(BF16)
            rhs1 = jnp.concatenate([v2[:, GDN_DV:] * b1, (ks[g] * b1) * e1], axis=1).astype(BF16)
            rhs_bd = jnp.concatenate([jnp.concatenate([rhs0, zero_rhs], axis=1),
                                      jnp.concatenate([zero_rhs, rhs1], axis=1)], axis=0)
            sols.append(_dot(xs[g].astype(BF16), rhs_bd))
        sts, wss = [], []
        for g in hs:
            sol = sols[g]
            e0, e1 = exps[g]
            w2 = jnp.concatenate([sol[:, GDN_DV:vw], sol[:, vw + GDN_DV:]], axis=1)
            qd2 = jnp.concatenate([qs[g] * e0, qs[g] * e1], axis=1)
            sts.append(st_ref[g])
            wss.append(_dot(jnp.concatenate([w2, qd2], axis=0).astype(BF16), sts[g].astype(BF16)))
        new_states, results = [], []
        for g in hs:
            b0, b1, g0, g1, gl0, gl1 = cols[g]
            sol = sols[g]
            u2 = jnp.concatenate([sol[:, :GDN_DV], sol[:, vw:vw + GDN_DV]], axis=1)
            v_new = (u2 - wss[g][:CHUNK]).astype(BF16)
            attn2 = (prods[g][CHUNK:] * decays[g]).astype(BF16)
            o2 = wss[g][CHUNK:] + _dot(attn2, _block_diag2(v_new, second_v))
            k_end2 = jnp.concatenate([ks[g] * jnp.exp(gl0 - g0), ks[g] * jnp.exp(gl1 - g1)],
                                     axis=1).astype(BF16)
            upd = _dot_tn(k_end2, v_new)
            new_states.append(jnp.where(st_r, jnp.exp(gl1), jnp.exp(gl0)) * sts[g]
                              + jnp.where(same_head, upd, 0.0))
            outs = []
            for hh in range(GDN_REP):
                o = o2[:, hh * GDN_DV:(hh + 1) * GDN_DV]
                outs.append(o * lax.rsqrt(jnp.mean(o * o, axis=-1, keepdims=True) + RMS_EPS) * ng)
            z2 = z_ref[pl.ds(r0, CHUNK), g * vw:(g + 1) * vw]
            results.append((jnp.concatenate(outs, axis=1) * _silu(z2)).astype(BF16))
        for g in hs:
            st_ref[g] = new_states[g]
            o_ref[pl.ds(r0, CHUNK), g * vw:(g + 1) * vw] = results[g]
        return carry

    lax.fori_loop(0, rows // CHUNK, chunk, 0)


def _gdn(qk, v, z, bg, gc, gc_rows, ng, *, batch, seq, rows, heads):
    t = batch * seq
    nblk = seq // rows
    cpb = rows // CHUNK
    row = lambda b, h, n: b * nblk + n
    vw = heads * GDN_REP * GDN_DV
    kw = heads * GDN_DK
    return pl.pallas_call(
        functools.partial(_gdn_body, rows=rows, heads=heads),
        grid=(batch, GDN_QK_HEADS // heads, nblk),
        in_specs=[
            pl.BlockSpec((rows, kw), lambda b, h, n: (row(b, h, n), h)),
            pl.BlockSpec((rows, kw), lambda b, h, n: (row(b, h, n), GDN_QK_HEADS // heads + h)),
            pl.BlockSpec((rows, vw), lambda b, h, n: (row(b, h, n), h)),
            pl.BlockSpec((rows, vw), lambda b, h, n: (row(b, h, n), h)),
            pl.BlockSpec((rows, LANES), lambda b, h, n: (row(b, h, n), 0)),
            pl.BlockSpec((rows, LANES), lambda b, h, n: (row(b, h, n), 0)),
            pl.BlockSpec((heads, cpb, GDN_REP * CHUNK), lambda b, h, n: (h, row(b, h, n), 0)),
            pl.BlockSpec((1, GDN_DV), lambda b, h, n: (0, 0)),
        ],
        out_specs=pl.BlockSpec((rows, vw), lambda b, h, n: (row(b, h, n), h)),
        out_shape=jax.ShapeDtypeStruct((t, GDN_VW), BF16),
        scratch_shapes=[pltpu.VMEM((heads, GDN_REP * GDN_DK, GDN_REP * GDN_DV), F32)],
        compiler_params=_params("arbitrary", "arbitrary", "arbitrary"),
        name="gdn",
    )(qk, qk, v, z, bg, gc, gc_rows, ng)


def _out_ln_body(a_ref, w_ref, x_ref, g_ref, b_ref, o_ref):
    mix = _dot(a_ref[...], w_ref[...])
    o_ref[...] = _layer_norm(ALPHA * x_ref[...] + mix, g_ref[...], b_ref[...])


def _out_ln(a, w, x, g, b, *, tm):
    t, d = x.shape
    kdim = a.shape[1]
    const = lambda shape: pl.BlockSpec(shape, lambda i: (0, 0), pipeline_mode=pl.Buffered(1))
    return pl.pallas_call(
        _out_ln_body,
        grid=(t // tm,),
        in_specs=[
            pl.BlockSpec((tm, kdim), lambda i: (i, 0)),
            const((kdim, d)),
            pl.BlockSpec((tm, d), lambda i: (i, 0)),
            const((1, d)), const((1, d)),
        ],
        out_specs=pl.BlockSpec((tm, d), lambda i: (i, 0)),
        out_shape=jax.ShapeDtypeStruct((t, d), F32),
        compiler_params=_params("arbitrary"),
        name="out_ln",
    )(a, w, x, g, b)


def _tiles(batch, seq):
    tm = min(512, seq)
    return dict(tm=tm, ffn_tm=min(1024, seq), gla_rows=min(256, seq), s5_rows=min(512, seq), gdn_rows=min(512, seq),
                gdn_heads=8)


def _ffn(x, wg, wu, wd, layer, g, b, tm, emit_bf16):
    return _ffn_ln(x, wg, wu, wd, layer, g[None], b[None], tm=tm, tf=256, emit_bf16=emit_bf16)


def _pad_cols(w, n):
    return jnp.pad(w, ((0, 0), (0, n - w.shape[1])))


def _gla_s5_layer(x, xb, p, i, g, b, batch, seq, tl):
    w_in = p['ab_w_in'][i]
    w_gate = _pad_cols(w_in[:, AB_A:AB_U], LANES).astype(BF16)
    proj = _proj(xb, w_in, tm=tl['tm'], tn=1024, col0=0, n=AB_A)
    u = _proj(xb, w_in[:, AB_U:], tm=tl['tm'], tn=1024)
    gate = _proj(xb, w_gate, tm=tl['tm'], tn=LANES)
    wlr = jnp.pad(p['gla_w_lr'][i], ((0, LANES - GLA_RANK), (0, 0))).astype(BF16)
    o_gla = _gla(proj, gate, wlr, p['gla_b_lr'][i][None], p['gla_norm_g'][i][None],
                 batch=batch, seq=seq, rows=tl['gla_rows'])
    bbd, cbd, pw = _s5_tables(p['s5_lam_re'][i], p['s5_lam_im'][i], p['s5_b_re'][i], p['s5_b_im'][i],
                              p['s5_c_re'][i], p['s5_c_im'][i], p['s5_log_step'][i])
    ys = _s5(u, bbd, cbd, pw, p['s5_d'][i][None], batch=batch, seq=seq, rows=tl['s5_rows'], u_col0=0)
    w_out = p['ab_w_out'][i].astype(BF16)
    return _mix0_out(o_gla, ys, p['s5_glu_w'][i].astype(BF16), p['s5_glu_b'][i][None],
                     w_out[:GLA_VW], w_out[GLA_VW:], x, g[None], b[None], tm=tl['tm'])


def _gdn_layer(x, xb, p, i, g, b, batch, seq, tl):
    t = batch * seq
    w_in = p['gdn_w_in'][i]
    conv_w = p['gdn_conv_w'][i]
    tm = tl['tm']
    qk = _conv_proj(xb, w_in, conv_w, tm=tm, tn=1024, seq=seq, col0=0, n=2 * GDN_KW,
                    l2_heads=True, n_scaled_tiles=GDN_KW // 1024, scale=GDN_DK ** -0.5)
    v = _conv_proj(xb, w_in, conv_w, tm=tm, tn=1024, seq=seq, col0=2 * GDN_KW, n=GDN_VW)
    z = _proj(xb, w_in, tm=tm, tn=1024, col0=GDN_Z, n=GDN_VW)
    w_gate = _pad_cols(w_in[:, GDN_B:GDN_IN], LANES).astype(BF16)
    lane_pad = lambda a: jnp.pad(a, (GDN_V_HEADS, LANES - 2 * GDN_V_HEADS))[None]
    bg, gc = _gate_proj(xb, w_gate, lane_pad(p['gdn_a_log'][i]), lane_pad(p['gdn_dt_bias'][i]), tm=tm)
    gc_rows = gc[:, GDN_V_HEADS:2 * GDN_V_HEADS].reshape(t // CHUNK, CHUNK, GDN_QK_HEADS, GDN_REP)
    gc_rows = gc_rows.transpose(2, 0, 3, 1).reshape(GDN_QK_HEADS, t // CHUNK, GDN_REP * CHUNK)
    o = _gdn(qk, v, z, bg, gc, gc_rows, p['gdn_norm_g'][i][None], batch=batch, seq=seq,
             rows=tl['gdn_rows'], heads=tl['gdn_heads'])
    return _out_ln(o, p['gdn_w_out'][i].astype(BF16), x, g[None], b[None], tm=min(256, tm))


def kernel(x, ffn_a_gate, ffn_a_up, ffn_a_down, ffn_b_gate, ffn_b_up, ffn_b_down, ln_g, ln_b, ab_w_in, gla_w_lr, gla_b_lr, gla_norm_g, s5_lam_re, s5_lam_im, s5_b_re, s5_b_im, s5_c_re, s5_c_im, s5_d, s5_log_step, s5_glu_w, s5_glu_b, ab_w_out, gdn_w_in, gdn_conv_w, gdn_a_log, gdn_dt_bias, gdn_norm_g, gdn_w_out):
    batch, seq, d = x.shape
    assert d == D_MODEL and seq % CHUNK == 0
    p = dict(ab_w_in=ab_w_in, gla_w_lr=gla_w_lr, gla_b_lr=gla_b_lr, gla_norm_g=gla_norm_g,
             s5_lam_re=s5_lam_re, s5_lam_im=s5_lam_im, s5_b_re=s5_b_re, s5_b_im=s5_b_im,
             s5_c_re=s5_c_re, s5_c_im=s5_c_im, s5_d=s5_d, s5_log_step=s5_log_step,
             s5_glu_w=s5_glu_w, s5_glu_b=s5_glu_b, ab_w_out=ab_w_out, gdn_w_in=gdn_w_in,
             gdn_conv_w=gdn_conv_w, gdn_a_log=gdn_a_log, gdn_dt_bias=gdn_dt_bias,
             gdn_norm_g=gdn_norm_g, gdn_w_out=gdn_w_out)
    tl = _tiles(batch, seq)
    h = x.reshape(batch * seq, d).astype(F32)
    for layer in range(DEPTH):
        h, hb = _ffn(h, ffn_a_gate, ffn_a_up, ffn_a_down, layer,
                     ln_g[layer, 0], ln_b[layer, 0], tl['ffn_tm'], True)
        i = layer // 2
        mixer = _gla_s5_layer if layer % 2 == 0 else _gdn_layer
        h = mixer(h, hb, p, i, ln_g[layer, 1], ln_b[layer, 1], batch, seq, tl)
        (h,) = _ffn(h, ffn_b_gate, ffn_b_up, ffn_b_down, layer,
                    ln_g[layer, 2], ln_b[layer, 2], tl['ffn_tm'], False)
    return h.reshape(batch, seq, d)
```

```python
import functools
import math

import jax
import jax.numpy as jnp
import numpy as np
from jax import lax
from jax.experimental import pallas as pl
from jax.experimental.pallas import tpu as pltpu

F32 = jnp.float32
BF16 = jnp.bfloat16

LANES = 128
SUBLANES = 8
VMEM_LIMIT_BYTES = 60 * 1024 * 1024
LN_ROWS = 64
CONV_SUB = 256

D_MODEL = 2048
DEPTH = 2
ALPHA = (2.0 * DEPTH) ** 0.25
MACARON = 0.5
LN_EPS = 1e-5
RMS_EPS = 1e-6
L2_EPS = 1e-6
D_FF = 5504

GLA_HEADS = 4
GLA_DK = 128
GLA_DV = 256
GLA_RANK = 16
GLA_TAU = 16.0
CHUNK = 64
GLA_KW = GLA_HEADS * GLA_DK
GLA_VW = GLA_HEADS * GLA_DV

S5_WIDTH = 1024
S5_GROUP = 16
S5_GROUPS = 64
S5_STATE = 64
S5_TILE_GROUPS = LANES // S5_GROUP
S5_TILES = S5_WIDTH // LANES
S5_TILE_STATES = S5_TILE_GROUPS * S5_STATE

AB_K = GLA_KW
AB_V = AB_K + GLA_KW
AB_G = AB_V + GLA_VW
AB_A = AB_G + GLA_VW
AB_U = AB_A + GLA_RANK
AB_IN = AB_U + S5_WIDTH

GDN_QK_HEADS = 16
GDN_V_HEADS = 32
GDN_DK = 128
GDN_DV = 128
GDN_KW = GDN_QK_HEADS * GDN_DK
GDN_VW = GDN_V_HEADS * GDN_DV
GDN_QKV = 2 * GDN_KW + GDN_VW
GDN_Z = GDN_QKV
GDN_B = GDN_Z + GDN_VW
GDN_A = GDN_B + GDN_V_HEADS
GDN_IN = GDN_A + GDN_V_HEADS


def _params(*sem):
    return pltpu.CompilerParams(dimension_semantics=sem, vmem_limit_bytes=VMEM_LIMIT_BYTES)


def _dot(a, b):
    return jnp.dot(a, b, preferred_element_type=F32)


def _dot_nt(a, b):
    return lax.dot_general(a, b, (((1,), (1,)), ((), ())), preferred_element_type=F32)


def _dot_tn(a, b):
    return lax.dot_general(a, b, (((0,), (0,)), ((), ())), preferred_element_type=F32)


def _layer_norm(y, g, b):
    mu = jnp.mean(y, axis=-1, keepdims=True)
    yc = y - mu
    var = jnp.mean(yc * yc, axis=-1, keepdims=True)
    return yc * lax.rsqrt(var + LN_EPS) * g + b


def _silu(v):
    return v * jax.nn.sigmoid(v)


def _softplus(v):
    return jnp.maximum(v, 0.0) + jnp.log1p(jnp.exp(-jnp.abs(v)))


def _split3(v):
    hi = v.astype(BF16)
    r1 = v - hi.astype(F32)
    mid = r1.astype(BF16)
    lo = (r1 - mid.astype(F32)).astype(BF16)
    return hi, mid, lo


def _tril_mask(n, strict=False):
    r = lax.broadcasted_iota(jnp.int32, (n, n), 0)
    c = lax.broadcasted_iota(jnp.int32, (n, n), 1)
    return (c < r) if strict else (c <= r)


def _chunk_cumsum(v, rows):
    r = lax.broadcasted_iota(jnp.int32, (rows, rows), 0)
    c = lax.broadcasted_iota(jnp.int32, (rows, rows), 1)
    tri = ((c <= r) & ((c // CHUNK) == (r // CHUNK))).astype(BF16)
    hi, mid, lo = _split3(v)
    return _dot(tri, hi) + _dot(tri, mid) + _dot(tri, lo)


def _ffn_body(x_ref, wg_ref, wu_ref, wd_ref, wgt_ref, wut_ref, wdt_ref, g_ref, b_ref, o_ref, *rest, nj):
    ob_ref, xb_ref = rest if len(rest) == 2 else (None, rest[0])
    j = pl.program_id(1)

    @pl.when(j == 0)
    def _():
        xb_ref[...] = x_ref[...].astype(BF16)
        o_ref[...] = jnp.zeros_like(o_ref)

    def down(wg, wu, wd):
        xb = xb_ref[...]
        h = (_silu(_dot(xb, wg.astype(BF16))) * _dot(xb, wu.astype(BF16))).astype(BF16)
        return _dot(h, wd.astype(BF16))

    @pl.when(j < nj - 1)
    def _():
        o_ref[...] += down(wg_ref[...], wu_ref[...], wd_ref[...])

    @pl.when(j == nj - 1)
    def _():
        o_ref[...] += down(wgt_ref[...], wut_ref[...], wdt_ref[...])
        g = g_ref[...]
        b = b_ref[...]

        def rows(r, carry):
            sl = pl.ds(pl.multiple_of(r * LN_ROWS, LN_ROWS), LN_ROWS)
            yn = _layer_norm(ALPHA * x_ref[sl, :] + MACARON * o_ref[sl, :], g, b)
            o_ref[sl, :] = yn
            if ob_ref is not None:
                ob_ref[sl, :] = yn.astype(BF16)
            return carry

        lax.fori_loop(0, o_ref.shape[0] // LN_ROWS, rows, 0)


def _ffn_ln(x, wg, wu, wd, layer, g, b, *, tm, tf, emit_bf16):
    t, d = x.shape
    f = wg.shape[2]
    n_full = f // tf
    f_tail = f - n_full * tf
    assert f_tail > 0 and f_tail % LANES == 0
    nj = n_full + 1
    last = n_full - 1
    const = lambda shape: pl.BlockSpec(shape, lambda i, j: (0, 0), pipeline_mode=pl.Buffered(1))
    tile = pl.BlockSpec((tm, d), lambda i, j: (i, 0))
    out_specs = [tile, tile] if emit_bf16 else [tile]
    out_shape = [jax.ShapeDtypeStruct((t, d), F32), jax.ShapeDtypeStruct((t, d), BF16)][:len(out_specs)]
    return pl.pallas_call(
        functools.partial(_ffn_body, nj=nj),
        grid=(t // tm, nj),
        in_specs=[
            pl.BlockSpec((tm, d), lambda i, j: (i, 0), pipeline_mode=pl.Buffered(1)),
            pl.BlockSpec((None, d, tf), lambda i, j: (layer, 0, jnp.minimum(j, last))),
            pl.BlockSpec((None, d, tf), lambda i, j: (layer, 0, jnp.minimum(j, last))),
            pl.BlockSpec((None, tf, d), lambda i, j: (layer, jnp.minimum(j, last), 0)),
            const((d, f_tail)), const((d, f_tail)), const((f_tail, d)),
            const((1, d)), const((1, d)),
        ],
        out_specs=out_specs,
        out_shape=out_shape,
        scratch_shapes=[pltpu.VMEM((tm, d), BF16)],
        compiler_params=_params("arbitrary", "arbitrary"),
        name="ffn_ln",
    )(x, wg, wu, wd, wg[layer, :, f - f_tail:], wu[layer, :, f - f_tail:], wd[layer, f - f_tail:], g, b)


def _proj_body(x_ref, w_ref, o_ref, wb_ref):
    @pl.when(pl.program_id(1) == 0)
    def _():
        wb_ref[...] = w_ref[...].astype(BF16)

    o_ref[...] = _dot(x_ref[...], wb_ref[...])


def _proj(xb, w, *, tm, tn, col0=0, n=None):
    t, k = xb.shape
    n = w.shape[1] if n is None else n
    assert col0 % tn == 0 and n % tn == 0
    j0 = col0 // tn
    return pl.pallas_call(
        _proj_body,
        grid=(n // tn, t // tm),
        in_specs=[
            pl.BlockSpec((tm, k), lambda j, i: (i, 0)),
            pl.BlockSpec((k, tn), lambda j, i: (0, j0 + j)),
        ],
        out_specs=pl.BlockSpec((tm, tn), lambda j, i: (i, j)),
        out_shape=jax.ShapeDtypeStruct((t, n), F32),
        scratch_shapes=[pltpu.VMEM((k, tn), BF16)],
        compiler_params=_params("arbitrary", "arbitrary"),
        name="proj",
    )(xb, w)


def _conv_proj_body(x_ref, w_ref, cw_ref, o_ref, wb_ref, stage_ref, *, tm, tn, tiles_per_seq,
                    l2_heads, n_scaled_tiles, scale):
    jt = pl.program_id(0)
    i = pl.program_id(1)

    @pl.when(i == 0)
    def _():
        wb_ref[...] = w_ref[...].astype(BF16)

    @pl.when(i % tiles_per_seq == 0)
    def _():
        stage_ref[0:SUBLANES, :] = jnp.zeros((SUBLANES, tn), F32)

    if l2_heads:
        fac = jnp.where(jt < n_scaled_tiles, jnp.float32(scale), jnp.float32(1.0))

    def post(v):
        v = _silu(v)
        if not l2_heads:
            return v
        out = []
        for h in range(CONV_SUB // LANES):
            vh = v[:, h * LANES:(h + 1) * LANES]
            ss = jnp.sum(vh * vh, axis=-1, keepdims=True)
            out.append(vh * lax.rsqrt(ss + L2_EPS) * fac)
        return jnp.concatenate(out, axis=1)

    subs = [slice(s * CONV_SUB, (s + 1) * CONV_SUB) for s in range(tn // CONV_SUB)]

    x = x_ref[...]
    raws = [_dot(x, wb_ref[:, cols]) for cols in subs]
    for cols, raw in zip(subs, raws):
        cw = cw_ref[:, cols]
        stage_ref[SUBLANES:SUBLANES + tm, cols] = raw
        acc = raw * cw[3:4]
        for d in (1, 2, 3):
            acc = acc + stage_ref[SUBLANES - d:SUBLANES - d + tm, cols] * cw[3 - d:4 - d]
        stage_ref[0:SUBLANES, cols] = raw[tm - SUBLANES:tm]
        o_ref[:, cols] = post(acc)


def _conv_proj(xb, w, cw, *, tm, tn, seq, col0, n, l2_heads=False, n_scaled_tiles=0, scale=1.0):
    t, k = xb.shape
    nt = t // tm
    assert col0 % tn == 0 and n % tn == 0
    j0 = col0 // tn
    body = functools.partial(_conv_proj_body, tm=tm, tn=tn, tiles_per_seq=seq // tm,
                             l2_heads=l2_heads, n_scaled_tiles=n_scaled_tiles, scale=scale)
    return pl.pallas_call(
        body,
        grid=(n // tn, nt),
        in_specs=[
            pl.BlockSpec((tm, k), lambda j, i: (i, 0)),
            pl.BlockSpec((k, tn), lambda j, i: (0, j0 + j)),
            pl.BlockSpec((4, tn), lambda j, i: (0, j0 + j)),
        ],
        out_specs=pl.BlockSpec((tm, tn), lambda j, i: (i, j)),
        out_shape=jax.ShapeDtypeStruct((t, n), F32),
        scratch_shapes=[pltpu.VMEM((k, tn), BF16),
                        pltpu.VMEM((SUBLANES + tm, tn), F32)],
        compiler_params=_params("arbitrary", "arbitrary"),
        name="conv_proj",
    )(xb, w, cw)


def _gate_proj_body(x_ref, w_ref, alog_ref, dtb_ref, bg_ref, gc_ref, *, tm):
    raw = _dot(x_ref[...], w_ref[...])
    lane = lax.broadcasted_iota(jnp.int32, raw.shape, 1)
    beta = jax.nn.sigmoid(raw)
    g = -jnp.exp(alog_ref[...]) * _softplus(raw + dtb_ref[...])
    g = jnp.where((lane >= GDN_V_HEADS) & (lane < 2 * GDN_V_HEADS), g, 0.0)
    bg_ref[...] = jnp.where(lane < GDN_V_HEADS, beta, g)
    gc_ref[...] = _chunk_cumsum(g, tm)


def _gate_proj(xb, w, alog, dtb, *, tm):
    t, k = xb.shape
    return pl.pallas_call(
        functools.partial(_gate_proj_body, tm=tm),
        grid=(t // tm,),
        in_specs=[
            pl.BlockSpec((tm, k), lambda i: (i, 0)),
            pl.BlockSpec((k, LANES), lambda i: (0, 0)),
            pl.BlockSpec((1, LANES), lambda i: (0, 0)),
            pl.BlockSpec((1, LANES), lambda i: (0, 0)),
        ],
        out_specs=[pl.BlockSpec((tm, LANES), lambda i: (i, 0))] * 2,
        out_shape=[jax.ShapeDtypeStruct((t, LANES), F32)] * 2,
        compiler_params=_params("arbitrary"),
        name="gate_proj",
    )(xb, w, alog, dtb)


def _gla_body(q_ref, k_ref, v_ref, go_ref, a_ref, wlr_ref, blr_ref, ng_ref, o_ref, st_ref, *, rows):
    @pl.when(pl.program_id(1) == 0)
    def _():
        st_ref[...] = jnp.zeros_like(st_ref)

    causal = _tril_mask(CHUNK)
    ng = ng_ref[...]
    heads = range(GLA_HEADS)
    chunks = range(rows // CHUNK)
    head = lambda a, h: a[:, h * GLA_DK:(h + 1) * GLA_DK]

    z = _dot(a_ref[...].astype(BF16), wlr_ref[...]) + blr_ref[...]
    b = _chunk_cumsum(-_softplus(-z) / GLA_TAU, rows)
    k = k_ref[...]
    q_dec = (q_ref[...] * (GLA_DK ** -0.5) * jnp.exp(b)).astype(BF16)
    k_inv = (k * jnp.exp(-b)).astype(BF16)
    rows_of = lambda a, c: a[c * CHUNK:(c + 1) * CHUNK]
    b_last = [rows_of(b, c)[CHUNK - 1:CHUNK] for c in chunks]
    k_end = [(rows_of(k, c) * jnp.exp(b_last[c] - rows_of(b, c))).astype(BF16) for c in chunks]
    scores = [[jnp.where(causal, _dot_nt(head(rows_of(q_dec, c), h), head(rows_of(k_inv, c), h)),
                         0.0).astype(BF16) for h in heads] for c in chunks]
    for c in chunks:
        sl = slice(c * CHUNK, (c + 1) * CHUNK)
        sts = [st_ref[h] for h in heads]
        vs = [v_ref[sl, h * GLA_DV:(h + 1) * GLA_DV].astype(BF16) for h in heads]
        qd = rows_of(q_dec, c)
        outs = [_dot(scores[c][h], vs[h]) + _dot_nt(head(qd, h), sts[h].astype(BF16)) for h in heads]
        decay = jnp.exp(b_last[c])
        new = [sts[h] * head(decay, h) + _dot_tn(vs[h], head(k_end[c], h)) for h in heads]
        for h in heads:
            st_ref[h] = new[h]
        normed = [o * lax.rsqrt(jnp.mean(o * o, axis=-1, keepdims=True) + RMS_EPS) * ng for o in outs]
        o_ref[sl, :] = (jnp.concatenate(normed, axis=1) * _silu(go_ref[sl, :])).astype(BF16)


def _gla(proj, gate, wlr, blr, ng, *, batch, seq, rows):
    t = batch * seq
    nblk = seq // rows
    row = lambda b, n: b * nblk + n
    return pl.pallas_call(
        functools.partial(_gla_body, rows=rows),
        grid=(batch, nblk),
        in_specs=[
            pl.BlockSpec((rows, GLA_KW), lambda b, n: (row(b, n), 0)),
            pl.BlockSpec((rows, GLA_KW), lambda b, n: (row(b, n), 1)),
            pl.BlockSpec((rows, GLA_VW), lambda b, n: (row(b, n), AB_V // GLA_VW)),
            pl.BlockSpec((rows, GLA_VW), lambda b, n: (row(b, n), AB_G // GLA_VW)),
            pl.BlockSpec((rows, LANES), lambda b, n: (row(b, n), 0)),
            pl.BlockSpec((LANES, GLA_KW), lambda b, n: (0, 0)),
            pl.BlockSpec((1, GLA_KW), lambda b, n: (0, 0)),
            pl.BlockSpec((1, GLA_DV), lambda b, n: (0, 0)),
        ],
        out_specs=pl.BlockSpec((rows, GLA_VW), lambda b, n: (row(b, n), 0)),
        out_shape=jax.ShapeDtypeStruct((t, GLA_VW), BF16),
        scratch_shapes=[pltpu.VMEM((GLA_HEADS, GLA_DV, GLA_DK), F32)],
        compiler_params=_params("arbitrary", "arbitrary"),
        name="gla",
    )(proj, proj, proj, proj, gate, wlr, blr, ng)


def _s5_body(u_ref, bbd_ref, cbd_ref, pw_ref, d_ref, y_ref, xs_ref, cr_ref, ci_ref, *, rows):
    @pl.when(pl.program_id(2) == 0)
    def _():
        cr_ref[...] = jnp.zeros_like(cr_ref)
        ci_ref[...] = jnp.zeros_like(ci_ref)

    ns = S5_TILE_STATES
    u = u_ref[...]
    bu = _dot(u.astype(BF16), bbd_ref[0])
    levels = [(pw_ref[0, 2 * l], pw_ref[0, 2 * l + 1], 1 << l) for l in range(3)]
    qr, qi = pw_ref[0, 6], pw_ref[0, 7]
    cr = cr_ref[...]
    ci = ci_ref[...]
    for blk in range(rows // SUBLANES):
        sl = slice(blk * SUBLANES, (blk + 1) * SUBLANES)
        xr = bu[sl, :ns]
        xi = bu[sl, ns:]
        for pr, pi, s in levels:
            sr = pltpu.roll(xr, s, 0)
            si = pltpu.roll(xi, s, 0)
            xr, xi = xr + (pr * sr - pi * si), xi + (pr * si + pi * sr)
        xr, xi = xr + (qr * cr - qi * ci), xi + (qr * ci + qi * cr)
        cr = jnp.broadcast_to(xr[SUBLANES - 1:SUBLANES], (SUBLANES, ns))
        ci = jnp.broadcast_to(xi[SUBLANES - 1:SUBLANES], (SUBLANES, ns))
        xs_ref[sl, :ns] = xr
        xs_ref[sl, ns:] = xi
    cr_ref[...] = cr
    ci_ref[...] = ci
    y = _dot(xs_ref[...].astype(BF16), cbd_ref[0]) + d_ref[...] * u
    y3 = y * y * y
    y_ref[...] = y * (0.5 * (1.0 + jnp.tanh(np.float32(math.sqrt(2.0 / math.pi)) * (y + 0.044715 * y3))))


def _s5(proj, bbd, cbd, pw, d, *, batch, seq, rows, u_col0):
    t = batch * seq
    nblk = seq // rows
    ns = S5_TILE_STATES
    return pl.pallas_call(
        functools.partial(_s5_body, rows=rows),
        grid=(batch, S5_TILES, nblk),
        in_specs=[
            pl.BlockSpec((rows, LANES), lambda b, j, n: (b * nblk + n, u_col0 + j)),
            pl.BlockSpec((1, LANES, 2 * ns), lambda b, j, n: (j, 0, 0)),
            pl.BlockSpec((1, 2 * ns, LANES), lambda b, j, n: (j, 0, 0)),
            pl.BlockSpec((1, 8, SUBLANES, ns), lambda b, j, n: (j, 0, 0, 0)),
            pl.BlockSpec((1, LANES), lambda b, j, n: (0, j)),
        ],
        out_specs=pl.BlockSpec((rows, LANES), lambda b, j, n: (b * nblk + n, j)),
        out_shape=jax.ShapeDtypeStruct((t, S5_WIDTH), F32),
        scratch_shapes=[pltpu.VMEM((rows, 2 * ns), F32),
                        pltpu.VMEM((SUBLANES, ns), F32),
                        pltpu.VMEM((SUBLANES, ns), F32)],
        compiler_params=_params("arbitrary", "arbitrary", "arbitrary"),
        name="s5",
    )(proj, bbd, cbd, pw, d)


def _s5_tables(lam_re, lam_im, b_re, b_im, c_re, c_im, log_step):
    step = jnp.exp(log_step)[:, None]
    mag = jnp.exp(lam_re * step)
    a_re = mag * jnp.cos(lam_im * step)
    a_im = mag * jnp.sin(lam_im * step)
    inv_den = 1.0 / (lam_re * lam_re + lam_im * lam_im)
    f_re = ((a_re - 1.0) * lam_re + a_im * lam_im) * inv_den
    f_im = (a_im * lam_re - (a_re - 1.0) * lam_im) * inv_den
    bb_re = f_re[..., None] * b_re - f_im[..., None] * b_im
    bb_im = f_re[..., None] * b_im + f_im[..., None] * b_re
    tg, nt = S5_TILE_GROUPS, S5_TILES
    eye = jnp.eye(tg, dtype=F32)

    def in_table(bb):
        bb = bb.reshape(nt, tg, S5_STATE, S5_GROUP)
        return jnp.einsum('jgph,gk->jghkp', bb, eye).reshape(nt, LANES, S5_TILE_STATES)

    def out_table(cc):
        cc = cc.reshape(nt, tg, S5_GROUP, S5_STATE)
        return jnp.einsum('jghp,gk->jgpkh', cc, eye).reshape(nt, S5_TILE_STATES, LANES)

    bbd = jnp.concatenate([in_table(bb_re), in_table(bb_im)], axis=2).astype(BF16)
    cbd = jnp.concatenate([out_table(c_re), -out_table(c_im)], axis=1).astype(BF16)

    def cmul(x, y):
        return x[0] * y[0] - x[1] * y[1], x[0] * y[1] + x[1] * y[0]

    a1 = (a_re.reshape(nt, 1, S5_TILE_STATES), a_im.reshape(nt, 1, S5_TILE_STATES))
    powers = [a1]
    for _ in range(SUBLANES - 1):
        powers.append(cmul(powers[-1], a1))
    row = jnp.arange(SUBLANES)[None, :, None]
    tabs = []
    for s in (1, 2, 4):
        pr, pi = powers[s - 1]
        tabs.append(jnp.where(row >= s, pr, 0.0))
        tabs.append(jnp.where(row >= s, pi, 0.0))
    tabs.append(jnp.concatenate([p[0] for p in powers], axis=1))
    tabs.append(jnp.concatenate([p[1] for p in powers], axis=1))
    pw = jnp.stack([jnp.broadcast_to(tb, (nt, SUBLANES, S5_TILE_STATES)) for tb in tabs], axis=1)
    return bbd, cbd, pw.astype(F32)


def _mix0_out_body(og_ref, ys_ref, gw_ref, gb_ref, w1_ref, w2_ref, x_ref, g_ref, b_ref, o_ref):
    ys = ys_ref[...]
    gate = jax.nn.sigmoid(_dot(ys.astype(BF16), gw_ref[...]) + gb_ref[...])
    os5 = (ys * gate).astype(BF16)
    mix = _dot(og_ref[...], w1_ref[...]) + _dot(os5, w2_ref[...])
    o_ref[...] = _layer_norm(ALPHA * x_ref[...] + mix, g_ref[...], b_ref[...])


def _mix0_out(og, ys, gw, gb, w1, w2, x, g, b, *, tm):
    t, d = x.shape
    const = lambda shape: pl.BlockSpec(shape, lambda i: (0, 0))
    return pl.pallas_call(
        _mix0_out_body,
        grid=(t // tm,),
        in_specs=[
            pl.BlockSpec((tm, GLA_VW), lambda i: (i, 0)),
            pl.BlockSpec((tm, S5_WIDTH), lambda i: (i, 0)),
            const((S5_WIDTH, S5_WIDTH)), const((1, S5_WIDTH)),
            const((GLA_VW, d)), const((S5_WIDTH, d)),
            pl.BlockSpec((tm, d), lambda i: (i, 0)),
            const((1, d)), const((1, d)),
        ],
        out_specs=pl.BlockSpec((tm, d), lambda i: (i, 0)),
        out_shape=jax.ShapeDtypeStruct((t, d), F32),
        compiler_params=_params("arbitrary"),
        name="mix0_out",
    )(og, ys, gw, gb, w1, w2, x, g, b)


GDN_REP = GDN_V_HEADS // GDN_QK_HEADS


def _block_diag2(m, second):
    zero = jnp.zeros_like(m)
    return jnp.concatenate([jnp.where(second, zero, m), jnp.where(second, m, zero)], axis=0)


def _gdn_body(q_ref, k_ref, v_ref, z_ref, bg_ref, gc_ref, gr_ref, ng_ref, o_ref, st_ref, *, rows, heads):
    hg = pl.program_id(1)
    c2 = 2 * CHUNK
    vw = GDN_REP * GDN_DV

    @pl.when(pl.program_id(2) == 0)
    def _():
        st_ref[...] = jnp.zeros_like(st_ref)

    lane2 = lax.broadcasted_iota(jnp.int32, (CHUNK, c2), 1)
    row2 = lax.broadcasted_iota(jnp.int32, (CHUNK, c2), 0)
    second = lane2 >= CHUNK
    pos = jnp.where(second, lane2 - CHUNK, lane2)
    incl2 = pos <= row2
    strict2 = pos < row2
    eye2 = pos == row2
    lane_bg = lax.broadcasted_iota(jnp.int32, (CHUNK, LANES), 1)
    second_v = lax.broadcasted_iota(jnp.int32, (CHUNK, vw), 1) >= GDN_DV
    st_r = lax.broadcasted_iota(jnp.int32, (GDN_REP * GDN_DK, vw), 0) >= GDN_DK
    st_c = lax.broadcasted_iota(jnp.int32, (GDN_REP * GDN_DK, vw), 1) >= GDN_DV
    same_head = st_r == st_c
    ng = ng_ref[...]
    zero_rhs = jnp.zeros((CHUNK, GDN_DV + GDN_DK), BF16)

    def chunk(c, carry):
        r0 = pl.multiple_of(c * CHUNK, CHUNK)
        bg = bg_ref[pl.ds(r0, CHUNK), :]
        gcb = gc_ref[pl.ds(r0, CHUNK), :]

        def column(arr, idx):
            return jnp.sum(jnp.where(lane_bg == idx, arr, 0.0), axis=-1, keepdims=True)

        hs = range(heads)
        ks, qs, prods, cols, decays, xs, ps, pbds = [], [], [], [], [], [], [], []
        for g in hs:
            k = k_ref[pl.ds(r0, CHUNK), g * GDN_DK:(g + 1) * GDN_DK]
            q = q_ref[pl.ds(r0, CHUNK), g * GDN_DK:(g + 1) * GDN_DK]
            kb = k.astype(BF16)
            ks.append(k)
            qs.append(q)
            prods.append(_dot_nt(jnp.concatenate([kb, q.astype(BF16)], axis=0),
                                 jnp.concatenate([kb, kb], axis=0)))
        for g in hs:
            hq = hg * heads + g
            b0 = column(bg, GDN_REP * hq)
            b1 = column(bg, GDN_REP * hq + 1)
            g0 = column(gcb, GDN_V_HEADS + GDN_REP * hq)
            g1 = column(gcb, GDN_V_HEADS + GDN_REP * hq + 1)
            cols.append((b0, b1, g0, g1, g0[CHUNK - 1:CHUNK, :], g1[CHUNK - 1:CHUNK, :]))
            gc_row = jnp.where(second[0:1], gr_ref[c, pl.ds(GDN_REP * hq + 1, 1), :],
                               gr_ref[c, pl.ds(GDN_REP * hq, 1), :])
            decay2 = jnp.exp(jnp.where(incl2, jnp.where(second, g1, g0) - gc_row, -jnp.inf))
            a2 = jnp.where(strict2, prods[g][:CHUNK] * jnp.where(second, b1, b0) * decay2, 0.0)
            decays.append(decay2)
            xs.append(jnp.where(eye2, 1.0, 0.0) - a2)
            ps.append(a2.astype(BF16))
            pbds.append(_block_diag2(ps[g], second))
        power = 1
        while 2 * power < CHUNK:
            for g in hs:
                ps[g] = _dot(ps[g], pbds[g]).astype(BF16)
                pbds[g] = _block_diag2(ps[g], second)
            for g in hs:
                xs[g] = xs[g] + _dot(xs[g].astype(BF16), pbds[g])
            power *= 2
        sols, exps = [], []
        for g in hs:
            b0, b1, g0, g1, gl0, gl1 = cols[g]
            e0 = jnp.exp(g0)
            e1 = jnp.exp(g1)
            exps.append((e0, e1))
            v2 = v_ref[pl.ds(r0, CHUNK), g * vw:(g + 1) * vw]
            rhs0 = jnp.concatenate([v2[:, :GDN_DV] * b0, (ks[g] * b0) * e0], axis=1).astype(BF16)
            rhs1 = jnp.concatenate([v2[:, GDN_DV:] * b1, (ks[g] * b1) * e1], axis=1).astype(BF16)
            rhs_bd = jnp.concatenate([jnp.concatenate([rhs0, zero_rhs], axis=1),
                                      jnp.concatenate([zero_rhs, rhs1], axis=1)], axis=0)
            sols.append(_dot(xs[g].astype(BF16), rhs_bd))
        sts, wss = [], []
        for g in hs:
            sol = sols[g]
            e0, e1 = exps[g]
            w2 = jnp.concatenate([sol[:, GDN_DV:vw], sol[:, vw + GDN_DV:]], axis=1)
            qd2 = jnp.concatenate([qs[g] * e0, qs[g] * e1], axis=1)
            sts.append(st_ref[g])
            wss.append(_dot(jnp.concatenate([w2, qd2], axis=0).astype(BF16), sts[g].astype(BF16)))
        new_states, results = [], []
        for g in hs:
            b0, b1, g0, g1, gl0, gl1 = cols[g]
            sol = sols[g]
            u2 = jnp.concatenate([sol[:, :GDN_DV], sol[:, vw:vw + GDN_DV]], axis=1)
            v_new = (u2 - wss[g][:CHUNK]).astype(BF16)
            attn2 = (prods[g][CHUNK:] * decays[g]).astype(BF16)
            o2 = wss[g][CHUNK:] + _dot(attn2, _block_diag2(v_new, second_v))
            k_end2 = jnp.concatenate([ks[g] * jnp.exp(gl0 - g0), ks[g] * jnp.exp(gl1 - g1)],
                                     axis=1).astype(BF16)
            upd = _dot_tn(k_end2, v_new)
            new_states.append(jnp.where(st_r, jnp.exp(gl1), jnp.exp(gl0)) * sts[g]
                              + jnp.where(same_head, upd, 0.0))
            outs = []
            for hh in range(GDN_REP):
                o = o2[:, hh * GDN_DV:(hh + 1) * GDN_DV]
                outs.append(o * lax.rsqrt(jnp.mean(o * o, axis=-1, keepdims=True) + RMS_EPS) * ng)
            z2 = z_ref[pl.ds(r0, CHUNK), g * vw:(g + 1) * vw]
            results.append((jnp.concatenate(outs, axis=1) * _silu(z2)).astype(BF16))
        for g in hs:
            st_ref[g] = new_states[g]
            o_ref[pl.ds(r0, CHUNK), g * vw:(g + 1) * vw] = results[g]
        return carry

    lax.fori_loop(0, rows // CHUNK, chunk, 0)


def _gdn(qk, v, z, bg, gc, gc_rows, ng, *, batch, seq, rows, heads):
    t = batch * seq
    nblk = seq // rows
    cpb = rows // CHUNK
    row = lambda b, h, n: b * nblk + n
    vw = heads * GDN_REP * GDN_DV
    kw = heads * GDN_DK
    return pl.pallas_call(
        functools.partial(_gdn_body, rows=rows, heads=heads),
        grid=(batch, GDN_QK_HEADS // heads, nblk),
        in_specs=[
            pl.BlockSpec((rows, kw), lambda b, h, n: (row(b, h, n), h)),
            pl.BlockSpec((rows, kw), lambda b, h, n: (row(b, h, n), GDN_QK_HEADS // heads + h)),
            pl.BlockSpec((rows, vw), lambda b, h, n: (row(b, h, n), h)),
            pl.BlockSpec((rows, vw), lambda b, h, n: (row(b, h, n), h)),
            pl.BlockSpec((rows, LANES), lambda b, h, n: (row(b, h, n), 0)),
            pl.BlockSpec((rows, LANES), lambda b, h, n: (row(b, h, n), 0)),
            pl.BlockSpec((cpb, GDN_V_HEADS, GDN_REP * CHUNK), lambda b, h, n: (row(b, h, n), 0, 0)),
            pl.BlockSpec((1, GDN_DV), lambda b, h, n: (0, 0)),
        ],
        out_specs=pl.BlockSpec((rows, vw), lambda b, h, n: (row(b, h, n), h)),
        out_shape=jax.ShapeDtypeStruct((t, GDN_VW), BF16),
        scratch_shapes=[pltpu.VMEM((heads, GDN_REP * GDN_DK, GDN_REP * GDN_DV), F32)],
        compiler_params=_params("arbitrary", "arbitrary", "arbitrary"),
        name="gdn",
    )(qk, qk, v, z, bg, gc, gc_rows, ng)


def _out_ln_body(a_ref, w_ref, x_ref, g_ref, b_ref, o_ref):
    mix = _dot(a_ref[...], w_ref[...])
    o_ref[...] = _layer_norm(ALPHA * x_ref[...] + mix, g_ref[...], b_ref[...])


def _out_ln(a, w, x, g, b, *, tm):
    t, d = x.shape
    kdim = a.shape[1]
    const = lambda shape: pl.BlockSpec(shape, lambda i: (0, 0), pipeline_mode=pl.Buffered(1))
    return pl.pallas_call(
        _out_ln_body,
        grid=(t // tm,),
        in_specs=[
            pl.BlockSpec((tm, kdim), lambda i: (i, 0)),
            const((kdim, d)),
            pl.BlockSpec((tm, d), lambda i: (i, 0)),
            const((1, d)), const((1, d)),
        ],
        out_specs=pl.BlockSpec((tm, d), lambda i: (i, 0)),
        out_shape=jax.ShapeDtypeStruct((t, d), F32),
        compiler_params=_params("arbitrary"),
        name="out_ln",
    )(a, w, x, g, b)


def _tiles(batch, seq):
    tm = min(512, seq)
    return dict(tm=tm, ffn_tm=min(1024, seq), gla_rows=min(256, seq), s5_rows=min(512, seq), gdn_rows=min(256, seq),
                gdn_heads=16)


def _ffn(x, wg, wu, wd, layer, g, b, tm, emit_bf16):
    return _ffn_ln(x, wg, wu, wd, layer, g[None], b[None], tm=tm, tf=256, emit_bf16=emit_bf16)


def _pad_cols(w, n):
    return jnp.pad(w, ((0, 0), (0, n - w.shape[1])))


def _gla_s5_layer(x, xb, p, i, g, b, batch, seq, tl):
    w_in = p['ab_w_in'][i]
    w_gate = _pad_cols(w_in[:, AB_A:AB_U], LANES).astype(BF16)
    proj = _proj(xb, w_in, tm=tl['tm'], tn=1024, col0=0, n=AB_A)
    u = _proj(xb, w_in[:, AB_U:], tm=tl['tm'], tn=1024)
    gate = _proj(xb, w_gate, tm=tl['tm'], tn=LANES)
    wlr = jnp.pad(p['gla_w_lr'][i], ((0, LANES - GLA_RANK), (0, 0))).astype(BF16)
    o_gla = _gla(proj, gate, wlr, p['gla_b_lr'][i][None], p['gla_norm_g'][i][None],
                 batch=batch, seq=seq, rows=tl['gla_rows'])
    bbd, cbd, pw = _s5_tables(p['s5_lam_re'][i], p['s5_lam_im'][i], p['s5_b_re'][i], p['s5_b_im'][i],
                              p['s5_c_re'][i], p['s5_c_im'][i], p['s5_log_step'][i])
    ys = _s5(u, bbd, cbd, pw, p['s5_d'][i][None], batch=batch, seq=seq, rows=tl['s5_rows'], u_col0=0)
    w_out = p['ab_w_out'][i].astype(BF16)
    return _mix0_out(o_gla, ys, p['s5_glu_w'][i].astype(BF16), p['s5_glu_b'][i][None],
                     w_out[:GLA_VW], w_out[GLA_VW:], x, g[None], b[None], tm=tl['tm'])


def _gdn_layer(x, xb, p, i, g, b, batch, seq, tl):
    t = batch * seq
    w_in = p['gdn_w_in'][i]
    conv_w = p['gdn_conv_w'][i]
    tm = tl['tm']
    qk = _conv_proj(xb, w_in, conv_w, tm=tm, tn=1024, seq=seq, col0=0, n=2 * GDN_KW,
                    l2_heads=True, n_scaled_tiles=GDN_KW // 1024, scale=GDN_DK ** -0.5)
    v = _conv_proj(xb, w_in, conv_w, tm=tm, tn=1024, seq=seq, col0=2 * GDN_KW, n=GDN_VW)
    z = _proj(xb, w_in, tm=tm, tn=1024, col0=GDN_Z, n=GDN_VW)
    w_gate = _pad_cols(w_in[:, GDN_B:GDN_IN], LANES).astype(BF16)
    lane_pad = lambda a: jnp.pad(a, (GDN_V_HEADS, LANES - 2 * GDN_V_HEADS))[None]
    bg, gc = _gate_proj(xb, w_gate, lane_pad(p['gdn_a_log'][i]), lane_pad(p['gdn_dt_bias'][i]), tm=tm)
    gc_rows = gc[:, GDN_V_HEADS:2 * GDN_V_HEADS].reshape(t // CHUNK, CHUNK, GDN_V_HEADS).transpose(0, 2, 1)
    gc_rows = jnp.concatenate([gc_rows] * GDN_REP, axis=-1)
    o = _gdn(qk, v, z, bg, gc, gc_rows, p['gdn_norm_g'][i][None], batch=batch, seq=seq,
             rows=tl['gdn_rows'], heads=tl['gdn_heads'])
    return _out_ln(o, p['gdn_w_out'][i].astype(BF16), x, g[None], b[None], tm=min(256, tm))


def kernel(x, ffn_a_gate, ffn_a_up, ffn_a_down, ffn_b_gate, ffn_b_up, ffn_b_down, ln_g, ln_b, ab_w_in, gla_w_lr, gla_b_lr, gla_norm_g, s5_lam_re, s5_lam_im, s5_b_re, s5_b_im, s5_c_re, s5_c_im, s5_d, s5_log_step, s5_glu_w, s5_glu_b, ab_w_out, gdn_w_in, gdn_conv_w, gdn_a_log, gdn_dt_bias, gdn_norm_g, gdn_w_out):
    batch, seq, d = x.shape
    assert d == D_MODEL and seq % CHUNK == 0
    p = dict(ab_w_in=ab_w_in, gla_w_lr=gla_w_lr, gla_b_lr=gla_b_lr, gla_norm_g=gla_norm_g,
             s5_lam_re=s5_lam_re, s5_lam_im=s5_lam_im, s5_b_re=s5_b_re, s5_b_im=s5_b_im,
             s5_c_re=s5_c_re, s5_c_im=s5_c_im, s5_d=s5_d, s5_log_step=s5_log_step,
             s5_glu_w=s5_glu_w, s5_glu_b=s5_glu_b, ab_w_out=ab_w_out, gdn_w_in=gdn_w_in,
             gdn_conv_w=gdn_conv_w, gdn_a_log=gdn_a_log, gdn_dt_bias=gdn_dt_bias,
             gdn_norm_g=gdn_norm_g, gdn_w_out=gdn_w_out)
    tl = _tiles(batch, seq)
    h = x.reshape(batch * seq, d).astype(F32)
    for layer in range(DEPTH):
        h, hb = _ffn(h, ffn_a_gate, ffn_a_up, ffn_a_down, layer,
                     ln_g[layer, 0], ln_b[layer, 0], tl['ffn_tm'], True)
        i = layer // 2
        mixer = _gla_s5_layer if layer % 2 == 0 else _gdn_layer
        h = mixer(h, hb, p, i, ln_g[layer, 1], ln_b[layer, 1], batch, seq, tl)
        (h,) = _ffn(h, ffn_b_gate, ffn_b_up, ffn_b_down, layer,
                    ln_g[layer, 2], ln_b[layer, 2], tl['ffn_tm'], False)
    return h.reshape(batch, seq, d)
```

```python
import functools
import math

import jax
import jax.numpy as jnp
import numpy as np
from jax import lax
from jax.experimental import pallas as pl
from jax.experimental.pallas import tpu as pltpu

F32 = jnp.float32
BF16 = jnp.bfloat16

LANES = 128
SUBLANES = 8
VMEM_LIMIT_BYTES = 60 * 1024 * 1024
LN_ROWS = 64
CONV_SUB = 256

D_MODEL = 2048
DEPTH = 2
ALPHA = (2.0 * DEPTH) ** 0.25
MACARON = 0.5
LN_EPS = 1e-5
RMS_EPS = 1e-6
L2_EPS = 1e-6
D_FF = 5504

GLA_HEADS = 4
GLA_DK = 128
GLA_DV = 256
GLA_RANK = 16
GLA_TAU = 16.0
CHUNK = 64
GLA_KW = GLA_HEADS * GLA_DK
GLA_VW = GLA_HEADS * GLA_DV

S5_WIDTH = 1024
S5_GROUP = 16
S5_GROUPS = 64
S5_STATE = 64
S5_TILE_GROUPS = LANES // S5_GROUP
S5_TILES = S5_WIDTH // LANES
S5_TILE_STATES = S5_TILE_GROUPS * S5_STATE

AB_K = GLA_KW
AB_V = AB_K + GLA_KW
AB_G = AB_V + GLA_VW
AB_A = AB_G + GLA_VW
AB_U = AB_A + GLA_RANK
AB_IN = AB_U + S5_WIDTH

GDN_QK_HEADS = 16
GDN_V_HEADS = 32
GDN_DK = 128
GDN_DV = 128
GDN_KW = GDN_QK_HEADS * GDN_DK
GDN_VW = GDN_V_HEADS * GDN_DV
GDN_QKV = 2 * GDN_KW + GDN_VW
GDN_Z = GDN_QKV
GDN_B = GDN_Z + GDN_VW
GDN_A = GDN_B + GDN_V_HEADS
GDN_IN = GDN_A + GDN_V_HEADS


def _params(*sem):
    return pltpu.CompilerParams(dimension_semantics=sem, vmem_limit_bytes=VMEM_LIMIT_BYTES)


def _dot(a, b):
    return jnp.dot(a, b, preferred_element_type=F32)


def _dot_nt(a, b):
    return lax.dot_general(a, b, (((1,), (1,)), ((), ())), preferred_element_type=F32)


def _dot_tn(a, b):
    return lax.dot_general(a, b, (((0,), (0,)), ((), ())), preferred_element_type=F32)


def _layer_norm(y, g, b):
    mu = jnp.mean(y, axis=-1, keepdims=True)
    yc = y - mu
    var = jnp.mean(yc * yc, axis=-1, keepdims=True)
    return yc * lax.rsqrt(var + LN_EPS) * g + b


def _silu(v):
    return v * jax.nn.sigmoid(v)


def _softplus(v):
    return jnp.maximum(v, 0.0) + jnp.log1p(jnp.exp(-jnp.abs(v)))


def _split3(v):
    hi = v.astype(BF16)
    r1 = v - hi.astype(F32)
    mid = r1.astype(BF16)
    lo = (r1 - mid.astype(F32)).astype(BF16)
    return hi, mid, lo


def _tril_mask(n, strict=False):
    r = lax.broadcasted_iota(jnp.int32, (n, n), 0)
    c = lax.broadcasted_iota(jnp.int32, (n, n), 1)
    return (c < r) if strict else (c <= r)


def _chunk_cumsum(v, rows):
    r = lax.broadcasted_iota(jnp.int32, (rows, rows), 0)
    c = lax.broadcasted_iota(jnp.int32, (rows, rows), 1)
    tri = ((c <= r) & ((c // CHUNK) == (r // CHUNK))).astype(BF16)
    hi, mid, lo = _split3(v)
    return _dot(tri, hi) + _dot(tri, mid) + _dot(tri, lo)


def _ffn_body(x_ref, wg_ref, wu_ref, wd_ref, wgt_ref, wut_ref, wdt_ref, g_ref, b_ref, o_ref, *rest, nj):
    ob_ref, xb_ref = rest if len(rest) == 2 else (None, rest[0])
    j = pl.program_id(1)

    @pl.when(j == 0)
    def _():
        xb_ref[...] = x_ref[...].astype(BF16)
        o_ref[...] = jnp.zeros_like(o_ref)

    def down(wg, wu, wd):
        xb = xb_ref[...]
        h = (_silu(_dot(xb, wg.astype(BF16))) * _dot(xb, wu.astype(BF16))).astype(BF16)
        return _dot(h, wd.astype(BF16))

    @pl.when(j < nj - 1)
    def _():
        o_ref[...] += down(wg_ref[...], wu_ref[...], wd_ref[...])

    @pl.when(j == nj - 1)
    def _():
        o_ref[...] += down(wgt_ref[...], wut_ref[...], wdt_ref[...])
        g = g_ref[...]
        b = b_ref[...]

        def rows(r, carry):
            sl = pl.ds(pl.multiple_of(r * LN_ROWS, LN_ROWS), LN_ROWS)
            yn = _layer_norm(ALPHA * x_ref[sl, :] + MACARON * o_ref[sl, :], g, b)
            o_ref[sl, :] = yn
            if ob_ref is not None:
                ob_ref[sl, :] = yn.astype(BF16)
            return carry

        lax.fori_loop(0, o_ref.shape[0] // LN_ROWS, rows, 0)


def _ffn_ln(x, wg, wu, wd, layer, g, b, *, tm, tf, emit_bf16):
    t, d = x.shape
    f = wg.shape[2]
    n_full = f // tf
    f_tail = f - n_full * tf
    assert f_tail > 0 and f_tail % LANES == 0
    nj = n_full + 1
    last = n_full - 1
    const = lambda shape: pl.BlockSpec(shape, lambda i, j: (0, 0), pipeline_mode=pl.Buffered(1))
    tile = pl.BlockSpec((tm, d), lambda i, j: (i, 0))
    out_specs = [tile, tile] if emit_bf16 else [tile]
    out_shape = [jax.ShapeDtypeStruct((t, d), F32), jax.ShapeDtypeStruct((t, d), BF16)][:len(out_specs)]
    return pl.pallas_call(
        functools.partial(_ffn_body, nj=nj),
        grid=(t // tm, nj),
        in_specs=[
            pl.BlockSpec((tm, d), lambda i, j: (i, 0), pipeline_mode=pl.Buffered(1)),
            pl.BlockSpec((None, d, tf), lambda i, j: (layer, 0, jnp.minimum(j, last))),
            pl.BlockSpec((None, d, tf), lambda i, j: (layer, 0, jnp.minimum(j, last))),
            pl.BlockSpec((None, tf, d), lambda i, j: (layer, jnp.minimum(j, last), 0)),
            const((d, f_tail)), const((d, f_tail)), const((f_tail, d)),
            const((1, d)), const((1, d)),
        ],
        out_specs=out_specs,
        out_shape=out_shape,
        scratch_shapes=[pltpu.VMEM((tm, d), BF16)],
        compiler_params=_params("arbitrary", "arbitrary"),
        name="ffn_ln",
    )(x, wg, wu, wd, wg[layer, :, f - f_tail:], wu[layer, :, f - f_tail:], wd[layer, f - f_tail:], g, b)


def _proj_body(x_ref, w_ref, o_ref, wb_ref):
    @pl.when(pl.program_id(1) == 0)
    def _():
        wb_ref[...] = w_ref[...].astype(BF16)

    o_ref[...] = _dot(x_ref[...], wb_ref[...])


def _proj(xb, w, *, tm, tn, col0=0, n=None):
    t, k = xb.shape
    n = w.shape[1] if n is None else n
    assert col0 % tn == 0 and n % tn == 0
    j0 = col0 // tn
    return pl.pallas_call(
        _proj_body,
        grid=(n // tn, t // tm),
        in_specs=[
            pl.BlockSpec((tm, k), lambda j, i: (i, 0)),
            pl.BlockSpec((k, tn), lambda j, i: (0, j0 + j)),
        ],
        out_specs=pl.BlockSpec((tm, tn), lambda j, i: (i, j)),
        out_shape=jax.ShapeDtypeStruct((t, n), F32),
        scratch_shapes=[pltpu.VMEM((k, tn), BF16)],
        compiler_params=_params("arbitrary", "arbitrary"),
        name="proj",
    )(xb, w)


def _conv_proj_body(x_ref, w_ref, cw_ref, o_ref, wb_ref, stage_ref, *, tm, tn, tiles_per_seq,
                    l2_heads, n_scaled_tiles, scale):
    jt = pl.program_id(0)
    i = pl.program_id(1)

    @pl.when(i == 0)
    def _():
        wb_ref[...] = w_ref[...].astype(BF16)

    @pl.when(i % tiles_per_seq == 0)
    def _():
        stage_ref[0:SUBLANES, :] = jnp.zeros((SUBLANES, tn), F32)

    if l2_heads:
        fac = jnp.where(jt < n_scaled_tiles, jnp.float32(scale), jnp.float32(1.0))

    def post(v):
        v = _silu(v)
        if not l2_heads:
            return v
        out = []
        for h in range(CONV_SUB // LANES):
            vh = v[:, h * LANES:(h + 1) * LANES]
            ss = jnp.sum(vh * vh, axis=-1, keepdims=True)
            out.append(vh * lax.rsqrt(ss + L2_EPS) * fac)
        return jnp.concatenate(out, axis=1)

    subs = [slice(s * CONV_SUB, (s + 1) * CONV_SUB) for s in range(tn // CONV_SUB)]

    x = x_ref[...]
    raws = [_dot(x, wb_ref[:, cols]) for cols in subs]
    for cols, raw in zip(subs, raws):
        cw = cw_ref[:, cols]
        stage_ref[SUBLANES:SUBLANES + tm, cols] = raw
        acc = raw * cw[3:4]
        for d in (1, 2, 3):
            acc = acc + stage_ref[SUBLANES - d:SUBLANES - d + tm, cols] * cw[3 - d:4 - d]
        stage_ref[0:SUBLANES, cols] = raw[tm - SUBLANES:tm]
        o_ref[:, cols] = post(acc)


def _conv_proj(xb, w, cw, *, tm, tn, seq, col0, n, l2_heads=False, n_scaled_tiles=0, scale=1.0):
    t, k = xb.shape
    nt = t // tm
    assert col0 % tn == 0 and n % tn == 0
    j0 = col0 // tn
    body = functools.partial(_conv_proj_body, tm=tm, tn=tn, tiles_per_seq=seq // tm,
                             l2_heads=l2_heads, n_scaled_tiles=n_scaled_tiles, scale=scale)
    return pl.pallas_call(
        body,
        grid=(n // tn, nt),
        in_specs=[
            pl.BlockSpec((tm, k), lambda j, i: (i, 0)),
            pl.BlockSpec((k, tn), lambda j, i: (0, j0 + j)),
            pl.BlockSpec((4, tn), lambda j, i: (0, j0 + j)),
        ],
        out_specs=pl.BlockSpec((tm, tn), lambda j, i: (i, j)),
        out_shape=jax.ShapeDtypeStruct((t, n), F32),
        scratch_shapes=[pltpu.VMEM((k, tn), BF16),
                        pltpu.VMEM((SUBLANES + tm, tn), F32)],
        compiler_params=_params("arbitrary", "arbitrary"),
        name="conv_proj",
    )(xb, w, cw)


def _gate_proj_body(x_ref, w_ref, alog_ref, dtb_ref, bg_ref, gc_ref, gr_ref, *, tm):
    raw = _dot(x_ref[...], w_ref[...])
    lane = lax.broadcasted_iota(jnp.int32, raw.shape, 1)
    beta = jax.nn.sigmoid(raw)
    g = -jnp.exp(alog_ref[...]) * _softplus(raw + dtb_ref[...])
    g = jnp.where((lane >= GDN_V_HEADS) & (lane < 2 * GDN_V_HEADS), g, 0.0)
    bg_ref[...] = jnp.where(lane < GDN_V_HEADS, beta, g)
    gc = _chunk_cumsum(g, tm)
    gc_ref[...] = gc
    width = (tm // CHUNK) * LANES
    t_idx = lax.broadcasted_iota(jnp.int32, (tm, width), 0)
    col = lax.broadcasted_iota(jnp.int32, (tm, width), 1)
    sel = (t_idx == (col // LANES) * CHUNK + (col % CHUNK)).astype(BF16)
    hi, mid, lo = _split3(gc)
    gr_ref[...] = _dot_tn(hi, sel) + _dot_tn(mid, sel) + _dot_tn(lo, sel)


def _gate_proj(xb, w, alog, dtb, *, tm):
    t, k = xb.shape
    return pl.pallas_call(
        functools.partial(_gate_proj_body, tm=tm),
        grid=(t // tm,),
        in_specs=[
            pl.BlockSpec((tm, k), lambda i: (i, 0)),
            pl.BlockSpec((k, LANES), lambda i: (0, 0)),
            pl.BlockSpec((1, LANES), lambda i: (0, 0)),
            pl.BlockSpec((1, LANES), lambda i: (0, 0)),
        ],
        out_specs=[pl.BlockSpec((tm, LANES), lambda i: (i, 0))] * 2
        + [pl.BlockSpec((LANES, (tm // CHUNK) * LANES), lambda i: (0, i))],
        out_shape=[jax.ShapeDtypeStruct((t, LANES), F32)] * 2
        + [jax.ShapeDtypeStruct((LANES, (t // CHUNK) * LANES), F32)],
        compiler_params=_params("arbitrary"),
        name="gate_proj",
    )(xb, w, alog, dtb)


def _gla_body(q_ref, k_ref, v_ref, go_ref, a_ref, wlr_ref, blr_ref, ng_ref, o_ref, st_ref, *, rows):
    @pl.when(pl.program_id(1) == 0)
    def _():
        st_ref[...] = jnp.zeros_like(st_ref)

    causal = _tril_mask(CHUNK)
    ng = ng_ref[...]
    heads = range(GLA_HEADS)
    chunks = range(rows // CHUNK)
    head = lambda a, h: a[:, h * GLA_DK:(h + 1) * GLA_DK]

    z = _dot(a_ref[...].astype(BF16), wlr_ref[...]) + blr_ref[...]
    b = _chunk_cumsum(-_softplus(-z) / GLA_TAU, rows)
    k = k_ref[...]
    q_dec = (q_ref[...] * (GLA_DK ** -0.5) * jnp.exp(b)).astype(BF16)
    k_inv = (k * jnp.exp(-b)).astype(BF16)
    rows_of = lambda a, c: a[c * CHUNK:(c + 1) * CHUNK]
    b_last = [rows_of(b, c)[CHUNK - 1:CHUNK] for c in chunks]
    k_end = [(rows_of(k, c) * jnp.exp(b_last[c] - rows_of(b, c))).astype(BF16) for c in chunks]
    scores = [[jnp.where(causal, _dot_nt(head(rows_of(q_dec, c), h), head(rows_of(k_inv, c), h)),
                         0.0).astype(BF16) for h in heads] for c in chunks]
    for c in chunks:
        sl = slice(c * CHUNK, (c + 1) * CHUNK)
        sts = [st_ref[h] for h in heads]
        vs = [v_ref[sl, h * GLA_DV:(h + 1) * GLA_DV].astype(BF16) for h in heads]
        qd = rows_of(q_dec, c)
        outs = [_dot(scores[c][h], vs[h]) + _dot_nt(head(qd, h), sts[h].astype(BF16)) for h in heads]
        decay = jnp.exp(b_last[c])
        new = [sts[h] * head(decay, h) + _dot_tn(vs[h], head(k_end[c], h)) for h in heads]
        for h in heads:
            st_ref[h] = new[h]
        normed = [o * lax.rsqrt(jnp.mean(o * o, axis=-1, keepdims=True) + RMS_EPS) * ng for o in outs]
        o_ref[sl, :] = (jnp.concatenate(normed, axis=1) * _silu(go_ref[sl, :])).astype(BF16)


def _gla(proj, gate, wlr, blr, ng, *, batch, seq, rows):
    t = batch * seq
    nblk = seq // rows
    row = lambda b, n: b * nblk + n
    return pl.pallas_call(
        functools.partial(_gla_body, rows=rows),
        grid=(batch, nblk),
        in_specs=[
            pl.BlockSpec((rows, GLA_KW), lambda b, n: (row(b, n), 0)),
            pl.BlockSpec((rows, GLA_KW), lambda b, n: (row(b, n), 1)),
            pl.BlockSpec((rows, GLA_VW), lambda b, n: (row(b, n), AB_V // GLA_VW)),
            pl.BlockSpec((rows, GLA_VW), lambda b, n: (row(b, n), AB_G // GLA_VW)),
            pl.BlockSpec((rows, LANES), lambda b, n: (row(b, n), 0)),
            pl.BlockSpec((LANES, GLA_KW), lambda b, n: (0, 0)),
            pl.BlockSpec((1, GLA_KW), lambda b, n: (0, 0)),
            pl.BlockSpec((1, GLA_DV), lambda b, n: (0, 0)),
        ],
        out_specs=pl.BlockSpec((rows, GLA_VW), lambda b, n: (row(b, n), 0)),
        out_shape=jax.ShapeDtypeStruct((t, GLA_VW), BF16),
        scratch_shapes=[pltpu.VMEM((GLA_HEADS, GLA_DV, GLA_DK), F32)],
        compiler_params=_params("arbitrary", "arbitrary"),
        name="gla",
    )(proj, proj, proj, proj, gate, wlr, blr, ng)


def _s5_body(u_ref, bbd_ref, cbd_ref, pw_ref, d_ref, y_ref, xs_ref, cr_ref, ci_ref, *, rows):
    @pl.when(pl.program_id(2) == 0)
    def _():
        cr_ref[...] = jnp.zeros_like(cr_ref)
        ci_ref[...] = jnp.zeros_like(ci_ref)

    ns = S5_TILE_STATES
    u = u_ref[...]
    bu = _dot(u.astype(BF16), bbd_ref[0])
    levels = [(pw_ref[0, 2 * l], pw_ref[0, 2 * l + 1], 1 << l) for l in range(3)]
    qr, qi = pw_ref[0, 6], pw_ref[0, 7]
    cr = cr_ref[...]
    ci = ci_ref[...]
    for blk in range(rows // SUBLANES):
        sl = slice(blk * SUBLANES, (blk + 1) * SUBLANES)
        xr = bu[sl, :ns]
        xi = bu[sl, ns:]
        for pr, pi, s in levels:
            sr = pltpu.roll(xr, s, 0)
            si = pltpu.roll(xi, s, 0)
            xr, xi = xr + (pr * sr - pi * si), xi + (pr * si + pi * sr)
        xr, xi = xr + (qr * cr - qi * ci), xi + (qr * ci + qi * cr)
        cr = jnp.broadcast_to(xr[SUBLANES - 1:SUBLANES], (SUBLANES, ns))
        ci = jnp.broadcast_to(xi[SUBLANES - 1:SUBLANES], (SUBLANES, ns))
        xs_ref[sl, :ns] = xr
        xs_ref[sl, ns:] = xi
    cr_ref[...] = cr
    ci_ref[...] = ci
    y = _dot(xs_ref[...].astype(BF16), cbd_ref[0]) + d_ref[...] * u
    y3 = y * y * y
    y_ref[...] = y * (0.5 * (1.0 + jnp.tanh(np.float32(math.sqrt(2.0 / math.pi)) * (y + 0.044715 * y3))))


def _s5(proj, bbd, cbd, pw, d, *, batch, seq, rows, u_col0):
    t = batch * seq
    nblk = seq // rows
    ns = S5_TILE_STATES
    return pl.pallas_call(
        functools.partial(_s5_body, rows=rows),
        grid=(batch, S5_TILES, nblk),
        in_specs=[
            pl.BlockSpec((rows, LANES), lambda b, j, n: (b * nblk + n, u_col0 + j)),
            pl.BlockSpec((1, LANES, 2 * ns), lambda b, j, n: (j, 0, 0)),
            pl.BlockSpec((1, 2 * ns, LANES), lambda b, j, n: (j, 0, 0)),
            pl.BlockSpec((1, 8, SUBLANES, ns), lambda b, j, n: (j, 0, 0, 0)),
            pl.BlockSpec((1, LANES), lambda b, j, n: (0, j)),
        ],
        out_specs=pl.BlockSpec((rows, LANES), lambda b, j, n: (b * nblk + n, j)),
        out_shape=jax.ShapeDtypeStruct((t, S5_WIDTH), F32),
        scratch_shapes=[pltpu.VMEM((rows, 2 * ns), F32),
                        pltpu.VMEM((SUBLANES, ns), F32),
                        pltpu.VMEM((SUBLANES, ns), F32)],
        compiler_params=_params("arbitrary", "arbitrary", "arbitrary"),
        name="s5",
    )(proj, bbd, cbd, pw, d)


def _s5_tables(lam_re, lam_im, b_re, b_im, c_re, c_im, log_step):
    step = jnp.exp(log_step)[:, None]
    mag = jnp.exp(lam_re * step)
    a_re = mag * jnp.cos(lam_im * step)
    a_im = mag * jnp.sin(lam_im * step)
    inv_den = 1.0 / (lam_re * lam_re + lam_im * lam_im)
    f_re = ((a_re - 1.0) * lam_re + a_im * lam_im) * inv_den
    f_im = (a_im * lam_re - (a_re - 1.0) * lam_im) * inv_den
    bb_re = f_re[..., None] * b_re - f_im[..., None] * b_im
    bb_im = f_re[..., None] * b_im + f_im[..., None] * b_re
    tg, nt = S5_TILE_GROUPS, S5_TILES
    eye = jnp.eye(tg, dtype=F32)

    def in_table(bb):
        bb = bb.reshape(nt, tg, S5_STATE, S5_GROUP)
        return jnp.einsum('jgph,gk->jghkp', bb, eye).reshape(nt, LANES, S5_TILE_STATES)

    def out_table(cc):
        cc = cc.reshape(nt, tg, S5_GROUP, S5_STATE)
        return jnp.einsum('jghp,gk->jgpkh', cc, eye).reshape(nt, S5_TILE_STATES, LANES)

    bbd = jnp.concatenate([in_table(bb_re), in_table(bb_im)], axis=2).astype(BF16)
    cbd = jnp.concatenate([out_table(c_re), -out_table(c_im)], axis=1).astype(BF16)

    def cmul(x, y):
        return x[0] * y[0] - x[1] * y[1], x[0] * y[1] + x[1] * y[0]

    a1 = (a_re.reshape(nt, 1, S5_TILE_STATES), a_im.reshape(nt, 1, S5_TILE_STATES))
    powers = [a1]
    for _ in range(SUBLANES - 1):
        powers.append(cmul(powers[-1], a1))
    row = jnp.arange(SUBLANES)[None, :, None]
    tabs = []
    for s in (1, 2, 4):
        pr, pi = powers[s - 1]
        tabs.append(jnp.where(row >= s, pr, 0.0))
        tabs.append(jnp.where(row >= s, pi, 0.0))
    tabs.append(jnp.concatenate([p[0] for p in powers], axis=1))
    tabs.append(jnp.concatenate([p[1] for p in powers], axis=1))
    pw = jnp.stack([jnp.broadcast_to(tb, (nt, SUBLANES, S5_TILE_STATES)) for tb in tabs], axis=1)
    return bbd, cbd, pw.astype(F32)


def _mix0_out_body(og_ref, ys_ref, gw_ref, gb_ref, w1_ref, w2_ref, x_ref, g_ref, b_ref, o_ref):
    ys = ys_ref[...]
    gate = jax.nn.sigmoid(_dot(ys.astype(BF16), gw_ref[...]) + gb_ref[...])
    os5 = (ys * gate).astype(BF16)
    mix = _dot(og_ref[...], w1_ref[...]) + _dot(os5, w2_ref[...])
    o_ref[...] = _layer_norm(ALPHA * x_ref[...] + mix, g_ref[...], b_ref[...])


def _mix0_out(og, ys, gw, gb, w1, w2, x, g, b, *, tm):
    t, d = x.shape
    const = lambda shape: pl.BlockSpec(shape, lambda i: (0, 0))
    return pl.pallas_call(
        _mix0_out_body,
        grid=(t // tm,),
        in_specs=[
            pl.BlockSpec((tm, GLA_VW), lambda i: (i, 0)),
            pl.BlockSpec((tm, S5_WIDTH), lambda i: (i, 0)),
            const((S5_WIDTH, S5_WIDTH)), const((1, S5_WIDTH)),
            const((GLA_VW, d)), const((S5_WIDTH, d)),
            pl.BlockSpec((tm, d), lambda i: (i, 0)),
            const((1, d)), const((1, d)),
        ],
        out_specs=pl.BlockSpec((tm, d), lambda i: (i, 0)),
        out_shape=jax.ShapeDtypeStruct((t, d), F32),
        compiler_params=_params("arbitrary"),
        name="mix0_out",
    )(og, ys, gw, gb, w1, w2, x, g, b)


GDN_REP = GDN_V_HEADS // GDN_QK_HEADS


def _block_diag2(m, second):
    zero = jnp.zeros_like(m)
    return jnp.concatenate([jnp.where(second, zero, m), jnp.where(second, m, zero)], axis=0)


def _gdn_body(q_ref, k_ref, v_ref, z_ref, bg_ref, gc_ref, gr_ref, ng_ref, o_ref, st_ref, *, rows, heads):
    hg = pl.program_id(1)
    c2 = 2 * CHUNK
    vw = GDN_REP * GDN_DV

    @pl.when(pl.program_id(2) == 0)
    def _():
        st_ref[...] = jnp.zeros_like(st_ref)

    lane2 = lax.broadcasted_iota(jnp.int32, (CHUNK, c2), 1)
    row2 = lax.broadcasted_iota(jnp.int32, (CHUNK, c2), 0)
    second = lane2 >= CHUNK
    pos = jnp.where(second, lane2 - CHUNK, lane2)
    incl2 = pos <= row2
    strict2 = pos < row2
    eye2 = pos == row2
    lane_bg = lax.broadcasted_iota(jnp.int32, (CHUNK, LANES), 1)
    second_v = lax.broadcasted_iota(jnp.int32, (CHUNK, vw), 1) >= GDN_DV
    st_r = lax.broadcasted_iota(jnp.int32, (GDN_REP * GDN_DK, vw), 0) >= GDN_DK
    st_c = lax.broadcasted_iota(jnp.int32, (GDN_REP * GDN_DK, vw), 1) >= GDN_DV
    same_head = st_r == st_c
    ng = ng_ref[...]
    zero_rhs = jnp.zeros((CHUNK, GDN_DV + GDN_DK), BF16)

    def chunk(c, carry):
        r0 = pl.multiple_of(c * CHUNK, CHUNK)
        bg = bg_ref[pl.ds(r0, CHUNK), :]
        gcb = gc_ref[pl.ds(r0, CHUNK), :]

        def column(arr, idx):
            return jnp.sum(jnp.where(lane_bg == idx, arr, 0.0), axis=-1, keepdims=True)

        hs = range(heads)
        ks, qs, prods, cols, decays, xs, ps, pbds = [], [], [], [], [], [], [], []
        for g in hs:
            k = k_ref[pl.ds(r0, CHUNK), g * GDN_DK:(g + 1) * GDN_DK]
            q = q_ref[pl.ds(r0, CHUNK), g * GDN_DK:(g + 1) * GDN_DK]
            kb = k.astype(BF16)
            ks.append(k)
            qs.append(q)
            prods.append(_dot_nt(jnp.concatenate([kb, q.astype(BF16)], axis=0),
                                 jnp.concatenate([kb, kb], axis=0)))
        for g in hs:
            hq = hg * heads + g
            b0 = column(bg, GDN_REP * hq)
            b1 = column(bg, GDN_REP * hq + 1)
            g0 = column(gcb, GDN_V_HEADS + GDN_REP * hq)
            g1 = column(gcb, GDN_V_HEADS + GDN_REP * hq + 1)
            cols.append((b0, b1, g0, g1, g0[CHUNK - 1:CHUNK, :], g1[CHUNK - 1:CHUNK, :]))
            gc_row = jnp.where(second[0:1], gr_ref[c, pl.ds(GDN_REP * hq + 1, 1), :],
                               gr_ref[c, pl.ds(GDN_REP * hq, 1), :])
            decay2 = jnp.exp(jnp.where(incl2, jnp.where(second, g1, g0) - gc_row, -jnp.inf))
            a2 = jnp.where(strict2, prods[g][:CHUNK] * jnp.where(second, b1, b0) * decay2, 0.0)
            decays.append(decay2)
            xs.append(jnp.where(eye2, 1.0, 0.0) - a2)
            ps.append(a2.astype(BF16))
            pbds.append(_block_diag2(ps[g], second))
        power = 1
        while 2 * power < CHUNK:
            for g in hs:
                ps[g] = _dot(ps[g], pbds[g]).astype(BF16)
                pbds[g] = _block_diag2(ps[g], second)
            for g in hs:
                xs[g] = xs[g] + _dot(xs[g].astype(BF16), pbds[g])
            power *= 2
        sols, exps = [], []
        for g in hs:
            b0, b1, g0, g1, gl0, gl1 = cols[g]
            e0 = jnp.exp(g0)
            e1 = jnp.exp(g1)
            exps.append((e0, e1))
            v2 = v_ref[pl.ds(r0, CHUNK), g * vw:(g + 1) * vw]
            rhs0 = jnp.concatenate([v2[:, :GDN_DV] * b0, (ks[g] * b0) * e0], axis=1).astype(BF16)
            rhs1 = jnp.concatenate([v2[:, GDN_DV:] * b1, (ks[g] * b1) * e1], axis=1).astype(BF16)
            rhs_bd = jnp.concatenate([jnp.concatenate([rhs0, zero_rhs], axis=1),
                                      jnp.concatenate([zero_rhs, rhs1], axis=1)], axis=0)
            sols.append(_dot(xs[g].astype(BF16), rhs_bd))
        sts, wss = [], []
        for g in hs:
            sol = sols[g]
            e0, e1 = exps[g]
            w2 = jnp.concatenate([sol[:, GDN_DV:vw], sol[:, vw + GDN_DV:]], axis=1)
            qd2 = jnp.concatenate([qs[g] * e0, qs[g] * e1], axis=1)
            sts.append(st_ref[g])
            wss.append(_dot(jnp.concatenate([w2, qd2], axis=0).astype(BF16), sts[g].astype(BF16)))
        new_states, results = [], []
        for g in hs:
            b0, b1, g0, g1, gl0, gl1 = cols[g]
            sol = sols[g]
            u2 = jnp.concatenate([sol[:, :GDN_DV], sol[:, vw:vw + GDN_DV]], axis=1)
            v_new = (u2 - wss[g][:CHUNK]).astype(BF16)
            attn2 = (prods[g][CHUNK:] * decays[g]).astype(BF16)
            o2 = wss[g][CHUNK:] + _dot(attn2, _block_diag2(v_new, second_v))
            k_end2 = jnp.concatenate([ks[g] * jnp.exp(gl0 - g0), ks[g] * jnp.exp(gl1 - g1)],
                                     axis=1).astype(BF16)
            upd = _dot_tn(k_end2, v_new)
            new_states.append(jnp.where(st_r, jnp.exp(gl1), jnp.exp(gl0)) * sts[g]
                              + jnp.where(same_head, upd, 0.0))
            outs = []
            for hh in range(GDN_REP):
                o = o2[:, hh * GDN_DV:(hh + 1) * GDN_DV]
                outs.append(o * lax.rsqrt(jnp.mean(o * o, axis=-1, keepdims=True) + RMS_EPS) * ng)
            z2 = z_ref[pl.ds(r0, CHUNK), g * vw:(g + 1) * vw]
            results.append((jnp.concatenate(outs, axis=1) * _silu(z2)).astype(BF16))
        for g in hs:
            st_ref[g] = new_states[g]
            o_ref[pl.ds(r0, CHUNK), g * vw:(g + 1) * vw] = results[g]
        return carry

    lax.fori_loop(0, rows // CHUNK, chunk, 0)


def _gdn(qk, v, z, bg, gc, gc_rows, ng, *, batch, seq, rows, heads):
    t = batch * seq
    nblk = seq // rows
    cpb = rows // CHUNK
    row = lambda b, h, n: b * nblk + n
    vw = heads * GDN_REP * GDN_DV
    kw = heads * GDN_DK
    return pl.pallas_call(
        functools.partial(_gdn_body, rows=rows, heads=heads),
        grid=(batch, GDN_QK_HEADS // heads, nblk),
        in_specs=[
            pl.BlockSpec((rows, kw), lambda b, h, n: (row(b, h, n), h)),
            pl.BlockSpec((rows, kw), lambda b, h, n: (row(b, h, n), GDN_QK_HEADS // heads + h)),
            pl.BlockSpec((rows, vw), lambda b, h, n: (row(b, h, n), h)),
            pl.BlockSpec((rows, vw), lambda b, h, n: (row(b, h, n), h)),
            pl.BlockSpec((rows, LANES), lambda b, h, n: (row(b, h, n), 0)),
            pl.BlockSpec((rows, LANES), lambda b, h, n: (row(b, h, n), 0)),
            pl.BlockSpec((cpb, GDN_V_HEADS, GDN_REP * CHUNK), lambda b, h, n: (row(b, h, n), 0, 0)),
            pl.BlockSpec((1, GDN_DV), lambda b, h, n: (0, 0)),
        ],
        out_specs=pl.BlockSpec((rows, vw), lambda b, h, n: (row(b, h, n), h)),
        out_shape=jax.ShapeDtypeStruct((t, GDN_VW), BF16),
        scratch_shapes=[pltpu.VMEM((heads, GDN_REP * GDN_DK, GDN_REP * GDN_DV), F32)],
        compiler_params=_params("arbitrary", "arbitrary", "arbitrary"),
        name="gdn",
    )(qk, qk, v, z, bg, gc, gc_rows, ng)


def _out_ln_body(a_ref, w_ref, x_ref, g_ref, b_ref, o_ref):
    mix = _dot(a_ref[...], w_ref[...])
    o_ref[...] = _layer_norm(ALPHA * x_ref[...] + mix, g_ref[...], b_ref[...])


def _out_ln(a, w, x, g, b, *, tm):
    t, d = x.shape
    kdim = a.shape[1]
    const = lambda shape: pl.BlockSpec(shape, lambda i: (0, 0), pipeline_mode=pl.Buffered(1))
    return pl.pallas_call(
        _out_ln_body,
        grid=(t // tm,),
        in_specs=[
            pl.BlockSpec((tm, kdim), lambda i: (i, 0)),
            const((kdim, d)),
            pl.BlockSpec((tm, d), lambda i: (i, 0)),
            const((1, d)), const((1, d)),
        ],
        out_specs=pl.BlockSpec((tm, d), lambda i: (i, 0)),
        out_shape=jax.ShapeDtypeStruct((t, d), F32),
        compiler_params=_params("arbitrary"),
        name="out_ln",
    )(a, w, x, g, b)


def _tiles(batch, seq):
    tm = min(512, seq)
    return dict(tm=tm, ffn_tm=min(1024, seq), proj_tm=min(1024, seq), gla_rows=min(256, seq), s5_rows=min(512, seq), gdn_rows=min(256, seq),
                gdn_heads=16)


def _ffn(x, wg, wu, wd, layer, g, b, tm, emit_bf16):
    return _ffn_ln(x, wg, wu, wd, layer, g[None], b[None], tm=tm, tf=256, emit_bf16=emit_bf16)


def _pad_cols(w, n):
    return jnp.pad(w, ((0, 0), (0, n - w.shape[1])))


def _gla_s5_layer(x, xb, p, i, g, b, batch, seq, tl):
    w_in = p['ab_w_in'][i]
    w_gate = _pad_cols(w_in[:, AB_A:AB_U], LANES).astype(BF16)
    proj = _proj(xb, w_in, tm=tl['proj_tm'], tn=1024, col0=0, n=AB_A)
    u = _proj(xb, w_in[:, AB_U:], tm=tl['proj_tm'], tn=1024)
    gate = _proj(xb, w_gate, tm=tl['tm'], tn=LANES)
    wlr = jnp.pad(p['gla_w_lr'][i], ((0, LANES - GLA_RANK), (0, 0))).astype(BF16)
    o_gla = _gla(proj, gate, wlr, p['gla_b_lr'][i][None], p['gla_norm_g'][i][None],
                 batch=batch, seq=seq, rows=tl['gla_rows'])
    bbd, cbd, pw = _s5_tables(p['s5_lam_re'][i], p['s5_lam_im'][i], p['s5_b_re'][i], p['s5_b_im'][i],
                              p['s5_c_re'][i], p['s5_c_im'][i], p['s5_log_step'][i])
    ys = _s5(u, bbd, cbd, pw, p['s5_d'][i][None], batch=batch, seq=seq, rows=tl['s5_rows'], u_col0=0)
    w_out = p['ab_w_out'][i].astype(BF16)
    return _mix0_out(o_gla, ys, p['s5_glu_w'][i].astype(BF16), p['s5_glu_b'][i][None],
                     w_out[:GLA_VW], w_out[GLA_VW:], x, g[None], b[None], tm=tl['tm'])


def _gdn_layer(x, xb, p, i, g, b, batch, seq, tl):
    t = batch * seq
    w_in = p['gdn_w_in'][i]
    conv_w = p['gdn_conv_w'][i]
    tm = tl['tm']
    qk = _conv_proj(xb, w_in, conv_w, tm=tl['proj_tm'], tn=1024, seq=seq, col0=0, n=2 * GDN_KW,
                    l2_heads=True, n_scaled_tiles=GDN_KW // 1024, scale=GDN_DK ** -0.5)
    v = _conv_proj(xb, w_in, conv_w, tm=tl['proj_tm'], tn=1024, seq=seq, col0=2 * GDN_KW, n=GDN_VW)
    z = _proj(xb, w_in, tm=tl['proj_tm'], tn=1024, col0=GDN_Z, n=GDN_VW)
    w_gate = _pad_cols(w_in[:, GDN_B:GDN_IN], LANES).astype(BF16)
    lane_pad = lambda a: jnp.pad(a, (GDN_V_HEADS, LANES - 2 * GDN_V_HEADS))[None]
    bg, gc, gr = _gate_proj(xb, w_gate, lane_pad(p['gdn_a_log'][i]), lane_pad(p['gdn_dt_bias'][i]), tm=tm)
    gc_rows = gr.reshape(LANES, t // CHUNK, LANES)[GDN_V_HEADS:2 * GDN_V_HEADS].transpose(1, 0, 2)
    o = _gdn(qk, v, z, bg, gc, gc_rows, p['gdn_norm_g'][i][None], batch=batch, seq=seq,
             rows=tl['gdn_rows'], heads=tl['gdn_heads'])
    return _out_ln(o, p['gdn_w_out'][i].astype(BF16), x, g[None], b[None], tm=min(256, tm))


def kernel(x, ffn_a_gate, ffn_a_up, ffn_a_down, ffn_b_gate, ffn_b_up, ffn_b_down, ln_g, ln_b, ab_w_in, gla_w_lr, gla_b_lr, gla_norm_g, s5_lam_re, s5_lam_im, s5_b_re, s5_b_im, s5_c_re, s5_c_im, s5_d, s5_log_step, s5_glu_w, s5_glu_b, ab_w_out, gdn_w_in, gdn_conv_w, gdn_a_log, gdn_dt_bias, gdn_norm_g, gdn_w_out):
    batch, seq, d = x.shape
    assert d == D_MODEL and seq % CHUNK == 0
    p = dict(ab_w_in=ab_w_in, gla_w_lr=gla_w_lr, gla_b_lr=gla_b_lr, gla_norm_g=gla_norm_g,
             s5_lam_re=s5_lam_re, s5_lam_im=s5_lam_im, s5_b_re=s5_b_re, s5_b_im=s5_b_im,
             s5_c_re=s5_c_re, s5_c_im=s5_c_im, s5_d=s5_d, s5_log_step=s5_log_step,
             s5_glu_w=s5_glu_w, s5_glu_b=s5_glu_b, ab_w_out=ab_w_out, gdn_w_in=gdn_w_in,
             gdn_conv_w=gdn_conv_w, gdn_a_log=gdn_a_log, gdn_dt_bias=gdn_dt_bias,
             gdn_norm_g=gdn_norm_g, gdn_w_out=gdn_w_out)
    tl = _tiles(batch, seq)
    h = x.reshape(batch * seq, d).astype(F32)
    for layer in range(DEPTH):
        h, hb = _ffn(h, ffn_a_gate, ffn_a_up, ffn_a_down, layer,
                     ln_g[layer, 0], ln_b[layer, 0], tl['ffn_tm'], True)
        i = layer // 2
        mixer = _gla_s5_layer if layer % 2 == 0 else _gdn_layer
        h = mixer(h, hb, p, i, ln_g[layer, 1], ln_b[layer, 1], batch, seq, tl)
        (h,) = _ffn(h, ffn_b_gate, ffn_b_up, ffn_b_down, layer,
                    ln_g[layer, 2], ln_b[layer, 2], tl['ffn_tm'], False)
    return h.reshape(batch, seq, d)
```

```python
import functools
import math

import jax
import jax.numpy as jnp
import numpy as np
from jax import lax
from jax.experimental import pallas as pl
from jax.experimental.pallas import tpu as pltpu

F32 = jnp.float32
BF16 = jnp.bfloat16

LANES = 128
SUBLANES = 8
VMEM_LIMIT_BYTES = 60 * 1024 * 1024
LN_ROWS = 64
CONV_SUB = 256

D_MODEL = 2048
DEPTH = 2
ALPHA = (2.0 * DEPTH) ** 0.25
MACARON = 0.5
LN_EPS = 1e-5
RMS_EPS = 1e-6
L2_EPS = 1e-6
D_FF = 5504

GLA_HEADS = 4
GLA_DK = 128
GLA_DV = 256
GLA_RANK = 16
GLA_TAU = 16.0
CHUNK = 64
GLA_KW = GLA_HEADS * GLA_DK
GLA_VW = GLA_HEADS * GLA_DV

S5_WIDTH = 1024
S5_GROUP = 16
S5_GROUPS = 64
S5_STATE = 64
S5_TILE_GROUPS = LANES // S5_GROUP
S5_TILES = S5_WIDTH // LANES
S5_TILE_STATES = S5_TILE_GROUPS * S5_STATE

AB_K = GLA_KW
AB_V = AB_K + GLA_KW
AB_G = AB_V + GLA_VW
AB_A = AB_G + GLA_VW
AB_U = AB_A + GLA_RANK
AB_IN = AB_U + S5_WIDTH

GDN_QK_HEADS = 16
GDN_V_HEADS = 32
GDN_DK = 128
GDN_DV = 128
GDN_KW = GDN_QK_HEADS * GDN_DK
GDN_VW = GDN_V_HEADS * GDN_DV
GDN_QKV = 2 * GDN_KW + GDN_VW
GDN_Z = GDN_QKV
GDN_B = GDN_Z + GDN_VW
GDN_A = GDN_B + GDN_V_HEADS
GDN_IN = GDN_A + GDN_V_HEADS


def _params(*sem):
    return pltpu.CompilerParams(dimension_semantics=sem, vmem_limit_bytes=VMEM_LIMIT_BYTES)


def _dot(a, b):
    return jnp.dot(a, b, preferred_element_type=F32)


def _dot_nt(a, b):
    return lax.dot_general(a, b, (((1,), (1,)), ((), ())), preferred_element_type=F32)


def _dot_tn(a, b):
    return lax.dot_general(a, b, (((0,), (0,)), ((), ())), preferred_element_type=F32)


def _layer_norm(y, g, b):
    mu = jnp.mean(y, axis=-1, keepdims=True)
    yc = y - mu
    var = jnp.mean(yc * yc, axis=-1, keepdims=True)
    return yc * lax.rsqrt(var + LN_EPS) * g + b


def _silu(v):
    return v * jax.nn.sigmoid(v)


def _softplus(v):
    return jnp.maximum(v, 0.0) + jnp.log1p(jnp.exp(-jnp.abs(v)))


def _split3(v):
    hi = v.astype(BF16)
    r1 = v - hi.astype(F32)
    mid = r1.astype(BF16)
    lo = (r1 - mid.astype(F32)).astype(BF16)
    return hi, mid, lo


def _tril_mask(n, strict=False):
    r = lax.broadcasted_iota(jnp.int32, (n, n), 0)
    c = lax.broadcasted_iota(jnp.int32, (n, n), 1)
    return (c < r) if strict else (c <= r)


def _chunk_cumsum(v, rows):
    r = lax.broadcasted_iota(jnp.int32, (rows, rows), 0)
    c = lax.broadcasted_iota(jnp.int32, (rows, rows), 1)
    tri = ((c <= r) & ((c // CHUNK) == (r // CHUNK))).astype(BF16)
    hi, mid, lo = _split3(v)
    return _dot(tri, hi) + _dot(tri, mid) + _dot(tri, lo)


def _ffn_body(x_ref, wg_ref, wu_ref, wd_ref, wgt_ref, wut_ref, wdt_ref, g_ref, b_ref, o_ref, *rest, nj):
    ob_ref, xb_ref = rest if len(rest) == 2 else (None, rest[0])
    j = pl.program_id(1)

    @pl.when(j == 0)
    def _():
        xb_ref[...] = x_ref[...].astype(BF16)
        o_ref[...] = jnp.zeros_like(o_ref)

    def down(wg, wu, wd):
        xb = xb_ref[...]
        h = (_silu(_dot(xb, wg.astype(BF16))) * _dot(xb, wu.astype(BF16))).astype(BF16)
        return _dot(h, wd.astype(BF16))

    @pl.when(j < nj - 1)
    def _():
        o_ref[...] += down(wg_ref[...], wu_ref[...], wd_ref[...])

    @pl.when(j == nj - 1)
    def _():
        o_ref[...] += down(wgt_ref[...], wut_ref[...], wdt_ref[...])
        g = g_ref[...]
        b = b_ref[...]

        def rows(r, carry):
            sl = pl.ds(pl.multiple_of(r * LN_ROWS, LN_ROWS), LN_ROWS)
            yn = _layer_norm(ALPHA * x_ref[sl, :] + MACARON * o_ref[sl, :], g, b)
            o_ref[sl, :] = yn
            if ob_ref is not None:
                ob_ref[sl, :] = yn.astype(BF16)
            return carry

        lax.fori_loop(0, o_ref.shape[0] // LN_ROWS, rows, 0)


def _ffn_ln(x, wg, wu, wd, layer, g, b, *, tm, tf, emit_bf16):
    t, d = x.shape
    f = wg.shape[2]
    n_full = f // tf
    f_tail = f - n_full * tf
    assert f_tail > 0 and f_tail % LANES == 0
    nj = n_full + 1
    last = n_full - 1
    const = lambda shape: pl.BlockSpec(shape, lambda i, j: (0, 0), pipeline_mode=pl.Buffered(1))
    tile = pl.BlockSpec((tm, d), lambda i, j: (i, 0))
    out_specs = [tile, tile] if emit_bf16 else [tile]
    out_shape = [jax.ShapeDtypeStruct((t, d), F32), jax.ShapeDtypeStruct((t, d), BF16)][:len(out_specs)]
    return pl.pallas_call(
        functools.partial(_ffn_body, nj=nj),
        grid=(t // tm, nj),
        in_specs=[
            pl.BlockSpec((tm, d), lambda i, j: (i, 0), pipeline_mode=pl.Buffered(1)),
            pl.BlockSpec((None, d, tf), lambda i, j: (layer, 0, jnp.minimum(j, last))),
            pl.BlockSpec((None, d, tf), lambda i, j: (layer, 0, jnp.minimum(j, last))),
            pl.BlockSpec((None, tf, d), lambda i, j: (layer, jnp.minimum(j, last), 0)),
            const((d, f_tail)), const((d, f_tail)), const((f_tail, d)),
            const((1, d)), const((1, d)),
        ],
        out_specs=out_specs,
        out_shape=out_shape,
        scratch_shapes=[pltpu.VMEM((tm, d), BF16)],
        compiler_params=_params("arbitrary", "arbitrary"),
        name="ffn_ln",
    )(x, wg, wu, wd, wg[layer, :, f - f_tail:], wu[layer, :, f - f_tail:], wd[layer, f - f_tail:], g, b)


def _load_weight_tile(w_ref, wb_ref):
    w = w_ref[...]
    if w.shape != wb_ref.shape:
        w = w.T
    wb_ref[...] = w.astype(BF16)


def _proj_body(x_ref, w_ref, o_ref, wb_ref):
    @pl.when(pl.program_id(1) == 0)
    def _():
        _load_weight_tile(w_ref, wb_ref)

    o_ref[...] = _dot(x_ref[...], wb_ref[...])


def _weight_spec(w, k, tn, j0, layer):
    if w.ndim == 3:
        return pl.BlockSpec((None, tn, k), lambda j, i: (layer, j0 + j, 0))
    return pl.BlockSpec((k, tn), lambda j, i: (0, j0 + j))


def _proj(xb, w, *, tm, tn, col0=0, n=None, layer=0):
    t, k = xb.shape
    n = (w.shape[1] if w.ndim == 3 else w.shape[-1]) if n is None else n
    assert col0 % tn == 0 and n % tn == 0
    j0 = col0 // tn
    return pl.pallas_call(
        _proj_body,
        grid=(n // tn, t // tm),
        in_specs=[
            pl.BlockSpec((tm, k), lambda j, i: (i, 0)),
            _weight_spec(w, k, tn, j0, layer),
        ],
        out_specs=pl.BlockSpec((tm, tn), lambda j, i: (i, j)),
        out_shape=jax.ShapeDtypeStruct((t, n), F32),
        scratch_shapes=[pltpu.VMEM((k, tn), BF16)],
        compiler_params=_params("arbitrary", "arbitrary"),
        name="proj",
    )(xb, w)


def _conv_proj_body(x_ref, w_ref, cw_ref, o_ref, wb_ref, stage_ref, *, tm, tn, tiles_per_seq,
                    l2_heads, n_scaled_tiles, scale):
    jt = pl.program_id(0)
    i = pl.program_id(1)

    @pl.when(i == 0)
    def _():
        _load_weight_tile(w_ref, wb_ref)

    @pl.when(i % tiles_per_seq == 0)
    def _():
        stage_ref[0:SUBLANES, :] = jnp.zeros((SUBLANES, tn), F32)

    if l2_heads:
        fac = jnp.where(jt < n_scaled_tiles, jnp.float32(scale), jnp.float32(1.0))

    def post(v):
        v = _silu(v)
        if not l2_heads:
            return v
        out = []
        for h in range(CONV_SUB // LANES):
            vh = v[:, h * LANES:(h + 1) * LANES]
            ss = jnp.sum(vh * vh, axis=-1, keepdims=True)
            out.append(vh * lax.rsqrt(ss + L2_EPS) * fac)
        return jnp.concatenate(out, axis=1)

    subs = [slice(s * CONV_SUB, (s + 1) * CONV_SUB) for s in range(tn // CONV_SUB)]

    x = x_ref[...]
    raws = [_dot(x, wb_ref[:, cols]) for cols in subs]
    for cols, raw in zip(subs, raws):
        cw = cw_ref[:, cols]
        stage_ref[SUBLANES:SUBLANES + tm, cols] = raw
        acc = raw * cw[3:4]
        for d in (1, 2, 3):
            acc = acc + stage_ref[SUBLANES - d:SUBLANES - d + tm, cols] * cw[3 - d:4 - d]
        stage_ref[0:SUBLANES, cols] = raw[tm - SUBLANES:tm]
        o_ref[:, cols] = post(acc)


def _conv_proj(xb, w, cw, *, tm, tn, seq, col0, n, layer=0, l2_heads=False, n_scaled_tiles=0, scale=1.0):
    t, k = xb.shape
    nt = t // tm
    assert col0 % tn == 0 and n % tn == 0
    j0 = col0 // tn
    body = functools.partial(_conv_proj_body, tm=tm, tn=tn, tiles_per_seq=seq // tm,
                             l2_heads=l2_heads, n_scaled_tiles=n_scaled_tiles, scale=scale)
    return pl.pallas_call(
        body,
        grid=(n // tn, nt),
        in_specs=[
            pl.BlockSpec((tm, k), lambda j, i: (i, 0)),
            _weight_spec(w, k, tn, j0, layer),
            pl.BlockSpec((4, tn), lambda j, i: (0, j0 + j)),
        ],
        out_specs=pl.BlockSpec((tm, tn), lambda j, i: (i, j)),
        out_shape=jax.ShapeDtypeStruct((t, n), F32),
        scratch_shapes=[pltpu.VMEM((k, tn), BF16),
                        pltpu.VMEM((SUBLANES + tm, tn), F32)],
        compiler_params=_params("arbitrary", "arbitrary"),
        name="conv_proj",
    )(xb, w, cw)


def _gate_proj_body(x_ref, w_ref, alog_ref, dtb_ref, bg_ref, gc_ref, gr_ref, *, tm):
    raw = _dot(x_ref[...], w_ref[...])
    lane = lax.broadcasted_iota(jnp.int32, raw.shape, 1)
    beta = jax.nn.sigmoid(raw)
    g = -jnp.exp(alog_ref[...]) * _softplus(raw + dtb_ref[...])
    g = jnp.where((lane >= GDN_V_HEADS) & (lane < 2 * GDN_V_HEADS), g, 0.0)
    bg_ref[...] = jnp.where(lane < GDN_V_HEADS, beta, g)
    gc = _chunk_cumsum(g, tm)
    gc_ref[...] = gc
    width = (tm // CHUNK) * LANES
    t_idx = lax.broadcasted_iota(jnp.int32, (tm, width), 0)
    col = lax.broadcasted_iota(jnp.int32, (tm, width), 1)
    sel = (t_idx == (col // LANES) * CHUNK + (col % CHUNK)).astype(BF16)
    hi, mid, lo = _split3(gc)
    rows = _dot_tn(hi, sel) + _dot_tn(mid, sel) + _dot_tn(lo, sel)
    gr_ref[...] = rows[GDN_V_HEADS:2 * GDN_V_HEADS]


def _gate_proj(xb, w, alog, dtb, *, tm):
    t, k = xb.shape
    return pl.pallas_call(
        functools.partial(_gate_proj_body, tm=tm),
        grid=(t // tm,),
        in_specs=[
            pl.BlockSpec((tm, k), lambda i: (i, 0)),
            pl.BlockSpec((k, LANES), lambda i: (0, 0)),
            pl.BlockSpec((1, LANES), lambda i: (0, 0)),
            pl.BlockSpec((1, LANES), lambda i: (0, 0)),
        ],
        out_specs=[pl.BlockSpec((tm, LANES), lambda i: (i, 0))] * 2
        + [pl.BlockSpec((GDN_V_HEADS, (tm // CHUNK) * LANES), lambda i: (0, i))],
        out_shape=[jax.ShapeDtypeStruct((t, LANES), F32)] * 2
        + [jax.ShapeDtypeStruct((GDN_V_HEADS, (t // CHUNK) * LANES), F32)],
        compiler_params=_params("arbitrary"),
        name="gate_proj",
    )(xb, w, alog, dtb)


def _gla_body(q_ref, k_ref, v_ref, go_ref, a_ref, wlr_ref, blr_ref, ng_ref, o_ref, st_ref, *, rows):
    @pl.when(pl.program_id(1) == 0)
    def _():
        st_ref[...] = jnp.zeros_like(st_ref)

    causal = _tril_mask(CHUNK)
    ng = ng_ref[...]
    heads = range(GLA_HEADS)
    chunks = range(rows // CHUNK)
    head = lambda a, h: a[:, h * GLA_DK:(h + 1) * GLA_DK]

    z = _dot(a_ref[...].astype(BF16), wlr_ref[...]) + blr_ref[...]
    b = _chunk_cumsum(-_softplus(-z) / GLA_TAU, rows)
    k = k_ref[...]
    q_dec = (q_ref[...] * (GLA_DK ** -0.5) * jnp.exp(b)).astype(BF16)
    k_inv = (k * jnp.exp(-b)).astype(BF16)
    rows_of = lambda a, c: a[c * CHUNK:(c + 1) * CHUNK]
    b_last = [rows_of(b, c)[CHUNK - 1:CHUNK] for c in chunks]
    k_end = [(rows_of(k, c) * jnp.exp(b_last[c] - rows_of(b, c))).astype(BF16) for c in chunks]
    scores = [[jnp.where(causal, _dot_nt(head(rows_of(q_dec, c), h), head(rows_of(k_inv, c), h)),
                         0.0).astype(BF16) for h in heads] for c in chunks]
    for c in chunks:
        sl = slice(c * CHUNK, (c + 1) * CHUNK)
        sts = [st_ref[h] for h in heads]
        vs = [v_ref[sl, h * GLA_DV:(h + 1) * GLA_DV].astype(BF16) for h in heads]
        qd = rows_of(q_dec, c)
        outs = [_dot(scores[c][h], vs[h]) + _dot_nt(head(qd, h), sts[h].astype(BF16)) for h in heads]
        decay = jnp.exp(b_last[c])
        new = [sts[h] * head(decay, h) + _dot_tn(vs[h], head(k_end[c], h)) for h in heads]
        for h in heads:
            st_ref[h] = new[h]
        normed = [o * lax.rsqrt(jnp.mean(o * o, axis=-1, keepdims=True) + RMS_EPS) * ng for o in outs]
        o_ref[sl, :] = (jnp.concatenate(normed, axis=1) * _silu(go_ref[sl, :])).astype(BF16)


def _gla(proj, gate, wlr, blr, ng, *, batch, seq, rows):
    t = batch * seq
    nblk = seq // rows
    row = lambda b, n: b * nblk + n
    return pl.pallas_call(
        functools.partial(_gla_body, rows=rows),
        grid=(batch, nblk),
        in_specs=[
            pl.BlockSpec((rows, GLA_KW), lambda b, n: (row(b, n), 0)),
            pl.BlockSpec((rows, GLA_KW), lambda b, n: (row(b, n), 1)),
            pl.BlockSpec((rows, GLA_VW), lambda b, n: (row(b, n), AB_V // GLA_VW)),
            pl.BlockSpec((rows, GLA_VW), lambda b, n: (row(b, n), AB_G // GLA_VW)),
            pl.BlockSpec((rows, LANES), lambda b, n: (row(b, n), 0)),
            pl.BlockSpec((LANES, GLA_KW), lambda b, n: (0, 0)),
            pl.BlockSpec((1, GLA_KW), lambda b, n: (0, 0)),
            pl.BlockSpec((1, GLA_DV), lambda b, n: (0, 0)),
        ],
        out_specs=pl.BlockSpec((rows, GLA_VW), lambda b, n: (row(b, n), 0)),
        out_shape=jax.ShapeDtypeStruct((t, GLA_VW), BF16),
        scratch_shapes=[pltpu.VMEM((GLA_HEADS, GLA_DV, GLA_DK), F32)],
        compiler_params=_params("arbitrary", "arbitrary"),
        name="gla",
    )(proj, proj, proj, proj, gate, wlr, blr, ng)


def _s5_body(u_ref, bbd_ref, cbd_ref, pw_ref, d_ref, y_ref, xs_ref, cr_ref, ci_ref, *, rows):
    @pl.when(pl.program_id(2) == 0)
    def _():
        cr_ref[...] = jnp.zeros_like(cr_ref)
        ci_ref[...] = jnp.zeros_like(ci_ref)

    ns = S5_TILE_STATES
    u = u_ref[...]
    bu = _dot(u.astype(BF16), bbd_ref[0])
    levels = [(pw_ref[0, 2 * l], pw_ref[0, 2 * l + 1], 1 << l) for l in range(3)]
    qr, qi = pw_ref[0, 6], pw_ref[0, 7]
    cr = cr_ref[...]
    ci = ci_ref[...]
    for blk in range(rows // SUBLANES):
        sl = slice(blk * SUBLANES, (blk + 1) * SUBLANES)
        xr = bu[sl, :ns]
        xi = bu[sl, ns:]
        for pr, pi, s in levels:
            sr = pltpu.roll(xr, s, 0)
            si = pltpu.roll(xi, s, 0)
            xr, xi = xr + (pr * sr - pi * si), xi + (pr * si + pi * sr)
        xr, xi = xr + (qr * cr - qi * ci), xi + (qr * ci + qi * cr)
        cr = jnp.broadcast_to(xr[SUBLANES - 1:SUBLANES], (SUBLANES, ns))
        ci = jnp.broadcast_to(xi[SUBLANES - 1:SUBLANES], (SUBLANES, ns))
        xs_ref[sl, :ns] = xr
        xs_ref[sl, ns:] = xi
    cr_ref[...] = cr
    ci_ref[...] = ci
    y = _dot(xs_ref[...].astype(BF16), cbd_ref[0]) + d_ref[...] * u
    y3 = y * y * y
    y_ref[...] = y * (0.5 * (1.0 + jnp.tanh(np.float32(math.sqrt(2.0 / math.pi)) * (y + 0.044715 * y3))))


def _s5(proj, bbd, cbd, pw, d, *, batch, seq, rows, u_col0):
    t = batch * seq
    nblk = seq // rows
    ns = S5_TILE_STATES
    return pl.pallas_call(
        functools.partial(_s5_body, rows=rows),
        grid=(batch, S5_TILES, nblk),
        in_specs=[
            pl.BlockSpec((rows, LANES), lambda b, j, n: (b * nblk + n, u_col0 + j)),
            pl.BlockSpec((1, LANES, 2 * ns), lambda b, j, n: (j, 0, 0)),
            pl.BlockSpec((1, 2 * ns, LANES), lambda b, j, n: (j, 0, 0)),
            pl.BlockSpec((1, 8, SUBLANES, ns), lambda b, j, n: (j, 0, 0, 0)),
            pl.BlockSpec((1, LANES), lambda b, j, n: (0, j)),
        ],
        out_specs=pl.BlockSpec((rows, LANES), lambda b, j, n: (b * nblk + n, j)),
        out_shape=jax.ShapeDtypeStruct((t, S5_WIDTH), F32),
        scratch_shapes=[pltpu.VMEM((rows, 2 * ns), F32),
                        pltpu.VMEM((SUBLANES, ns), F32),
                        pltpu.VMEM((SUBLANES, ns), F32)],
        compiler_params=_params("arbitrary", "arbitrary", "arbitrary"),
        name="s5",
    )(proj, bbd, cbd, pw, d)


def _s5_tables(lam_re, lam_im, b_re, b_im, c_re, c_im, log_step):
    step = jnp.exp(log_step)[:, None]
    mag = jnp.exp(lam_re * step)
    a_re = mag * jnp.cos(lam_im * step)
    a_im = mag * jnp.sin(lam_im * step)
    inv_den = 1.0 / (lam_re * lam_re + lam_im * lam_im)
    f_re = ((a_re - 1.0) * lam_re + a_im * lam_im) * inv_den
    f_im = (a_im * lam_re - (a_re - 1.0) * lam_im) * inv_den
    bb_re = f_re[..., None] * b_re - f_im[..., None] * b_im
    bb_im = f_re[..., None] * b_im + f_im[..., None] * b_re
    tg, nt = S5_TILE_GROUPS, S5_TILES
    eye = jnp.eye(tg, dtype=F32)

    def in_table(bb):
        bb = bb.reshape(nt, tg, S5_STATE, S5_GROUP)
        return jnp.einsum('jgph,gk->jghkp', bb, eye).reshape(nt, LANES, S5_TILE_STATES)

    def out_table(cc):
        cc = cc.reshape(nt, tg, S5_GROUP, S5_STATE)
        return jnp.einsum('jghp,gk->jgpkh', cc, eye).reshape(nt, S5_TILE_STATES, LANES)

    bbd = jnp.concatenate([in_table(bb_re), in_table(bb_im)], axis=2).astype(BF16)
    cbd = jnp.concatenate([out_table(c_re), -out_table(c_im)], axis=1).astype(BF16)

    def cmul(x, y):
        return x[0] * y[0] - x[1] * y[1], x[0] * y[1] + x[1] * y[0]

    a1 = (a_re.reshape(nt, 1, S5_TILE_STATES), a_im.reshape(nt, 1, S5_TILE_STATES))
    powers = [a1]
    for _ in range(SUBLANES - 1):
        powers.append(cmul(powers[-1], a1))
    row = jnp.arange(SUBLANES)[None, :, None]
    tabs = []
    for s in (1, 2, 4):
        pr, pi = powers[s - 1]
        tabs.append(jnp.where(row >= s, pr, 0.0))
        tabs.append(jnp.where(row >= s, pi, 0.0))
    tabs.append(jnp.concatenate([p[0] for p in powers], axis=1))
    tabs.append(jnp.concatenate([p[1] for p in powers], axis=1))
    pw = jnp.stack([jnp.broadcast_to(tb, (nt, SUBLANES, S5_TILE_STATES)) for tb in tabs], axis=1)
    return bbd, cbd, pw.astype(F32)


def _mix0_out_body(og_ref, ys_ref, gw_ref, gb_ref, w1_ref, w2_ref, x_ref, g_ref, b_ref, o_ref):
    ys = ys_ref[...]
    gate = jax.nn.sigmoid(_dot(ys.astype(BF16), gw_ref[...]) + gb_ref[...])
    os5 = (ys * gate).astype(BF16)
    mix = _dot(og_ref[...], w1_ref[...]) + _dot(os5, w2_ref[...])
    o_ref[...] = _layer_norm(ALPHA * x_ref[...] + mix, g_ref[...], b_ref[...])


def _mix0_out(og, ys, gw, gb, w1, w2, x, g, b, *, tm):
    t, d = x.shape
    const = lambda shape: pl.BlockSpec(shape, lambda i: (0, 0))
    return pl.pallas_call(
        _mix0_out_body,
        grid=(t // tm,),
        in_specs=[
            pl.BlockSpec((tm, GLA_VW), lambda i: (i, 0)),
            pl.BlockSpec((tm, S5_WIDTH), lambda i: (i, 0)),
            const((S5_WIDTH, S5_WIDTH)), const((1, S5_WIDTH)),
            const((GLA_VW, d)), const((S5_WIDTH, d)),
            pl.BlockSpec((tm, d), lambda i: (i, 0)),
            const((1, d)), const((1, d)),
        ],
        out_specs=pl.BlockSpec((tm, d), lambda i: (i, 0)),
        out_shape=jax.ShapeDtypeStruct((t, d), F32),
        compiler_params=_params("arbitrary"),
        name="mix0_out",
    )(og, ys, gw, gb, w1, w2, x, g, b)


GDN_REP = GDN_V_HEADS // GDN_QK_HEADS


def _block_diag2(m, second):
    zero = jnp.zeros_like(m)
    return jnp.concatenate([jnp.where(second, zero, m), jnp.where(second, m, zero)], axis=0)


def _gdn_body(q_ref, k_ref, v_ref, z_ref, bg_ref, gc_ref, gr_ref, ng_ref, o_ref, st_ref, *, rows, heads):
    hg = pl.program_id(1)
    c2 = 2 * CHUNK
    vw = GDN_REP * GDN_DV

    @pl.when(pl.program_id(2) == 0)
    def _():
        st_ref[...] = jnp.zeros_like(st_ref)

    lane2 = lax.broadcasted_iota(jnp.int32, (CHUNK, c2), 1)
    row2 = lax.broadcasted_iota(jnp.int32, (CHUNK, c2), 0)
    second = lane2 >= CHUNK
    pos = jnp.where(second, lane2 - CHUNK, lane2)
    incl2 = pos <= row2
    strict2 = pos < row2
    eye2 = pos == row2
    lane_bg = lax.broadcasted_iota(jnp.int32, (CHUNK, LANES), 1)
    second_v = lax.broadcasted_iota(jnp.int32, (CHUNK, vw), 1) >= GDN_DV
    st_r = lax.broadcasted_iota(jnp.int32, (GDN_REP * GDN_DK, vw), 0) >= GDN_DK
    st_c = lax.broadcasted_iota(jnp.int32, (GDN_REP * GDN_DK, vw), 1) >= GDN_DV
    same_head = st_r == st_c
    ng = ng_ref[...]
    zero_rhs = jnp.zeros((CHUNK, GDN_DV + GDN_DK), BF16)

    def chunk(c, carry):
        r0 = pl.multiple_of(c * CHUNK, CHUNK)
        bg = bg_ref[pl.ds(r0, CHUNK), :]
        gcb = gc_ref[pl.ds(r0, CHUNK), :]

        def column(arr, idx):
            return jnp.sum(jnp.where(lane_bg == idx, arr, 0.0), axis=-1, keepdims=True)

        hs = range(heads)
        ks, qs, prods, cols, decays, xs, ps, pbds = [], [], [], [], [], [], [], []
        for g in hs:
            k = k_ref[pl.ds(r0, CHUNK), g * GDN_DK:(g + 1) * GDN_DK]
            q = q_ref[pl.ds(r0, CHUNK), g * GDN_DK:(g + 1) * GDN_DK]
            kb = k.astype(BF16)
            ks.append(k)
            qs.append(q)
            prods.append(_dot_nt(jnp.concatenate([kb, q.astype(BF16)], axis=0),
                                 jnp.concatenate([kb, kb], axis=0)))
        for g in hs:
            hq = hg * heads + g
            b0 = column(bg, GDN_REP * hq)
            b1 = column(bg, GDN_REP * hq + 1)
            g0 = column(gcb, GDN_V_HEADS + GDN_REP * hq)
            g1 = column(gcb, GDN_V_HEADS + GDN_REP * hq + 1)
            cols.append((b0, b1, g0, g1, g0[CHUNK - 1:CHUNK, :], g1[CHUNK - 1:CHUNK, :]))
            gc_row = jnp.where(second[0:1], gr_ref[c, pl.ds(GDN_REP * hq + 1, 1), :],
                               gr_ref[c, pl.ds(GDN_REP * hq, 1), :])
            decay2 = jnp.exp(jnp.where(incl2, jnp.where(second, g1, g0) - gc_row, -jnp.inf))
            a2 = jnp.where(strict2, prods[g][:CHUNK] * jnp.where(second, b1, b0) * decay2, 0.0)
            decays.append(decay2)
            xs.append(jnp.where(eye2, 1.0, 0.0) - a2)
            ps.append(a2.astype(BF16))
            pbds.append(_block_diag2(ps[g], second))
        power = 1
        while 2 * power < CHUNK:
            for g in hs:
                ps[g] = _dot(ps[g], pbds[g]).astype(BF16)
                pbds[g] = _block_diag2(ps[g], second)
            for g in hs:
                xs[g] = xs[g] + _dot(xs[g].astype(BF16), pbds[g])
            power *= 2
        sols, exps = [], []
        for g in hs:
            b0, b1, g0, g1, gl0, gl1 = cols[g]
            e0 = jnp.exp(g0)
            e1 = jnp.exp(g1)
            exps.append((e0, e1))
            v2 = v_ref[pl.ds(r0, CHUNK), g * vw:(g + 1) * vw]
            rhs0 = jnp.concatenate([v2[:, :GDN_DV] * b0, (ks[g] * b0) * e0], axis=1).astype(BF16)
            rhs1 = jnp.concatenate([v2[:, GDN_DV:] * b1, (ks[g] * b1) * e1], axis=1).astype(BF16)
            rhs_bd = jnp.concatenate([jnp.concatenate([rhs0, zero_rhs], axis=1),
                                      jnp.concatenate([zero_rhs, rhs1], axis=1)], axis=0)
            sols.append(_dot(xs[g].astype(BF16), rhs_bd))
        sts, wss = [], []
        for g in hs:
            sol = sols[g]
            e0, e1 = exps[g]
            w2 = jnp.concatenate([sol[:, GDN_DV:vw], sol[:, vw + GDN_DV:]], axis=1)
            qd2 = jnp.concatenate([qs[g] * e0, qs[g] * e1], axis=1)
            sts.append(st_ref[g])
            wss.append(_dot(jnp.concatenate([w2, qd2], axis=0).astype(BF16), sts[g].astype(BF16)))
        new_states, results = [], []
        for g in hs:
            b0, b1, g0, g1, gl0, gl1 = cols[g]
            sol = sols[g]
            u2 = jnp.concatenate([sol[:, :GDN_DV], sol[:, vw:vw + GDN_DV]], axis=1)
            v_new = (u2 - wss[g][:CHUNK]).astype(BF16)
            attn2 = (prods[g][CHUNK:] * decays[g]).astype(BF16)
            o2 = wss[g][CHUNK:] + _dot(attn2, _block_diag2(v_new, second_v))
            k_end2 = jnp.concatenate([ks[g] * jnp.exp(gl0 - g0), ks[g] * jnp.exp(gl1 - g1)],
                                     axis=1).astype(BF16)
            upd = _dot_tn(k_end2, v_new)
            new_states.append(jnp.where(st_r, jnp.exp(gl1), jnp.exp(gl0)) * sts[g]
                              + jnp.where(same_head, upd, 0.0))
            outs = []
            for hh in range(GDN_REP):
                o = o2[:, hh * GDN_DV:(hh + 1) * GDN_DV]
                outs.append(o * lax.rsqrt(jnp.mean(o * o, axis=-1, keepdims=True) + RMS_EPS) * ng)
            z2 = z_ref[pl.ds(r0, CHUNK), g * vw:(g + 1) * vw]
            results.append((jnp.concatenate(outs, axis=1) * _silu(z2)).astype(BF16))
        for g in hs:
            st_ref[g] = new_states[g]
            o_ref[pl.ds(r0, CHUNK), g * vw:(g + 1) * vw] = results[g]
        return carry

    lax.fori_loop(0, rows // CHUNK, chunk, 0)


def _gdn(qk, v, z, bg, gc, gc_rows, ng, *, batch, seq, rows, heads):
    t = batch * seq
    nblk = seq // rows
    cpb = rows // CHUNK
    row = lambda b, h, n: b * nblk + n
    vw = heads * GDN_REP * GDN_DV
    kw = heads * GDN_DK
    return pl.pallas_call(
        functools.partial(_gdn_body, rows=rows, heads=heads),
        grid=(batch, GDN_QK_HEADS // heads, nblk),
        in_specs=[
            pl.BlockSpec((rows, kw), lambda b, h, n: (row(b, h, n), h)),
            pl.BlockSpec((rows, kw), lambda b, h, n: (row(b, h, n), GDN_QK_HEADS // heads + h)),
            pl.BlockSpec((rows, vw), lambda b, h, n: (row(b, h, n), h)),
            pl.BlockSpec((rows, vw), lambda b, h, n: (row(b, h, n), h)),
            pl.BlockSpec((rows, LANES), lambda b, h, n: (row(b, h, n), 0)),
            pl.BlockSpec((rows, LANES), lambda b, h, n: (row(b, h, n), 0)),
            pl.BlockSpec((cpb, GDN_V_HEADS, GDN_REP * CHUNK), lambda b, h, n: (row(b, h, n), 0, 0)),
            pl.BlockSpec((1, GDN_DV), lambda b, h, n: (0, 0)),
        ],
        out_specs=pl.BlockSpec((rows, vw), lambda b, h, n: (row(b, h, n), h)),
        out_shape=jax.ShapeDtypeStruct((t, GDN_VW), BF16),
        scratch_shapes=[pltpu.VMEM((heads, GDN_REP * GDN_DK, GDN_REP * GDN_DV), F32)],
        compiler_params=_params("arbitrary", "arbitrary", "arbitrary"),
        name="gdn",
    )(qk, qk, v, z, bg, gc, gc_rows, ng)


def _out_ln_body(a_ref, w_ref, x_ref, g_ref, b_ref, o_ref):
    mix = _dot(a_ref[...], w_ref[...])
    o_ref[...] = _layer_norm(ALPHA * x_ref[...] + mix, g_ref[...], b_ref[...])


def _out_ln(a, w, x, g, b, *, tm):
    t, d = x.shape
    kdim = a.shape[1]
    const = lambda shape: pl.BlockSpec(shape, lambda i: (0, 0), pipeline_mode=pl.Buffered(1))
    return pl.pallas_call(
        _out_ln_body,
        grid=(t // tm,),
        in_specs=[
            pl.BlockSpec((tm, kdim), lambda i: (i, 0)),
            const((kdim, d)),
            pl.BlockSpec((tm, d), lambda i: (i, 0)),
            const((1, d)), const((1, d)),
        ],
        out_specs=pl.BlockSpec((tm, d), lambda i: (i, 0)),
        out_shape=jax.ShapeDtypeStruct((t, d), F32),
        compiler_params=_params("arbitrary"),
        name="out_ln",
    )(a, w, x, g, b)


def _tiles(batch, seq):
    tm = min(512, seq)
    return dict(tm=tm, ffn_tm=min(1024, seq), proj_tm=min(1024, seq), gla_rows=min(256, seq), s5_rows=min(512, seq), gdn_rows=min(256, seq),
                gdn_heads=16)


def _ffn(x, wg, wu, wd, layer, g, b, tm, emit_bf16):
    return _ffn_ln(x, wg, wu, wd, layer, g[None], b[None], tm=tm, tf=256, emit_bf16=emit_bf16)


def _pad_cols(w, n):
    return jnp.pad(w, ((0, 0), (0, n - w.shape[1])))


def _gla_s5_layer(x, xb, p, i, g, b, batch, seq, tl):
    w_in = p['ab_w_in']
    w_gate = _pad_cols(w_in[i, :, AB_A:AB_U], LANES).astype(BF16)
    proj = _proj(xb, jnp.swapaxes(w_in, 1, 2), tm=tl['proj_tm'], tn=1024, col0=0, n=AB_A, layer=i)
    u = _proj(xb, w_in[i, :, AB_U:], tm=tl['proj_tm'], tn=1024)
    gate = _proj(xb, w_gate, tm=tl['tm'], tn=LANES)
    wlr = jnp.pad(p['gla_w_lr'][i], ((0, LANES - GLA_RANK), (0, 0))).astype(BF16)
    o_gla = _gla(proj, gate, wlr, p['gla_b_lr'][i][None], p['gla_norm_g'][i][None],
                 batch=batch, seq=seq, rows=tl['gla_rows'])
    bbd, cbd, pw = _s5_tables(p['s5_lam_re'][i], p['s5_lam_im'][i], p['s5_b_re'][i], p['s5_b_im'][i],
                              p['s5_c_re'][i], p['s5_c_im'][i], p['s5_log_step'][i])
    ys = _s5(u, bbd, cbd, pw, p['s5_d'][i][None], batch=batch, seq=seq, rows=tl['s5_rows'], u_col0=0)
    w_out = p['ab_w_out'][i].astype(BF16)
    return _mix0_out(o_gla, ys, p['s5_glu_w'][i].astype(BF16), p['s5_glu_b'][i][None],
                     w_out[:GLA_VW], w_out[GLA_VW:], x, g[None], b[None], tm=tl['tm'])


def _gdn_layer(x, xb, p, i, g, b, batch, seq, tl):
    t = batch * seq
    w_in = p['gdn_w_in']
    conv_w = p['gdn_conv_w'][i]
    tm = tl['tm']
    w_t = jnp.swapaxes(w_in, 1, 2)
    qk = _conv_proj(xb, w_t, conv_w, tm=tl['proj_tm'], tn=1024, seq=seq, col0=0, n=2 * GDN_KW, layer=i,
                    l2_heads=True, n_scaled_tiles=GDN_KW // 1024, scale=GDN_DK ** -0.5)
    v = _conv_proj(xb, w_t, conv_w, tm=tl['proj_tm'], tn=1024, seq=seq, col0=2 * GDN_KW, n=GDN_VW,
                   layer=i)
    z = _proj(xb, w_t, tm=tl['proj_tm'], tn=1024, col0=GDN_Z, n=GDN_VW, layer=i)
    w_gate = _pad_cols(w_in[i, :, GDN_B:GDN_IN], LANES).astype(BF16)
    lane_pad = lambda a: jnp.pad(a, (GDN_V_HEADS, LANES - 2 * GDN_V_HEADS))[None]
    bg, gc, gr = _gate_proj(xb, w_gate, lane_pad(p['gdn_a_log'][i]), lane_pad(p['gdn_dt_bias'][i]), tm=tm)
    gc_rows = gr.reshape(GDN_V_HEADS, t // CHUNK, LANES).transpose(1, 0, 2)
    o = _gdn(qk, v, z, bg, gc, gc_rows, p['gdn_norm_g'][i][None], batch=batch, seq=seq,
             rows=tl['gdn_rows'], heads=tl['gdn_heads'])
    return _out_ln(o, p['gdn_w_out'][i].astype(BF16), x, g[None], b[None], tm=min(256, tm))


def kernel(x, ffn_a_gate, ffn_a_up, ffn_a_down, ffn_b_gate, ffn_b_up, ffn_b_down, ln_g, ln_b, ab_w_in, gla_w_lr, gla_b_lr, gla_norm_g, s5_lam_re, s5_lam_im, s5_b_re, s5_b_im, s5_c_re, s5_c_im, s5_d, s5_log_step, s5_glu_w, s5_glu_b, ab_w_out, gdn_w_in, gdn_conv_w, gdn_a_log, gdn_dt_bias, gdn_norm_g, gdn_w_out):
    batch, seq, d = x.shape
    assert d == D_MODEL and seq % CHUNK == 0
    p = dict(ab_w_in=ab_w_in, gla_w_lr=gla_w_lr, gla_b_lr=gla_b_lr, gla_norm_g=gla_norm_g,
             s5_lam_re=s5_lam_re, s5_lam_im=s5_lam_im, s5_b_re=s5_b_re, s5_b_im=s5_b_im,
             s5_c_re=s5_c_re, s5_c_im=s5_c_im, s5_d=s5_d, s5_log_step=s5_log_step,
             s5_glu_w=s5_glu_w, s5_glu_b=s5_glu_b, ab_w_out=ab_w_out, gdn_w_in=gdn_w_in,
             gdn_conv_w=gdn_conv_w, gdn_a_log=gdn_a_log, gdn_dt_bias=gdn_dt_bias,
             gdn_norm_g=gdn_norm_g, gdn_w_out=gdn_w_out)
    tl = _tiles(batch, seq)
    h = x.reshape(batch * seq, d).astype(F32)
    for layer in range(DEPTH):
        h, hb = _ffn(h, ffn_a_gate, ffn_a_up, ffn_a_down, layer,
                     ln_g[layer, 0], ln_b[layer, 0], tl['ffn_tm'], True)
        i = layer // 2
        mixer = _gla_s5_layer if layer % 2 == 0 else _gdn_layer
        h = mixer(h, hb, p, i, ln_g[layer, 1], ln_b[layer, 1], batch, seq, tl)
        (h,) = _ffn(h, ffn_b_gate, ffn_b_up, ffn_b_down, layer,
                    ln_g[layer, 2], ln_b[layer, 2], tl['ffn_tm'], False)
    return h.reshape(batch, seq, d)
```

```python
import functools
import math

import jax
import jax.numpy as jnp
import numpy as np
from jax import lax
from jax.experimental import pallas as pl
from jax.experimental.pallas import tpu as pltpu

F32 = jnp.float32
BF16 = jnp.bfloat16

LANES = 128
SUBLANES = 8
VMEM_LIMIT_BYTES = 60 * 1024 * 1024
LN_ROWS = 64
CONV_SUB = 256

D_MODEL = 2048
DEPTH = 2
ALPHA = (2.0 * DEPTH) ** 0.25
MACARON = 0.5
LN_EPS = 1e-5
RMS_EPS = 1e-6
L2_EPS = 1e-6
D_FF = 5504

GLA_HEADS = 4
GLA_DK = 128
GLA_DV = 256
GLA_RANK = 16
GLA_TAU = 16.0
CHUNK = 64
GLA_KW = GLA_HEADS * GLA_DK
GLA_VW = GLA_HEADS * GLA_DV

S5_WIDTH = 1024
S5_GROUP = 16
S5_GROUPS = 64
S5_STATE = 64
S5_TILE_GROUPS = LANES // S5_GROUP
S5_TILES = S5_WIDTH // LANES
S5_TILE_STATES = S5_TILE_GROUPS * S5_STATE

AB_K = GLA_KW
AB_V = AB_K + GLA_KW
AB_G = AB_V + GLA_VW
AB_A = AB_G + GLA_VW
AB_U = AB_A + GLA_RANK
AB_IN = AB_U + S5_WIDTH

GDN_QK_HEADS = 16
GDN_V_HEADS = 32
GDN_DK = 128
GDN_DV = 128
GDN_KW = GDN_QK_HEADS * GDN_DK
GDN_VW = GDN_V_HEADS * GDN_DV
GDN_QKV = 2 * GDN_KW + GDN_VW
GDN_Z = GDN_QKV
GDN_B = GDN_Z + GDN_VW
GDN_A = GDN_B + GDN_V_HEADS
GDN_IN = GDN_A + GDN_V_HEADS


def _params(*sem):
    return pltpu.CompilerParams(dimension_semantics=sem, vmem_limit_bytes=VMEM_LIMIT_BYTES)


def _dot(a, b):
    return jnp.dot(a, b, preferred_element_type=F32)


def _dot_nt(a, b):
    return lax.dot_general(a, b, (((1,), (1,)), ((), ())), preferred_element_type=F32)


def _dot_tn(a, b):
    return lax.dot_general(a, b, (((0,), (0,)), ((), ())), preferred_element_type=F32)


def _layer_norm(y, g, b):
    mu = jnp.mean(y, axis=-1, keepdims=True)
    yc = y - mu
    var = jnp.mean(yc * yc, axis=-1, keepdims=True)
    return yc * lax.rsqrt(var + LN_EPS) * g + b


def _silu(v):
    return v * jax.nn.sigmoid(v)


def _softplus(v):
    return jnp.maximum(v, 0.0) + jnp.log1p(jnp.exp(-jnp.abs(v)))


def _split3(v):
    hi = v.astype(BF16)
    r1 = v - hi.astype(F32)
    mid = r1.astype(BF16)
    lo = (r1 - mid.astype(F32)).astype(BF16)
    return hi, mid, lo


def _tril_mask(n, strict=False):
    r = lax.broadcasted_iota(jnp.int32, (n, n), 0)
    c = lax.broadcasted_iota(jnp.int32, (n, n), 1)
    return (c < r) if strict else (c <= r)


def _chunk_cumsum(v, rows):
    r = lax.broadcasted_iota(jnp.int32, (rows, rows), 0)
    c = lax.broadcasted_iota(jnp.int32, (rows, rows), 1)
    tri = ((c <= r) & ((c // CHUNK) == (r // CHUNK))).astype(BF16)
    hi, mid, lo = _split3(v)
    return _dot(tri, hi) + _dot(tri, mid) + _dot(tri, lo)


def _ffn_body(x_ref, wg_ref, wu_ref, wd_ref, wgt_ref, wut_ref, wdt_ref, g_ref, b_ref, o_ref, *rest, nj):
    ob_ref, xb_ref = rest if len(rest) == 2 else (None, rest[0])
    j = pl.program_id(1)

    @pl.when(j == 0)
    def _():
        xb_ref[...] = x_ref[...].astype(BF16)
        o_ref[...] = jnp.zeros_like(o_ref)

    def down(wg, wu, wd):
        xb = xb_ref[...]
        h = (_silu(_dot(xb, wg.astype(BF16))) * _dot(xb, wu.astype(BF16))).astype(BF16)
        return _dot(h, wd.astype(BF16))

    @pl.when(j < nj - 1)
    def _():
        o_ref[...] += down(wg_ref[...], wu_ref[...], wd_ref[...])

    @pl.when(j == nj - 1)
    def _():
        o_ref[...] += down(wgt_ref[...], wut_ref[...], wdt_ref[...])
        g = g_ref[...]
        b = b_ref[...]

        def rows(r, carry):
            sl = pl.ds(pl.multiple_of(r * LN_ROWS, LN_ROWS), LN_ROWS)
            yn = _layer_norm(ALPHA * x_ref[sl, :] + MACARON * o_ref[sl, :], g, b)
            o_ref[sl, :] = yn
            if ob_ref is not None:
                ob_ref[sl, :] = yn.astype(BF16)
            return carry

        lax.fori_loop(0, o_ref.shape[0] // LN_ROWS, rows, 0)


def _ffn_ln(x, wg, wu, wd, layer, g, b, *, tm, tf, emit_bf16):
    t, d = x.shape
    f = wg.shape[2]
    n_full = f // tf
    f_tail = f - n_full * tf
    assert f_tail > 0 and f_tail % LANES == 0
    nj = n_full + 1
    last = n_full - 1
    const = lambda shape: pl.BlockSpec(shape, lambda i, j: (0, 0), pipeline_mode=pl.Buffered(1))
    tile = pl.BlockSpec((tm, d), lambda i, j: (i, 0))
    out_specs = [tile, tile] if emit_bf16 else [tile]
    out_shape = [jax.ShapeDtypeStruct((t, d), F32), jax.ShapeDtypeStruct((t, d), BF16)][:len(out_specs)]
    return pl.pallas_call(
        functools.partial(_ffn_body, nj=nj),
        grid=(t // tm, nj),
        in_specs=[
            pl.BlockSpec((tm, d), lambda i, j: (i, 0), pipeline_mode=pl.Buffered(1 if emit_bf16 else 2)),
            pl.BlockSpec((None, d, tf), lambda i, j: (layer, 0, jnp.minimum(j, last))),
            pl.BlockSpec((None, d, tf), lambda i, j: (layer, 0, jnp.minimum(j, last))),
            pl.BlockSpec((None, tf, d), lambda i, j: (layer, jnp.minimum(j, last), 0)),
            const((d, f_tail)), const((d, f_tail)), const((f_tail, d)),
            const((1, d)), const((1, d)),
        ],
        out_specs=out_specs,
        out_shape=out_shape,
        scratch_shapes=[pltpu.VMEM((tm, d), BF16)],
        compiler_params=_params("arbitrary", "arbitrary"),
        name="ffn_ln",
    )(x, wg, wu, wd, wg[layer, :, f - f_tail:], wu[layer, :, f - f_tail:], wd[layer, f - f_tail:], g, b)


def _load_weight_tile(w_ref, wb_ref):
    w = w_ref[...]
    if w.shape != wb_ref.shape:
        w = w.T
    wb_ref[...] = w.astype(BF16)


def _proj_body(x_ref, w_ref, o_ref, wb_ref):
    @pl.when(pl.program_id(1) == 0)
    def _():
        _load_weight_tile(w_ref, wb_ref)

    o_ref[...] = _dot(x_ref[...], wb_ref[...])


def _weight_spec(w, k, tn, j0, layer):
    if w.ndim == 3:
        return pl.BlockSpec((None, tn, k), lambda j, i: (layer, j0 + j, 0))
    return pl.BlockSpec((k, tn), lambda j, i: (0, j0 + j))


def _proj(xb, w, *, tm, tn, col0=0, n=None, layer=0):
    t, k = xb.shape
    n = (w.shape[1] if w.ndim == 3 else w.shape[-1]) if n is None else n
    assert col0 % tn == 0 and n % tn == 0
    j0 = col0 // tn
    return pl.pallas_call(
        _proj_body,
        grid=(n // tn, t // tm),
        in_specs=[
            pl.BlockSpec((tm, k), lambda j, i: (i, 0)),
            _weight_spec(w, k, tn, j0, layer),
        ],
        out_specs=pl.BlockSpec((tm, tn), lambda j, i: (i, j)),
        out_shape=jax.ShapeDtypeStruct((t, n), F32),
        scratch_shapes=[pltpu.VMEM((k, tn), BF16)],
        compiler_params=_params("arbitrary", "arbitrary"),
        name="proj",
    )(xb, w)


def _conv_proj_body(x_ref, w_ref, cw_ref, o_ref, wb_ref, stage_ref, *, tm, tn, tiles_per_seq,
                    l2_heads, n_scaled_tiles, scale):
    jt = pl.program_id(0)
    i = pl.program_id(1)

    @pl.when(i == 0)
    def _():
        _load_weight_tile(w_ref, wb_ref)

    @pl.when(i % tiles_per_seq == 0)
    def _():
        stage_ref[0:SUBLANES, :] = jnp.zeros((SUBLANES, tn), F32)

    if l2_heads:
        fac = jnp.where(jt < n_scaled_tiles, jnp.float32(scale), jnp.float32(1.0))

    def post(v):
        v = _silu(v)
        if not l2_heads:
            return v
        out = []
        for h in range(CONV_SUB // LANES):
            vh = v[:, h * LANES:(h + 1) * LANES]
            ss = jnp.sum(vh * vh, axis=-1, keepdims=True)
            out.append(vh * lax.rsqrt(ss + L2_EPS) * fac)
        return jnp.concatenate(out, axis=1)

    subs = [slice(s * CONV_SUB, (s + 1) * CONV_SUB) for s in range(tn // CONV_SUB)]

    x = x_ref[...]
    raws = [_dot(x, wb_ref[:, cols]) for cols in subs]
    for cols, raw in zip(subs, raws):
        cw = cw_ref[:, cols]
        stage_ref[SUBLANES:SUBLANES + tm, cols] = raw
        acc = raw * cw[3:4]
        for d in (1, 2, 3):
            acc = acc + stage_ref[SUBLANES - d:SUBLANES - d + tm, cols] * cw[3 - d:4 - d]
        stage_ref[0:SUBLANES, cols] = raw[tm - SUBLANES:tm]
        o_ref[:, cols] = post(acc)


def _conv_proj(xb, w, cw, *, tm, tn, seq, col0, n, layer=0, l2_heads=False, n_scaled_tiles=0, scale=1.0):
    t, k = xb.shape
    nt = t // tm
    assert col0 % tn == 0 and n % tn == 0
    j0 = col0 // tn
    body = functools.partial(_conv_proj_body, tm=tm, tn=tn, tiles_per_seq=seq // tm,
                             l2_heads=l2_heads, n_scaled_tiles=n_scaled_tiles, scale=scale)
    return pl.pallas_call(
        body,
        grid=(n // tn, nt),
        in_specs=[
            pl.BlockSpec((tm, k), lambda j, i: (i, 0)),
            _weight_spec(w, k, tn, j0, layer),
            pl.BlockSpec((4, tn), lambda j, i: (0, j0 + j)),
        ],
        out_specs=pl.BlockSpec((tm, tn), lambda j, i: (i, j)),
        out_shape=jax.ShapeDtypeStruct((t, n), F32),
        scratch_shapes=[pltpu.VMEM((k, tn), BF16),
                        pltpu.VMEM((SUBLANES + tm, tn), F32)],
        compiler_params=_params("arbitrary", "arbitrary"),
        name="conv_proj",
    )(xb, w, cw)


def _gate_proj_body(x_ref, w_ref, alog_ref, dtb_ref, bg_ref, gc_ref, gr_ref, *, tm):
    raw = _dot(x_ref[...], w_ref[...])
    lane = lax.broadcasted_iota(jnp.int32, raw.shape, 1)
    beta = jax.nn.sigmoid(raw)
    g = -jnp.exp(alog_ref[...]) * _softplus(raw + dtb_ref[...])
    g = jnp.where((lane >= GDN_V_HEADS) & (lane < 2 * GDN_V_HEADS), g, 0.0)
    bg_ref[...] = jnp.where(lane < GDN_V_HEADS, beta, g)
    gc = _chunk_cumsum(g, tm)
    gc_ref[...] = gc
    width = (tm // CHUNK) * LANES
    t_idx = lax.broadcasted_iota(jnp.int32, (tm, width), 0)
    col = lax.broadcasted_iota(jnp.int32, (tm, width), 1)
    sel = (t_idx == (col // LANES) * CHUNK + (col % CHUNK)).astype(BF16)
    hi, mid, lo = _split3(gc)
    rows = _dot_tn(hi, sel) + _dot_tn(mid, sel) + _dot_tn(lo, sel)
    gr_ref[...] = rows[GDN_V_HEADS:2 * GDN_V_HEADS]


def _gate_proj(xb, w, alog, dtb, *, tm):
    t, k = xb.shape
    return pl.pallas_call(
        functools.partial(_gate_proj_body, tm=tm),
        grid=(t // tm,),
        in_specs=[
            pl.BlockSpec((tm, k), lambda i: (i, 0)),
            pl.BlockSpec((k, LANES), lambda i: (0, 0)),
            pl.BlockSpec((1, LANES), lambda i: (0, 0)),
            pl.BlockSpec((1, LANES), lambda i: (0, 0)),
        ],
        out_specs=[pl.BlockSpec((tm, LANES), lambda i: (i, 0))] * 2
        + [pl.BlockSpec((GDN_V_HEADS, (tm // CHUNK) * LANES), lambda i: (0, i))],
        out_shape=[jax.ShapeDtypeStruct((t, LANES), F32)] * 2
        + [jax.ShapeDtypeStruct((GDN_V_HEADS, (t // CHUNK) * LANES), F32)],
        compiler_params=_params("arbitrary"),
        name="gate_proj",
    )(xb, w, alog, dtb)


def _gla_body(q_ref, k_ref, v_ref, go_ref, a_ref, wlr_ref, blr_ref, ng_ref, o_ref, st_ref, *, rows):
    @pl.when(pl.program_id(1) == 0)
    def _():
        st_ref[...] = jnp.zeros_like(st_ref)

    causal = _tril_mask(CHUNK)
    ng = ng_ref[...]
    heads = range(GLA_HEADS)
    chunks = range(rows // CHUNK)
    head = lambda a, h: a[:, h * GLA_DK:(h + 1) * GLA_DK]

    z = _dot(a_ref[...].astype(BF16), wlr_ref[...]) + blr_ref[...]
    b = _chunk_cumsum(-_softplus(-z) / GLA_TAU, rows)
    k = k_ref[...]
    q_dec = (q_ref[...] * (GLA_DK ** -0.5) * jnp.exp(b)).astype(BF16)
    k_inv = (k * jnp.exp(-b)).astype(BF16)
    rows_of = lambda a, c: a[c * CHUNK:(c + 1) * CHUNK]
    b_last = [rows_of(b, c)[CHUNK - 1:CHUNK] for c in chunks]
    k_end = [(rows_of(k, c) * jnp.exp(b_last[c] - rows_of(b, c))).astype(BF16) for c in chunks]
    scores = [[jnp.where(causal, _dot_nt(head(rows_of(q_dec, c), h), head(rows_of(k_inv, c), h)),
                         0.0).astype(BF16) for h in heads] for c in chunks]
    for c in chunks:
        sl = slice(c * CHUNK, (c + 1) * CHUNK)
        sts = [st_ref[h] for h in heads]
        vs = [v_ref[sl, h * GLA_DV:(h + 1) * GLA_DV].astype(BF16) for h in heads]
        qd = rows_of(q_dec, c)
        outs = [_dot(scores[c][h], vs[h]) + _dot_nt(head(qd, h), sts[h].astype(BF16)) for h in heads]
        decay = jnp.exp(b_last[c])
        new = [sts[h] * head(decay, h) + _dot_tn(vs[h], head(k_end[c], h)) for h in heads]
        for h in heads:
            st_ref[h] = new[h]
        normed = [o * lax.rsqrt(jnp.mean(o * o, axis=-1, keepdims=True) + RMS_EPS) * ng for o in outs]
        o_ref[sl, :] = (jnp.concatenate(normed, axis=1) * _silu(go_ref[sl, :])).astype(BF16)


def _gla(proj, gate, wlr, blr, ng, *, batch, seq, rows):
    t = batch * seq
    nblk = seq // rows
    row = lambda b, n: b * nblk + n
    return pl.pallas_call(
        functools.partial(_gla_body, rows=rows),
        grid=(batch, nblk),
        in_specs=[
            pl.BlockSpec((rows, GLA_KW), lambda b, n: (row(b, n), 0)),
            pl.BlockSpec((rows, GLA_KW), lambda b, n: (row(b, n), 1)),
            pl.BlockSpec((rows, GLA_VW), lambda b, n: (row(b, n), AB_V // GLA_VW)),
            pl.BlockSpec((rows, GLA_VW), lambda b, n: (row(b, n), AB_G // GLA_VW)),
            pl.BlockSpec((rows, LANES), lambda b, n: (row(b, n), 0)),
            pl.BlockSpec((LANES, GLA_KW), lambda b, n: (0, 0)),
            pl.BlockSpec((1, GLA_KW), lambda b, n: (0, 0)),
            pl.BlockSpec((1, GLA_DV), lambda b, n: (0, 0)),
        ],
        out_specs=pl.BlockSpec((rows, GLA_VW), lambda b, n: (row(b, n), 0)),
        out_shape=jax.ShapeDtypeStruct((t, GLA_VW), BF16),
        scratch_shapes=[pltpu.VMEM((GLA_HEADS, GLA_DV, GLA_DK), F32)],
        compiler_params=_params("arbitrary", "arbitrary"),
        name="gla",
    )(proj, proj, proj, proj, gate, wlr, blr, ng)


def _s5_body(u_ref, bbd_ref, cbd_ref, pw_ref, d_ref, y_ref, xs_ref, cr_ref, ci_ref, *, rows):
    @pl.when(pl.program_id(2) == 0)
    def _():
        cr_ref[...] = jnp.zeros_like(cr_ref)
        ci_ref[...] = jnp.zeros_like(ci_ref)

    ns = S5_TILE_STATES
    u = u_ref[...]
    bu = _dot(u.astype(BF16), bbd_ref[0])
    levels = [(pw_ref[0, 2 * l], pw_ref[0, 2 * l + 1], 1 << l) for l in range(3)]
    qr, qi = pw_ref[0, 6], pw_ref[0, 7]
    cr = cr_ref[...]
    ci = ci_ref[...]
    for blk in range(rows // SUBLANES):
        sl = slice(blk * SUBLANES, (blk + 1) * SUBLANES)
        xr = bu[sl, :ns]
        xi = bu[sl, ns:]
        for pr, pi, s in levels:
            sr = pltpu.roll(xr, s, 0)
            si = pltpu.roll(xi, s, 0)
            xr, xi = xr + (pr * sr - pi * si), xi + (pr * si + pi * sr)
        xr, xi = xr + (qr * cr - qi * ci), xi + (qr * ci + qi * cr)
        cr = jnp.broadcast_to(xr[SUBLANES - 1:SUBLANES], (SUBLANES, ns))
        ci = jnp.broadcast_to(xi[SUBLANES - 1:SUBLANES], (SUBLANES, ns))
        xs_ref[sl, :ns] = xr
        xs_ref[sl, ns:] = xi
    cr_ref[...] = cr
    ci_ref[...] = ci
    y = _dot(xs_ref[...].astype(BF16), cbd_ref[0]) + d_ref[...] * u
    y3 = y * y * y
    y_ref[...] = y * (0.5 * (1.0 + jnp.tanh(np.float32(math.sqrt(2.0 / math.pi)) * (y + 0.044715 * y3))))


def _s5(proj, bbd, cbd, pw, d, *, batch, seq, rows, u_col0):
    t = batch * seq
    nblk = seq // rows
    ns = S5_TILE_STATES
    return pl.pallas_call(
        functools.partial(_s5_body, rows=rows),
        grid=(batch, S5_TILES, nblk),
        in_specs=[
            pl.BlockSpec((rows, LANES), lambda b, j, n: (b * nblk + n, u_col0 + j)),
            pl.BlockSpec((1, LANES, 2 * ns), lambda b, j, n: (j, 0, 0)),
            pl.BlockSpec((1, 2 * ns, LANES), lambda b, j, n: (j, 0, 0)),
            pl.BlockSpec((1, 8, SUBLANES, ns), lambda b, j, n: (j, 0, 0, 0)),
            pl.BlockSpec((1, LANES), lambda b, j, n: (0, j)),
        ],
        out_specs=pl.BlockSpec((rows, LANES), lambda b, j, n: (b * nblk + n, j)),
        out_shape=jax.ShapeDtypeStruct((t, S5_WIDTH), F32),
        scratch_shapes=[pltpu.VMEM((rows, 2 * ns), F32),
                        pltpu.VMEM((SUBLANES, ns), F32),
                        pltpu.VMEM((SUBLANES, ns), F32)],
        compiler_params=_params("arbitrary", "arbitrary", "arbitrary"),
        name="s5",
    )(proj, bbd, cbd, pw, d)


def _s5_tables(lam_re, lam_im, b_re, b_im, c_re, c_im, log_step):
    step = jnp.exp(log_step)[:, None]
    mag = jnp.exp(lam_re * step)
    a_re = mag * jnp.cos(lam_im * step)
    a_im = mag * jnp.sin(lam_im * step)
    inv_den = 1.0 / (lam_re * lam_re + lam_im * lam_im)
    f_re = ((a_re - 1.0) * lam_re + a_im * lam_im) * inv_den
    f_im = (a_im * lam_re - (a_re - 1.0) * lam_im) * inv_den
    bb_re = f_re[..., None] * b_re - f_im[..., None] * b_im
    bb_im = f_re[..., None] * b_im + f_im[..., None] * b_re
    tg, nt = S5_TILE_GROUPS, S5_TILES
    eye = jnp.eye(tg, dtype=F32)

    def in_table(bb):
        bb = bb.reshape(nt, tg, S5_STATE, S5_GROUP)
        return jnp.einsum('jgph,gk->jghkp', bb, eye).reshape(nt, LANES, S5_TILE_STATES)

    def out_table(cc):
        cc = cc.reshape(nt, tg, S5_GROUP, S5_STATE)
        return jnp.einsum('jghp,gk->jgpkh', cc, eye).reshape(nt, S5_TILE_STATES, LANES)

    bbd = jnp.concatenate([in_table(bb_re), in_table(bb_im)], axis=2).astype(BF16)
    cbd = jnp.concatenate([out_table(c_re), -out_table(c_im)], axis=1).astype(BF16)

    def cmul(x, y):
        return x[0] * y[0] - x[1] * y[1], x[0] * y[1] + x[1] * y[0]

    a1 = (a_re.reshape(nt, 1, S5_TILE_STATES), a_im.reshape(nt, 1, S5_TILE_STATES))
    powers = [a1]
    for _ in range(SUBLANES - 1):
        powers.append(cmul(powers[-1], a1))
    row = jnp.arange(SUBLANES)[None, :, None]
    tabs = []
    for s in (1, 2, 4):
        pr, pi = powers[s - 1]
        tabs.append(jnp.where(row >= s, pr, 0.0))
        tabs.append(jnp.where(row >= s, pi, 0.0))
    tabs.append(jnp.concatenate([p[0] for p in powers], axis=1))
    tabs.append(jnp.concatenate([p[1] for p in powers], axis=1))
    pw = jnp.stack([jnp.broadcast_to(tb, (nt, SUBLANES, S5_TILE_STATES)) for tb in tabs], axis=1)
    return bbd, cbd, pw.astype(F32)


def _mix0_out_body(og_ref, ys_ref, gw_ref, gb_ref, w1_ref, w2_ref, x_ref, g_ref, b_ref, o_ref):
    ys = ys_ref[...]
    gate = jax.nn.sigmoid(_dot(ys.astype(BF16), gw_ref[...]) + gb_ref[...])
    os5 = (ys * gate).astype(BF16)
    mix = _dot(og_ref[...], w1_ref[...]) + _dot(os5, w2_ref[...])
    o_ref[...] = _layer_norm(ALPHA * x_ref[...] + mix, g_ref[...], b_ref[...])


def _mix0_out(og, ys, gw, gb, w1, w2, x, g, b, *, tm):
    t, d = x.shape
    const = lambda shape: pl.BlockSpec(shape, lambda i: (0, 0))
    return pl.pallas_call(
        _mix0_out_body,
        grid=(t // tm,),
        in_specs=[
            pl.BlockSpec((tm, GLA_VW), lambda i: (i, 0)),
            pl.BlockSpec((tm, S5_WIDTH), lambda i: (i, 0)),
            const((S5_WIDTH, S5_WIDTH)), const((1, S5_WIDTH)),
            const((GLA_VW, d)), const((S5_WIDTH, d)),
            pl.BlockSpec((tm, d), lambda i: (i, 0)),
            const((1, d)), const((1, d)),
        ],
        out_specs=pl.BlockSpec((tm, d), lambda i: (i, 0)),
        out_shape=jax.ShapeDtypeStruct((t, d), F32),
        compiler_params=_params("arbitrary"),
        name="mix0_out",
    )(og, ys, gw, gb, w1, w2, x, g, b)


GDN_REP = GDN_V_HEADS // GDN_QK_HEADS


def _block_diag2(m, second):
    zero = jnp.zeros_like(m)
    return jnp.concatenate([jnp.where(second, zero, m), jnp.where(second, m, zero)], axis=0)


def _gdn_body(q_ref, k_ref, v_ref, z_ref, bg_ref, gc_ref, gr_ref, ng_ref, o_ref, st_ref, *, rows, heads):
    hg = pl.program_id(1)
    c2 = 2 * CHUNK
    vw = GDN_REP * GDN_DV

    @pl.when(pl.program_id(2) == 0)
    def _():
        st_ref[...] = jnp.zeros_like(st_ref)

    lane2 = lax.broadcasted_iota(jnp.int32, (CHUNK, c2), 1)
    row2 = lax.broadcasted_iota(jnp.int32, (CHUNK, c2), 0)
    second = lane2 >= CHUNK
    pos = jnp.where(second, lane2 - CHUNK, lane2)
    incl2 = pos <= row2
    strict2 = pos < row2
    eye2 = pos == row2
    lane_bg = lax.broadcasted_iota(jnp.int32, (CHUNK, LANES), 1)
    second_v = lax.broadcasted_iota(jnp.int32, (CHUNK, vw), 1) >= GDN_DV
    st_r = lax.broadcasted_iota(jnp.int32, (GDN_REP * GDN_DK, vw), 0) >= GDN_DK
    st_c = lax.broadcasted_iota(jnp.int32, (GDN_REP * GDN_DK, vw), 1) >= GDN_DV
    same_head = st_r == st_c
    ng = ng_ref[...]
    zero_rhs = jnp.zeros((CHUNK, GDN_DV + GDN_DK), BF16)

    def chunk(c, carry):
        r0 = pl.multiple_of(c * CHUNK, CHUNK)
        bg = bg_ref[pl.ds(r0, CHUNK), :]
        gcb = gc_ref[pl.ds(r0, CHUNK), :]

        def column(arr, idx):
            return jnp.sum(jnp.where(lane_bg == idx, arr, 0.0), axis=-1, keepdims=True)

        hs = range(heads)
        ks, qs, prods, cols, decays, xs, ps, pbds = [], [], [], [], [], [], [], []
        for g in hs:
            k = k_ref[pl.ds(r0, CHUNK), g * GDN_DK:(g + 1) * GDN_DK]
            q = q_ref[pl.ds(r0, CHUNK), g * GDN_DK:(g + 1) * GDN_DK]
            kb = k.astype(BF16)
            ks.append(k)
            qs.append(q)
            prods.append(_dot_nt(jnp.concatenate([kb, q.astype(BF16)], axis=0),
                                 jnp.concatenate([kb, kb], axis=0)))
        for g in hs:
            hq = hg * heads + g
            b0 = column(bg, GDN_REP * hq)
            b1 = column(bg, GDN_REP * hq + 1)
            g0 = column(gcb, GDN_V_HEADS + GDN_REP * hq)
            g1 = column(gcb, GDN_V_HEADS + GDN_REP * hq + 1)
            cols.append((b0, b1, g0, g1, g0[CHUNK - 1:CHUNK, :], g1[CHUNK - 1:CHUNK, :]))
            gc_row = jnp.where(second[0:1], gr_ref[c, pl.ds(GDN_REP * hq + 1, 1), :],
                               gr_ref[c, pl.ds(GDN_REP * hq, 1), :])
            decay2 = jnp.exp(jnp.where(incl2, jnp.where(second, g1, g0) - gc_row, -jnp.inf))
            a2 = jnp.where(strict2, prods[g][:CHUNK] * jnp.where(second, b1, b0) * decay2, 0.0)
            decays.append(decay2)
            xs.append(jnp.where(eye2, 1.0, 0.0) - a2)
            ps.append(a2.astype(BF16))
            pbds.append(_block_diag2(ps[g], second))
        power = 1
        while 2 * power < CHUNK:
            for g in hs:
                ps[g] = _dot(ps[g], pbds[g]).astype(BF16)
                pbds[g] = _block_diag2(ps[g], second)
            for g in hs:
                xs[g] = xs[g] + _dot(xs[g].astype(BF16), pbds[g])
            power *= 2
        sols, exps = [], []
        for g in hs:
            b0, b1, g0, g1, gl0, gl1 = cols[g]
            e0 = jnp.exp(g0)
            e1 = jnp.exp(g1)
            exps.append((e0, e1))
            v2 = v_ref[pl.ds(r0, CHUNK), g * vw:(g + 1) * vw]
            rhs0 = jnp.concatenate([v2[:, :GDN_DV] * b0, (ks[g] * b0) * e0], axis=1).astype(BF16)
            rhs1 = jnp.concatenate([v2[:, GDN_DV:] * b1, (ks[g] * b1) * e1], axis=1).astype(BF16)
            rhs_bd = jnp.concatenate([jnp.concatenate([rhs0, zero_rhs], axis=1),
                                      jnp.concatenate([zero_rhs, rhs1], axis=1)], axis=0)
            sols.append(_dot(xs[g].astype(BF16), rhs_bd))
        sts, wss = [], []
        for g in hs:
            sol = sols[g]
            e0, e1 = exps[g]
            w2 = jnp.concatenate([sol[:, GDN_DV:vw], sol[:, vw + GDN_DV:]], axis=1)
            qd2 = jnp.concatenate([qs[g] * e0, qs[g] * e1], axis=1)
            sts.append(st_ref[g])
            wss.append(_dot(jnp.concatenate([w2, qd2], axis=0).astype(BF16), sts[g].astype(BF16)))
        new_states, results = [], []
        for g in hs:
            b0, b1, g0, g1, gl0, gl1 = cols[g]
            sol = sols[g]
            u2 = jnp.concatenate([sol[:, :GDN_DV], sol[:, vw:vw + GDN_DV]], axis=1)
            v_new = (u2 - wss[g][:CHUNK]).astype(BF16)
            attn2 = (prods[g][CHUNK:] * decays[g]).astype(BF16)
            o2 = wss[g][CHUNK:] + _dot(attn2, _block_diag2(v_new, second_v))
            k_end2 = jnp.concatenate([ks[g] * jnp.exp(gl0 - g0), ks[g] * jnp.exp(gl1 - g1)],
                                     axis=1).astype(BF16)
            upd = _dot_tn(k_end2, v_new)
            new_states.append(jnp.where(st_r, jnp.exp(gl1), jnp.exp(gl0)) * sts[g]
                              + jnp.where(same_head, upd, 0.0))
            outs = []
            for hh in range(GDN_REP):
                o = o2[:, hh * GDN_DV:(hh + 1) * GDN_DV]
                outs.append(o * lax.rsqrt(jnp.mean(o * o, axis=-1, keepdims=True) + RMS_EPS) * ng)
            z2 = z_ref[pl.ds(r0, CHUNK), g * vw:(g + 1) * vw]
            results.append((jnp.concatenate(outs, axis=1) * _silu(z2)).astype(BF16))
        for g in hs:
            st_ref[g] = new_states[g]
            o_ref[pl.ds(r0, CHUNK), g * vw:(g + 1) * vw] = results[g]
        return carry

    lax.fori_loop(0, rows // CHUNK, chunk, 0)


def _gdn(qk, v, z, bg, gc, gc_rows, ng, *, batch, seq, rows, heads):
    t = batch * seq
    nblk = seq // rows
    cpb = rows // CHUNK
    row = lambda b, h, n: b * nblk + n
    vw = heads * GDN_REP * GDN_DV
    kw = heads * GDN_DK
    return pl.pallas_call(
        functools.partial(_gdn_body, rows=rows, heads=heads),
        grid=(batch, GDN_QK_HEADS // heads, nblk),
        in_specs=[
            pl.BlockSpec((rows, kw), lambda b, h, n: (row(b, h, n), h)),
            pl.BlockSpec((rows, kw), lambda b, h, n: (row(b, h, n), GDN_QK_HEADS // heads + h)),
            pl.BlockSpec((rows, vw), lambda b, h, n: (row(b, h, n), h)),
            pl.BlockSpec((rows, vw), lambda b, h, n: (row(b, h, n), h)),
            pl.BlockSpec((rows, LANES), lambda b, h, n: (row(b, h, n), 0)),
            pl.BlockSpec((rows, LANES), lambda b, h, n: (row(b, h, n), 0)),
            pl.BlockSpec((cpb, GDN_V_HEADS, GDN_REP * CHUNK), lambda b, h, n: (row(b, h, n), 0, 0)),
            pl.BlockSpec((1, GDN_DV), lambda b, h, n: (0, 0)),
        ],
        out_specs=pl.BlockSpec((rows, vw), lambda b, h, n: (row(b, h, n), h)),
        out_shape=jax.ShapeDtypeStruct((t, GDN_VW), BF16),
        scratch_shapes=[pltpu.VMEM((heads, GDN_REP * GDN_DK, GDN_REP * GDN_DV), F32)],
        compiler_params=_params("arbitrary", "arbitrary", "arbitrary"),
        name="gdn",
    )(qk, qk, v, z, bg, gc, gc_rows, ng)


def _out_ln_body(a_ref, w_ref, x_ref, g_ref, b_ref, o_ref):
    mix = _dot(a_ref[...], w_ref[...])
    o_ref[...] = _layer_norm(ALPHA * x_ref[...] + mix, g_ref[...], b_ref[...])


def _out_ln(a, w, x, g, b, *, tm):
    t, d = x.shape
    kdim = a.shape[1]
    const = lambda shape: pl.BlockSpec(shape, lambda i: (0, 0), pipeline_mode=pl.Buffered(1))
    return pl.pallas_call(
        _out_ln_body,
        grid=(t // tm,),
        in_specs=[
            pl.BlockSpec((tm, kdim), lambda i: (i, 0)),
            const((kdim, d)),
            pl.BlockSpec((tm, d), lambda i: (i, 0)),
            const((1, d)), const((1, d)),
        ],
        out_specs=pl.BlockSpec((tm, d), lambda i: (i, 0)),
        out_shape=jax.ShapeDtypeStruct((t, d), F32),
        compiler_params=_params("arbitrary"),
        name="out_ln",
    )(a, w, x, g, b)


def _tiles(batch, seq):
    tm = min(512, seq)
    return dict(tm=tm, ffn_tm=min(1024, seq), proj_tm=min(1024, seq), gla_rows=min(256, seq), s5_rows=min(1024, seq), gdn_rows=min(256, seq),
                gdn_heads=16)


def _ffn(x, wg, wu, wd, layer, g, b, tm, emit_bf16):
    return _ffn_ln(x, wg, wu, wd, layer, g[None], b[None], tm=tm, tf=256, emit_bf16=emit_bf16)


def _pad_cols(w, n):
    return jnp.pad(w, ((0, 0), (0, n - w.shape[1])))


def _gla_s5_layer(x, xb, p, i, g, b, batch, seq, tl):
    w_in = p['ab_w_in']
    w_gate = _pad_cols(w_in[i, :, AB_A:AB_U], LANES).astype(BF16)
    proj = _proj(xb, jnp.swapaxes(w_in, 1, 2), tm=tl['proj_tm'], tn=1024, col0=0, n=AB_A, layer=i)
    u = _proj(xb, w_in[i, :, AB_U:], tm=tl['proj_tm'], tn=1024)
    gate = _proj(xb, w_gate, tm=tl['tm'], tn=LANES)
    wlr = jnp.pad(p['gla_w_lr'][i], ((0, LANES - GLA_RANK), (0, 0))).astype(BF16)
    o_gla = _gla(proj, gate, wlr, p['gla_b_lr'][i][None], p['gla_norm_g'][i][None],
                 batch=batch, seq=seq, rows=tl['gla_rows'])
    bbd, cbd, pw = _s5_tables(p['s5_lam_re'][i], p['s5_lam_im'][i], p['s5_b_re'][i], p['s5_b_im'][i],
                              p['s5_c_re'][i], p['s5_c_im'][i], p['s5_log_step'][i])
    ys = _s5(u, bbd, cbd, pw, p['s5_d'][i][None], batch=batch, seq=seq, rows=tl['s5_rows'], u_col0=0)
    w_out = p['ab_w_out'][i].astype(BF16)
    return _mix0_out(o_gla, ys, p['s5_glu_w'][i].astype(BF16), p['s5_glu_b'][i][None],
                     w_out[:GLA_VW], w_out[GLA_VW:], x, g[None], b[None], tm=tl['tm'])


def _gdn_layer(x, xb, p, i, g, b, batch, seq, tl):
    t = batch * seq
    w_in = p['gdn_w_in']
    conv_w = p['gdn_conv_w'][i]
    tm = tl['tm']
    w_t = jnp.swapaxes(w_in, 1, 2)
    qk = _conv_proj(xb, w_t, conv_w, tm=tl['proj_tm'], tn=1024, seq=seq, col0=0, n=2 * GDN_KW, layer=i,
                    l2_heads=True, n_scaled_tiles=GDN_KW // 1024, scale=GDN_DK ** -0.5)
    v = _conv_proj(xb, w_t, conv_w, tm=tl['proj_tm'], tn=1024, seq=seq, col0=2 * GDN_KW, n=GDN_VW,
                   layer=i)
    z = _proj(xb, w_t, tm=tl['proj_tm'], tn=1024, col0=GDN_Z, n=GDN_VW, layer=i)
    w_gate = _pad_cols(w_in[i, :, GDN_B:GDN_IN], LANES).astype(BF16)
    lane_pad = lambda a: jnp.pad(a, (GDN_V_HEADS, LANES - 2 * GDN_V_HEADS))[None]
    bg, gc, gr = _gate_proj(xb, w_gate, lane_pad(p['gdn_a_log'][i]), lane_pad(p['gdn_dt_bias'][i]), tm=tm)
    gc_rows = gr.reshape(GDN_V_HEADS, t // CHUNK, LANES).transpose(1, 0, 2)
    o = _gdn(qk, v, z, bg, gc, gc_rows, p['gdn_norm_g'][i][None], batch=batch, seq=seq,
             rows=tl['gdn_rows'], heads=tl['gdn_heads'])
    return _out_ln(o, p['gdn_w_out'][i].astype(BF16), x, g[None], b[None], tm=min(256, tm))


def kernel(x, ffn_a_gate, ffn_a_up, ffn_a_down, ffn_b_gate, ffn_b_up, ffn_b_down, ln_g, ln_b, ab_w_in, gla_w_lr, gla_b_lr, gla_norm_g, s5_lam_re, s5_lam_im, s5_b_re, s5_b_im, s5_c_re, s5_c_im, s5_d, s5_log_step, s5_glu_w, s5_glu_b, ab_w_out, gdn_w_in, gdn_conv_w, gdn_a_log, gdn_dt_bias, gdn_norm_g, gdn_w_out):
    batch, seq, d = x.shape
    assert d == D_MODEL and seq % CHUNK == 0
    p = dict(ab_w_in=ab_w_in, gla_w_lr=gla_w_lr, gla_b_lr=gla_b_lr, gla_norm_g=gla_norm_g,
             s5_lam_re=s5_lam_re, s5_lam_im=s5_lam_im, s5_b_re=s5_b_re, s5_b_im=s5_b_im,
             s5_c_re=s5_c_re, s5_c_im=s5_c_im, s5_d=s5_d, s5_log_step=s5_log_step,
             s5_glu_w=s5_glu_w, s5_glu_b=s5_glu_b, ab_w_out=ab_w_out, gdn_w_in=gdn_w_in,
             gdn_conv_w=gdn_conv_w, gdn_a_log=gdn_a_log, gdn_dt_bias=gdn_dt_bias,
             gdn_norm_g=gdn_norm_g, gdn_w_out=gdn_w_out)
    tl = _tiles(batch, seq)
    h = x.reshape(batch * seq, d).astype(F32)
    for layer in range(DEPTH):
        h, hb = _ffn(h, ffn_a_gate, ffn_a_up, ffn_a_down, layer,
                     ln_g[layer, 0], ln_b[layer, 0], tl['ffn_tm'], True)
        i = layer // 2
        mixer = _gla_s5_layer if layer % 2 == 0 else _gdn_layer
        h = mixer(h, hb, p, i, ln_g[layer, 1], ln_b[layer, 1], batch, seq, tl)
        (h,) = _ffn(h, ffn_b_gate, ffn_b_up, ffn_b_down, layer,
                    ln_g[layer, 2], ln_b[layer, 2], tl['ffn_tm'], False)
    return h.reshape(batch, seq, d)
```

```python
import functools
import math

import jax
import jax.numpy as jnp
import numpy as np
from jax import lax
from jax.experimental import pallas as pl
from jax.experimental.pallas import tpu as pltpu

F32 = jnp.float32
BF16 = jnp.bfloat16

LANES = 128
SUBLANES = 8
VMEM_LIMIT_BYTES = 60 * 1024 * 1024
LN_ROWS = 64
CONV_SUB = 256

D_MODEL = 2048
DEPTH = 2
ALPHA = (2.0 * DEPTH) ** 0.25
MACARON = 0.5
LN_EPS = 1e-5
RMS_EPS = 1e-6
L2_EPS = 1e-6
D_FF = 5504

GLA_HEADS = 4
GLA_DK = 128
GLA_DV = 256
GLA_RANK = 16
GLA_TAU = 16.0
CHUNK = 64
GLA_KW = GLA_HEADS * GLA_DK
GLA_VW = GLA_HEADS * GLA_DV

S5_WIDTH = 1024
S5_GROUP = 16
S5_GROUPS = 64
S5_STATE = 64
S5_TILE_GROUPS = LANES // S5_GROUP
S5_TILES = S5_WIDTH // LANES
S5_TILE_STATES = S5_TILE_GROUPS * S5_STATE

AB_K = GLA_KW
AB_V = AB_K + GLA_KW
AB_G = AB_V + GLA_VW
AB_A = AB_G + GLA_VW
AB_U = AB_A + GLA_RANK
AB_IN = AB_U + S5_WIDTH

GDN_QK_HEADS = 16
GDN_V_HEADS = 32
GDN_DK = 128
GDN_DV = 128
GDN_KW = GDN_QK_HEADS * GDN_DK
GDN_VW = GDN_V_HEADS * GDN_DV
GDN_QKV = 2 * GDN_KW + GDN_VW
GDN_Z = GDN_QKV
GDN_B = GDN_Z + GDN_VW
GDN_A = GDN_B + GDN_V_HEADS
GDN_IN = GDN_A + GDN_V_HEADS


def _params(*sem):
    return pltpu.CompilerParams(dimension_semantics=sem, vmem_limit_bytes=VMEM_LIMIT_BYTES)


def _dot(a, b):
    return jnp.dot(a, b, preferred_element_type=F32)


def _dot_nt(a, b):
    return lax.dot_general(a, b, (((1,), (1,)), ((), ())), preferred_element_type=F32)


def _dot_tn(a, b):
    return lax.dot_general(a, b, (((0,), (0,)), ((), ())), preferred_element_type=F32)


def _layer_norm(y, g, b):
    mu = jnp.mean(y, axis=-1, keepdims=True)
    yc = y - mu
    var = jnp.mean(yc * yc, axis=-1, keepdims=True)
    return yc * lax.rsqrt(var + LN_EPS) * g + b


def _silu(v):
    return v * jax.nn.sigmoid(v)


def _softplus(v):
    return jnp.maximum(v, 0.0) + jnp.log1p(jnp.exp(-jnp.abs(v)))


def _split3(v):
    hi = v.astype(BF16)
    r1 = v - hi.astype(F32)
    mid = r1.astype(BF16)
    lo = (r1 - mid.astype(F32)).astype(BF16)
    return hi, mid, lo


def _tril_mask(n, strict=False):
    r = lax.broadcasted_iota(jnp.int32, (n, n), 0)
    c = lax.broadcasted_iota(jnp.int32, (n, n), 1)
    return (c < r) if strict else (c <= r)


def _chunk_cumsum(v, rows):
    r = lax.broadcasted_iota(jnp.int32, (rows, rows), 0)
    c = lax.broadcasted_iota(jnp.int32, (rows, rows), 1)
    tri = ((c <= r) & ((c // CHUNK) == (r // CHUNK))).astype(BF16)
    hi, mid, lo = _split3(v)
    return _dot(tri, hi) + _dot(tri, mid) + _dot(tri, lo)


def _ffn_body(x_ref, wg_ref, wu_ref, wd_ref, wgt_ref, wut_ref, wdt_ref, g_ref, b_ref, o_ref, *rest, nj):
    ob_ref, xb_ref = rest if len(rest) == 2 else (None, rest[0])
    j = pl.program_id(1)

    @pl.when(j == 0)
    def _():
        xb_ref[...] = x_ref[...].astype(BF16)
        o_ref[...] = jnp.zeros_like(o_ref)

    def down(wg, wu, wd):
        xb = xb_ref[...]
        h = (_silu(_dot(xb, wg.astype(BF16))) * _dot(xb, wu.astype(BF16))).astype(BF16)
        return _dot(h, wd.astype(BF16))

    @pl.when(j < nj - 1)
    def _():
        o_ref[...] += down(wg_ref[...], wu_ref[...], wd_ref[...])

    @pl.when(j == nj - 1)
    def _():
        o_ref[...] += down(wgt_ref[...], wut_ref[...], wdt_ref[...])
        g = g_ref[...]
        b = b_ref[...]

        def rows(r, carry):
            sl = pl.ds(pl.multiple_of(r * LN_ROWS, LN_ROWS), LN_ROWS)
            yn = _layer_norm(ALPHA * x_ref[sl, :] + MACARON * o_ref[sl, :], g, b)
            o_ref[sl, :] = yn
            if ob_ref is not None:
                ob_ref[sl, :] = yn.astype(BF16)
            return carry

        lax.fori_loop(0, o_ref.shape[0] // LN_ROWS, rows, 0)


def _ffn_ln(x, wg, wu, wd, layer, g, b, *, tm, tf, emit_bf16):
    t, d = x.shape
    f = wg.shape[2]
    n_full = f // tf
    f_tail = f - n_full * tf
    assert f_tail > 0 and f_tail % LANES == 0
    nj = n_full + 1
    last = n_full - 1
    const = lambda shape: pl.BlockSpec(shape, lambda i, j: (0, 0), pipeline_mode=pl.Buffered(1))
    tile = pl.BlockSpec((tm, d), lambda i, j: (i, 0))
    tile_once = pl.BlockSpec((tm, d), lambda i, j: (i, 0), pipeline_mode=pl.Buffered(1))
    out_specs = [tile, tile_once] if emit_bf16 else [tile]
    out_shape = [jax.ShapeDtypeStruct((t, d), F32), jax.ShapeDtypeStruct((t, d), BF16)][:len(out_specs)]
    return pl.pallas_call(
        functools.partial(_ffn_body, nj=nj),
        grid=(t // tm, nj),
        in_specs=[
            pl.BlockSpec((tm, d), lambda i, j: (i, 0)),
            pl.BlockSpec((None, d, tf), lambda i, j: (layer, 0, jnp.minimum(j, last))),
            pl.BlockSpec((None, d, tf), lambda i, j: (layer, 0, jnp.minimum(j, last))),
            pl.BlockSpec((None, tf, d), lambda i, j: (layer, jnp.minimum(j, last), 0)),
            const((d, f_tail)), const((d, f_tail)), const((f_tail, d)),
            const((1, d)), const((1, d)),
        ],
        out_specs=out_specs,
        out_shape=out_shape,
        scratch_shapes=[pltpu.VMEM((tm, d), BF16)],
        compiler_params=_params("arbitrary", "arbitrary"),
        name="ffn_ln",
    )(x, wg, wu, wd, wg[layer, :, f - f_tail:], wu[layer, :, f - f_tail:], wd[layer, f - f_tail:], g, b)


def _load_weight_tile(w_ref, wb_ref):
    w = w_ref[...]
    if w.shape != wb_ref.shape:
        w = w.T
    wb_ref[...] = w.astype(BF16)


def _proj_body(x_ref, w_ref, o_ref, wb_ref):
    @pl.when(pl.program_id(1) == 0)
    def _():
        _load_weight_tile(w_ref, wb_ref)

    o_ref[...] = _dot(x_ref[...], wb_ref[...])


def _weight_spec(w, k, tn, j0, layer):
    if w.ndim == 3:
        return pl.BlockSpec((None, tn, k), lambda j, i: (layer, j0 + j, 0))
    return pl.BlockSpec((k, tn), lambda j, i: (0, j0 + j))


def _proj(xb, w, *, tm, tn, col0=0, n=None, layer=0):
    t, k = xb.shape
    n = (w.shape[1] if w.ndim == 3 else w.shape[-1]) if n is None else n
    assert col0 % tn == 0 and n % tn == 0
    j0 = col0 // tn
    return pl.pallas_call(
        _proj_body,
        grid=(n // tn, t // tm),
        in_specs=[
            pl.BlockSpec((tm, k), lambda j, i: (i, 0)),
            _weight_spec(w, k, tn, j0, layer),
        ],
        out_specs=pl.BlockSpec((tm, tn), lambda j, i: (i, j)),
        out_shape=jax.ShapeDtypeStruct((t, n), F32),
        scratch_shapes=[pltpu.VMEM((k, tn), BF16)],
        compiler_params=_params("arbitrary", "arbitrary"),
        name="proj",
    )(xb, w)


def _conv_proj_body(x_ref, w_ref, cw_ref, o_ref, wb_ref, stage_ref, *, tm, tn, tiles_per_seq,
                    l2_heads, n_scaled_tiles, scale):
    jt = pl.program_id(0)
    i = pl.program_id(1)

    @pl.when(i == 0)
    def _():
        _load_weight_tile(w_ref, wb_ref)

    @pl.when(i % tiles_per_seq == 0)
    def _():
        stage_ref[0:SUBLANES, :] = jnp.zeros((SUBLANES, tn), F32)

    if l2_heads:
        fac = jnp.where(jt < n_scaled_tiles, jnp.float32(scale), jnp.float32(1.0))

    def post(v):
        v = _silu(v)
        if not l2_heads:
            return v
        out = []
        for h in range(CONV_SUB // LANES):
            vh = v[:, h * LANES:(h + 1) * LANES]
            ss = jnp.sum(vh * vh, axis=-1, keepdims=True)
            out.append(vh * lax.rsqrt(ss + L2_EPS) * fac)
        return jnp.concatenate(out, axis=1)

    subs = [slice(s * CONV_SUB, (s + 1) * CONV_SUB) for s in range(tn // CONV_SUB)]

    x = x_ref[...]
    raws = [_dot(x, wb_ref[:, cols]) for cols in subs]
    for cols, raw in zip(subs, raws):
        cw = cw_ref[:, cols]
        stage_ref[SUBLANES:SUBLANES + tm, cols] = raw
        acc = raw * cw[3:4]
        for d in (1, 2, 3):
            acc = acc + stage_ref[SUBLANES - d:SUBLANES - d + tm, cols] * cw[3 - d:4 - d]
        stage_ref[0:SUBLANES, cols] = raw[tm - SUBLANES:tm]
        o_ref[:, cols] = post(acc)


def _conv_proj(xb, w, cw, *, tm, tn, seq, col0, n, layer=0, l2_heads=False, n_scaled_tiles=0, scale=1.0):
    t, k = xb.shape
    nt = t // tm
    assert col0 % tn == 0 and n % tn == 0
    j0 = col0 // tn
    body = functools.partial(_conv_proj_body, tm=tm, tn=tn, tiles_per_seq=seq // tm,
                             l2_heads=l2_heads, n_scaled_tiles=n_scaled_tiles, scale=scale)
    return pl.pallas_call(
        body,
        grid=(n // tn, nt),
        in_specs=[
            pl.BlockSpec((tm, k), lambda j, i: (i, 0)),
            _weight_spec(w, k, tn, j0, layer),
            pl.BlockSpec((4, tn), lambda j, i: (0, j0 + j)),
        ],
        out_specs=pl.BlockSpec((tm, tn), lambda j, i: (i, j)),
        out_shape=jax.ShapeDtypeStruct((t, n), F32),
        scratch_shapes=[pltpu.VMEM((k, tn), BF16),
                        pltpu.VMEM((SUBLANES + tm, tn), F32)],
        compiler_params=_params("arbitrary", "arbitrary"),
        name="conv_proj",
    )(xb, w, cw)


def _gate_proj_body(x_ref, w_ref, alog_ref, dtb_ref, bg_ref, gc_ref, gr_ref, *, tm):
    raw = _dot(x_ref[...], w_ref[...])
    lane = lax.broadcasted_iota(jnp.int32, raw.shape, 1)
    beta = jax.nn.sigmoid(raw)
    g = -jnp.exp(alog_ref[...]) * _softplus(raw + dtb_ref[...])
    g = jnp.where((lane >= GDN_V_HEADS) & (lane < 2 * GDN_V_HEADS), g, 0.0)
    bg_ref[...] = jnp.where(lane < GDN_V_HEADS, beta, g)
    gc = _chunk_cumsum(g, tm)
    gc_ref[...] = gc
    width = (tm // CHUNK) * LANES
    t_idx = lax.broadcasted_iota(jnp.int32, (tm, width), 0)
    col = lax.broadcasted_iota(jnp.int32, (tm, width), 1)
    sel = (t_idx == (col // LANES) * CHUNK + (col % CHUNK)).astype(BF16)
    hi, mid, lo = _split3(gc)
    rows = _dot_tn(hi, sel) + _dot_tn(mid, sel) + _dot_tn(lo, sel)
    gr_ref[...] = rows[GDN_V_HEADS:2 * GDN_V_HEADS]


def _gate_proj(xb, w, alog, dtb, *, tm):
    t, k = xb.shape
    return pl.pallas_call(
        functools.partial(_gate_proj_body, tm=tm),
        grid=(t // tm,),
        in_specs=[
            pl.BlockSpec((tm, k), lambda i: (i, 0)),
            pl.BlockSpec((k, LANES), lambda i: (0, 0)),
            pl.BlockSpec((1, LANES), lambda i: (0, 0)),
            pl.BlockSpec((1, LANES), lambda i: (0, 0)),
        ],
        out_specs=[pl.BlockSpec((tm, LANES), lambda i: (i, 0))] * 2
        + [pl.BlockSpec((GDN_V_HEADS, (tm // CHUNK) * LANES), lambda i: (0, i))],
        out_shape=[jax.ShapeDtypeStruct((t, LANES), F32)] * 2
        + [jax.ShapeDtypeStruct((GDN_V_HEADS, (t // CHUNK) * LANES), F32)],
        compiler_params=_params("arbitrary"),
        name="gate_proj",
    )(xb, w, alog, dtb)


def _gla_body(q_ref, k_ref, v_ref, go_ref, a_ref, wlr_ref, blr_ref, ng_ref, o_ref, st_ref, *, rows):
    @pl.when(pl.program_id(1) == 0)
    def _():
        st_ref[...] = jnp.zeros_like(st_ref)

    causal = _tril_mask(CHUNK)
    ng = ng_ref[...]
    heads = range(GLA_HEADS)
    chunks = range(rows // CHUNK)
    head = lambda a, h: a[:, h * GLA_DK:(h + 1) * GLA_DK]

    z = _dot(a_ref[...].astype(BF16), wlr_ref[...]) + blr_ref[...]
    b = _chunk_cumsum(-_softplus(-z) / GLA_TAU, rows)
    k = k_ref[...]
    q_dec = (q_ref[...] * (GLA_DK ** -0.5) * jnp.exp(b)).astype(BF16)
    k_inv = (k * jnp.exp(-b)).astype(BF16)
    rows_of = lambda a, c: a[c * CHUNK:(c + 1) * CHUNK]
    b_last = [rows_of(b, c)[CHUNK - 1:CHUNK] for c in chunks]
    k_end = [(rows_of(k, c) * jnp.exp(b_last[c] - rows_of(b, c))).astype(BF16) for c in chunks]
    scores = [[jnp.where(causal, _dot_nt(head(rows_of(q_dec, c), h), head(rows_of(k_inv, c), h)),
                         0.0).astype(BF16) for h in heads] for c in chunks]
    for c in chunks:
        sl = slice(c * CHUNK, (c + 1) * CHUNK)
        sts = [st_ref[h] for h in heads]
        vs = [v_ref[sl, h * GLA_DV:(h + 1) * GLA_DV].astype(BF16) for h in heads]
        qd = rows_of(q_dec, c)
        outs = [_dot(scores[c][h], vs[h]) + _dot_nt(head(qd, h), sts[h].astype(BF16)) for h in heads]
        decay = jnp.exp(b_last[c])
        new = [sts[h] * head(decay, h) + _dot_tn(vs[h], head(k_end[c], h)) for h in heads]
        for h in heads:
            st_ref[h] = new[h]
        normed = [o * lax.rsqrt(jnp.mean(o * o, axis=-1, keepdims=True) + RMS_EPS) * ng for o in outs]
        o_ref[sl, :] = (jnp.concatenate(normed, axis=1) * _silu(go_ref[sl, :])).astype(BF16)


def _gla(proj, gate, wlr, blr, ng, *, batch, seq, rows):
    t = batch * seq
    nblk = seq // rows
    row = lambda b, n: b * nblk + n
    return pl.pallas_call(
        functools.partial(_gla_body, rows=rows),
        grid=(batch, nblk),
        in_specs=[
            pl.BlockSpec((rows, GLA_KW), lambda b, n: (row(b, n), 0)),
            pl.BlockSpec((rows, GLA_KW), lambda b, n: (row(b, n), 1)),
            pl.BlockSpec((rows, GLA_VW), lambda b, n: (row(b, n), AB_V // GLA_VW)),
            pl.BlockSpec((rows, GLA_VW), lambda b, n: (row(b, n), AB_G // GLA_VW)),
            pl.BlockSpec((rows, LANES), lambda b, n: (row(b, n), 0)),
            pl.BlockSpec((LANES, GLA_KW), lambda b, n: (0, 0)),
            pl.BlockSpec((1, GLA_KW), lambda b, n: (0, 0)),
            pl.BlockSpec((1, GLA_DV), lambda b, n: (0, 0)),
        ],
        out_specs=pl.BlockSpec((rows, GLA_VW), lambda b, n: (row(b, n), 0)),
        out_shape=jax.ShapeDtypeStruct((t, GLA_VW), BF16),
        scratch_shapes=[pltpu.VMEM((GLA_HEADS, GLA_DV, GLA_DK), F32)],
        compiler_params=_params("arbitrary", "arbitrary"),
        name="gla",
    )(proj, proj, proj, proj, gate, wlr, blr, ng)


def _s5_body(u_ref, bbd_ref, cbd_ref, pw_ref, d_ref, y_ref, xs_ref, cr_ref, ci_ref, *, rows):
    @pl.when(pl.program_id(2) == 0)
    def _():
        cr_ref[...] = jnp.zeros_like(cr_ref)
        ci_ref[...] = jnp.zeros_like(ci_ref)

    ns = S5_TILE_STATES
    u = u_ref[...]
    bu = _dot(u.astype(BF16), bbd_ref[0])
    levels = [(pw_ref[0, 2 * l], pw_ref[0, 2 * l + 1], 1 << l) for l in range(3)]
    qr, qi = pw_ref[0, 6], pw_ref[0, 7]
    cr = cr_ref[...]
    ci = ci_ref[...]
    for blk in range(rows // SUBLANES):
        sl = slice(blk * SUBLANES, (blk + 1) * SUBLANES)
        xr = bu[sl, :ns]
        xi = bu[sl, ns:]
        for pr, pi, s in levels:
            sr = pltpu.roll(xr, s, 0)
            si = pltpu.roll(xi, s, 0)
            xr, xi = xr + (pr * sr - pi * si), xi + (pr * si + pi * sr)
        xr, xi = xr + (qr * cr - qi * ci), xi + (qr * ci + qi * cr)
        cr = jnp.broadcast_to(xr[SUBLANES - 1:SUBLANES], (SUBLANES, ns))
        ci = jnp.broadcast_to(xi[SUBLANES - 1:SUBLANES], (SUBLANES, ns))
        xs_ref[sl, :ns] = xr
        xs_ref[sl, ns:] = xi
    cr_ref[...] = cr
    ci_ref[...] = ci
    y = _dot(xs_ref[...].astype(BF16), cbd_ref[0]) + d_ref[...] * u
    y3 = y * y * y
    y_ref[...] = y * (0.5 * (1.0 + jnp.tanh(np.float32(math.sqrt(2.0 / math.pi)) * (y + 0.044715 * y3))))


def _s5(proj, bbd, cbd, pw, d, *, batch, seq, rows, u_col0):
    t = batch * seq
    nblk = seq // rows
    ns = S5_TILE_STATES
    return pl.pallas_call(
        functools.partial(_s5_body, rows=rows),
        grid=(batch, S5_TILES, nblk),
        in_specs=[
            pl.BlockSpec((rows, LANES), lambda b, j, n: (b * nblk + n, u_col0 + j)),
            pl.BlockSpec((1, LANES, 2 * ns), lambda b, j, n: (j, 0, 0)),
            pl.BlockSpec((1, 2 * ns, LANES), lambda b, j, n: (j, 0, 0)),
            pl.BlockSpec((1, 8, SUBLANES, ns), lambda b, j, n: (j, 0, 0, 0)),
            pl.BlockSpec((1, LANES), lambda b, j, n: (0, j)),
        ],
        out_specs=pl.BlockSpec((rows, LANES), lambda b, j, n: (b * nblk + n, j)),
        out_shape=jax.ShapeDtypeStruct((t, S5_WIDTH), F32),
        scratch_shapes=[pltpu.VMEM((rows, 2 * ns), F32),
                        pltpu.VMEM((SUBLANES, ns), F32),
                        pltpu.VMEM((SUBLANES, ns), F32)],
        compiler_params=_params("arbitrary", "arbitrary", "arbitrary"),
        name="s5",
    )(proj, bbd, cbd, pw, d)


def _s5_tables(lam_re, lam_im, b_re, b_im, c_re, c_im, log_step):
    step = jnp.exp(log_step)[:, None]
    mag = jnp.exp(lam_re * step)
    a_re = mag * jnp.cos(lam_im * step)
    a_im = mag * jnp.sin(lam_im * step)
    inv_den = 1.0 / (lam_re * lam_re + lam_im * lam_im)
    f_re = ((a_re - 1.0) * lam_re + a_im * lam_im) * inv_den
    f_im = (a_im * lam_re - (a_re - 1.0) * lam_im) * inv_den
    bb_re = f_re[..., None] * b_re - f_im[..., None] * b_im
    bb_im = f_re[..., None] * b_im + f_im[..., None] * b_re
    tg, nt = S5_TILE_GROUPS, S5_TILES
    eye = jnp.eye(tg, dtype=F32)

    def in_table(bb):
        bb = bb.reshape(nt, tg, S5_STATE, S5_GROUP)
        return jnp.einsum('jgph,gk->jghkp', bb, eye).reshape(nt, LANES, S5_TILE_STATES)

    def out_table(cc):
        cc = cc.reshape(nt, tg, S5_GROUP, S5_STATE)
        return jnp.einsum('jghp,gk->jgpkh', cc, eye).reshape(nt, S5_TILE_STATES, LANES)

    bbd = jnp.concatenate([in_table(bb_re), in_table(bb_im)], axis=2).astype(BF16)
    cbd = jnp.concatenate([out_table(c_re), -out_table(c_im)], axis=1).astype(BF16)

    def cmul(x, y):
        return x[0] * y[0] - x[1] * y[1], x[0] * y[1] + x[1] * y[0]

    a1 = (a_re.reshape(nt, 1, S5_TILE_STATES), a_im.reshape(nt, 1, S5_TILE_STATES))
    powers = [a1]
    for _ in range(SUBLANES - 1):
        powers.append(cmul(powers[-1], a1))
    row = jnp.arange(SUBLANES)[None, :, None]
    tabs = []
    for s in (1, 2, 4):
        pr, pi = powers[s - 1]
        tabs.append(jnp.where(row >= s, pr, 0.0))
        tabs.append(jnp.where(row >= s, pi, 0.0))
    tabs.append(jnp.concatenate([p[0] for p in powers], axis=1))
    tabs.append(jnp.concatenate([p[1] for p in powers], axis=1))
    pw = jnp.stack([jnp.broadcast_to(tb, (nt, SUBLANES, S5_TILE_STATES)) for tb in tabs], axis=1)
    return bbd, cbd, pw.astype(F32)


def _mix0_out_body(og_ref, ys_ref, gw_ref, gb_ref, w1_ref, w2_ref, x_ref, g_ref, b_ref, o_ref):
    ys = ys_ref[...]
    gate = jax.nn.sigmoid(_dot(ys.astype(BF16), gw_ref[...]) + gb_ref[...])
    os5 = (ys * gate).astype(BF16)
    mix = _dot(og_ref[...], w1_ref[...]) + _dot(os5, w2_ref[...])
    o_ref[...] = _layer_norm(ALPHA * x_ref[...] + mix, g_ref[...], b_ref[...])


def _mix0_out(og, ys, gw, gb, w1, w2, x, g, b, *, tm):
    t, d = x.shape
    const = lambda shape: pl.BlockSpec(shape, lambda i: (0, 0))
    return pl.pallas_call(
        _mix0_out_body,
        grid=(t // tm,),
        in_specs=[
            pl.BlockSpec((tm, GLA_VW), lambda i: (i, 0)),
            pl.BlockSpec((tm, S5_WIDTH), lambda i: (i, 0)),
            const((S5_WIDTH, S5_WIDTH)), const((1, S5_WIDTH)),
            const((GLA_VW, d)), const((S5_WIDTH, d)),
            pl.BlockSpec((tm, d), lambda i: (i, 0)),
            const((1, d)), const((1, d)),
        ],
        out_specs=pl.BlockSpec((tm, d), lambda i: (i, 0)),
        out_shape=jax.ShapeDtypeStruct((t, d), F32),
        compiler_params=_params("arbitrary"),
        name="mix0_out",
    )(og, ys, gw, gb, w1, w2, x, g, b)


GDN_REP = GDN_V_HEADS // GDN_QK_HEADS


def _block_diag2(m, second):
    zero = jnp.zeros_like(m)
    return jnp.concatenate([jnp.where(second, zero, m), jnp.where(second, m, zero)], axis=0)


def _gdn_body(q_ref, k_ref, v_ref, z_ref, bg_ref, gc_ref, gr_ref, ng_ref, o_ref, st_ref, *, rows, heads):
    hg = pl.program_id(1)
    c2 = 2 * CHUNK
    vw = GDN_REP * GDN_DV

    @pl.when(pl.program_id(2) == 0)
    def _():
        st_ref[...] = jnp.zeros_like(st_ref)

    lane2 = lax.broadcasted_iota(jnp.int32, (CHUNK, c2), 1)
    row2 = lax.broadcasted_iota(jnp.int32, (CHUNK, c2), 0)
    second = lane2 >= CHUNK
    pos = jnp.where(second, lane2 - CHUNK, lane2)
    incl2 = pos <= row2
    strict2 = pos < row2
    eye2 = pos == row2
    lane_bg = lax.broadcasted_iota(jnp.int32, (CHUNK, LANES), 1)
    second_v = lax.broadcasted_iota(jnp.int32, (CHUNK, vw), 1) >= GDN_DV
    st_r = lax.broadcasted_iota(jnp.int32, (GDN_REP * GDN_DK, vw), 0) >= GDN_DK
    st_c = lax.broadcasted_iota(jnp.int32, (GDN_REP * GDN_DK, vw), 1) >= GDN_DV
    same_head = st_r == st_c
    ng = ng_ref[...]
    zero_rhs = jnp.zeros((CHUNK, GDN_DV + GDN_DK), BF16)

    def chunk(c, carry):
        r0 = pl.multiple_of(c * CHUNK, CHUNK)
        bg = bg_ref[pl.ds(r0, CHUNK), :]
        gcb = gc_ref[pl.ds(r0, CHUNK), :]

        def column(arr, idx):
            return jnp.sum(jnp.where(lane_bg == idx, arr, 0.0), axis=-1, keepdims=True)

        hs = range(heads)
        ks, qs, prods, cols, decays, xs, ps, pbds = [], [], [], [], [], [], [], []
        for g in hs:
            k = k_ref[pl.ds(r0, CHUNK), g * GDN_DK:(g + 1) * GDN_DK]
            q = q_ref[pl.ds(r0, CHUNK), g * GDN_DK:(g + 1) * GDN_DK]
            kb = k.astype(BF16)
            ks.append(k)
            qs.append(q)
            prods.append(_dot_nt(jnp.concatenate([kb, q.astype(BF16)], axis=0),
                                 jnp.concatenate([kb, kb], axis=0)))
        for g in hs:
            hq = hg * heads + g
            b0 = column(bg, GDN_REP * hq)
            b1 = column(bg, GDN_REP * hq + 1)
            g0 = column(gcb, GDN_V_HEADS + GDN_REP * hq)
            g1 = column(gcb, GDN_V_HEADS + GDN_REP * hq + 1)
            cols.append((b0, b1, g0, g1, g0[CHUNK - 1:CHUNK, :], g1[CHUNK - 1:CHUNK, :]))
            gc_row = jnp.where(second[0:1], gr_ref[c, pl.ds(GDN_REP * hq + 1, 1), :],
                               gr_ref[c, pl.ds(GDN_REP * hq, 1), :])
            decay2 = jnp.exp(jnp.where(incl2, jnp.where(second, g1, g0) - gc_row, -jnp.inf))
            a2 = jnp.where(strict2, prods[g][:CHUNK] * jnp.where(second, b1, b0) * decay2, 0.0)
            decays.append(decay2)
            xs.append(jnp.where(eye2, 1.0, 0.0) - a2)
            ps.append(a2.astype(BF16))
            pbds.append(_block_diag2(ps[g], second))
        power = 1
        while 2 * power < CHUNK:
            for g in hs:
                ps[g] = _dot(ps[g], pbds[g]).astype(BF16)
                pbds[g] = _block_diag2(ps[g], second)
            for g in hs:
                xs[g] = xs[g] + _dot(xs[g].astype(BF16), pbds[g])
            power *= 2
        sols, exps = [], []
        for g in hs:
            b0, b1, g0, g1, gl0, gl1 = cols[g]
            e0 = jnp.exp(g0)
            e1 = jnp.exp(g1)
            exps.append((e0, e1))
            v2 = v_ref[pl.ds(r0, CHUNK), g * vw:(g + 1) * vw]
            rhs0 = jnp.concatenate([v2[:, :GDN_DV] * b0, (ks[g] * b0) * e0], axis=1).astype(BF16)
            rhs1 = jnp.concatenate([v2[:, GDN_DV:] * b1, (ks[g] * b1) * e1], axis=1).astype(BF16)
            rhs_bd = jnp.concatenate([jnp.concatenate([rhs0, zero_rhs], axis=1),
                                      jnp.concatenate([zero_rhs, rhs1], axis=1)], axis=0)
            sols.append(_dot(xs[g].astype(BF16), rhs_bd))
        sts, wss = [], []
        for g in hs:
            sol = sols[g]
            e0, e1 = exps[g]
            w2 = jnp.concatenate([sol[:, GDN_DV:vw], sol[:, vw + GDN_DV:]], axis=1)
            qd2 = jnp.concatenate([qs[g] * e0, qs[g] * e1], axis=1)
            sts.append(st_ref[g])
            wss.append(_dot(jnp.concatenate([w2, qd2], axis=0).astype(BF16), sts[g].astype(BF16)))
        new_states, results = [], []
        for g in hs:
            b0, b1, g0, g1, gl0, gl1 = cols[g]
            sol = sols[g]
            u2 = jnp.concatenate([sol[:, :GDN_DV], sol[:, vw:vw + GDN_DV]], axis=1)
            v_new = (u2 - wss[g][:CHUNK]).astype(BF16)
            attn2 = (prods[g][CHUNK:] * decays[g]).astype(BF16)
            o2 = wss[g][CHUNK:] + _dot(attn2, _block_diag2(v_new, second_v))
            k_end2 = jnp.concatenate([ks[g] * jnp.exp(gl0 - g0), ks[g] * jnp.exp(gl1 - g1)],
                                     axis=1).astype(BF16)
            upd = _dot_tn(k_end2, v_new)
            new_states.append(jnp.where(st_r, jnp.exp(gl1), jnp.exp(gl0)) * sts[g]
                              + jnp.where(same_head, upd, 0.0))
            outs = []
            for hh in range(GDN_REP):
                o = o2[:, hh * GDN_DV:(hh + 1) * GDN_DV]
                outs.append(o * lax.rsqrt(jnp.mean(o * o, axis=-1, keepdims=True) + RMS_EPS) * ng)
            z2 = z_ref[pl.ds(r0, CHUNK), g * vw:(g + 1) * vw]
            results.append((jnp.concatenate(outs, axis=1) * _silu(z2)).astype(BF16))
        for g in hs:
            st_ref[g] = new_states[g]
            o_ref[pl.ds(r0, CHUNK), g * vw:(g + 1) * vw] = results[g]
        return carry

    lax.fori_loop(0, rows // CHUNK, chunk, 0)


def _gdn(qk, v, z, bg, gc, gc_rows, ng, *, batch, seq, rows, heads):
    t = batch * seq
    nblk = seq // rows
    cpb = rows // CHUNK
    row = lambda b, h, n: b * nblk + n
    vw = heads * GDN_REP * GDN_DV
    kw = heads * GDN_DK
    return pl.pallas_call(
        functools.partial(_gdn_body, rows=rows, heads=heads),
        grid=(batch, GDN_QK_HEADS // heads, nblk),
        in_specs=[
            pl.BlockSpec((rows, kw), lambda b, h, n: (row(b, h, n), h)),
            pl.BlockSpec((rows, kw), lambda b, h, n: (row(b, h, n), GDN_QK_HEADS // heads + h)),
            pl.BlockSpec((rows, vw), lambda b, h, n: (row(b, h, n), h)),
            pl.BlockSpec((rows, vw), lambda b, h, n: (row(b, h, n), h)),
            pl.BlockSpec((rows, LANES), lambda b, h, n: (row(b, h, n), 0)),
            pl.BlockSpec((rows, LANES), lambda b, h, n: (row(b, h, n), 0)),
            pl.BlockSpec((cpb, GDN_V_HEADS, GDN_REP * CHUNK), lambda b, h, n: (row(b, h, n), 0, 0)),
            pl.BlockSpec((1, GDN_DV), lambda b, h, n: (0, 0)),
        ],
        out_specs=pl.BlockSpec((rows, vw), lambda b, h, n: (row(b, h, n), h)),
        out_shape=jax.ShapeDtypeStruct((t, GDN_VW), BF16),
        scratch_shapes=[pltpu.VMEM((heads, GDN_REP * GDN_DK, GDN_REP * GDN_DV), F32)],
        compiler_params=_params("arbitrary", "arbitrary", "arbitrary"),
        name="gdn",
    )(qk, qk, v, z, bg, gc, gc_rows, ng)


def _out_ln_body(a_ref, w_ref, x_ref, g_ref, b_ref, o_ref):
    mix = _dot(a_ref[...], w_ref[...])
    o_ref[...] = _layer_norm(ALPHA * x_ref[...] + mix, g_ref[...], b_ref[...])


def _out_ln(a, w, x, g, b, *, tm):
    t, d = x.shape
    kdim = a.shape[1]
    const = lambda shape: pl.BlockSpec(shape, lambda i: (0, 0), pipeline_mode=pl.Buffered(1))
    return pl.pallas_call(
        _out_ln_body,
        grid=(t // tm,),
        in_specs=[
            pl.BlockSpec((tm, kdim), lambda i: (i, 0)),
            const((kdim, d)),
            pl.BlockSpec((tm, d), lambda i: (i, 0)),
            const((1, d)), const((1, d)),
        ],
        out_specs=pl.BlockSpec((tm, d), lambda i: (i, 0)),
        out_shape=jax.ShapeDtypeStruct((t, d), F32),
        compiler_params=_params("arbitrary"),
        name="out_ln",
    )(a, w, x, g, b)


def _tiles(batch, seq):
    tm = min(512, seq)
    return dict(tm=tm, ffn_tm=min(1024, seq), proj_tm=min(1024, seq), gla_rows=min(256, seq), s5_rows=min(1024, seq), gdn_rows=min(256, seq),
                gdn_heads=16)


def _ffn(x, wg, wu, wd, layer, g, b, tm, emit_bf16):
    return _ffn_ln(x, wg, wu, wd, layer, g[None], b[None], tm=tm, tf=256, emit_bf16=emit_bf16)


def _pad_cols(w, n):
    return jnp.pad(w, ((0, 0), (0, n - w.shape[1])))


def _gla_s5_layer(x, xb, p, i, g, b, batch, seq, tl):
    w_in = p['ab_w_in']
    w_gate = _pad_cols(w_in[i, :, AB_A:AB_U], LANES).astype(BF16)
    proj = _proj(xb, jnp.swapaxes(w_in, 1, 2), tm=tl['proj_tm'], tn=1024, col0=0, n=AB_A, layer=i)
    u = _proj(xb, w_in[i, :, AB_U:], tm=tl['proj_tm'], tn=1024)
    gate = _proj(xb, w_gate, tm=tl['tm'], tn=LANES)
    wlr = jnp.pad(p['gla_w_lr'][i], ((0, LANES - GLA_RANK), (0, 0))).astype(BF16)
    o_gla = _gla(proj, gate, wlr, p['gla_b_lr'][i][None], p['gla_norm_g'][i][None],
                 batch=batch, seq=seq, rows=tl['gla_rows'])
    bbd, cbd, pw = _s5_tables(p['s5_lam_re'][i], p['s5_lam_im'][i], p['s5_b_re'][i], p['s5_b_im'][i],
                              p['s5_c_re'][i], p['s5_c_im'][i], p['s5_log_step'][i])
    ys = _s5(u, bbd, cbd, pw, p['s5_d'][i][None], batch=batch, seq=seq, rows=tl['s5_rows'], u_col0=0)
    w_out = p['ab_w_out'][i].astype(BF16)
    return _mix0_out(o_gla, ys, p['s5_glu_w'][i].astype(BF16), p['s5_glu_b'][i][None],
                     w_out[:GLA_VW], w_out[GLA_VW:], x, g[None], b[None], tm=tl['tm'])


def _gdn_layer(x, xb, p, i, g, b, batch, seq, tl):
    t = batch * seq
    w_in = p['gdn_w_in']
    conv_w = p['gdn_conv_w'][i]
    tm = tl['tm']
    w_t = jnp.swapaxes(w_in, 1, 2)
    qk = _conv_proj(xb, w_t, conv_w, tm=tl['proj_tm'], tn=1024, seq=seq, col0=0, n=2 * GDN_KW, layer=i,
                    l2_heads=True, n_scaled_tiles=GDN_KW // 1024, scale=GDN_DK ** -0.5)
    v = _conv_proj(xb, w_t, conv_w, tm=tl['proj_tm'], tn=1024, seq=seq, col0=2 * GDN_KW, n=GDN_VW,
                   layer=i)
    z = _proj(xb, w_t, tm=tl['proj_tm'], tn=1024, col0=GDN_Z, n=GDN_VW, layer=i)
    w_gate = _pad_cols(w_in[i, :, GDN_B:GDN_IN], LANES).astype(BF16)
    lane_pad = lambda a: jnp.pad(a, (GDN_V_HEADS, LANES - 2 * GDN_V_HEADS))[None]
    bg, gc, gr = _gate_proj(xb, w_gate, lane_pad(p['gdn_a_log'][i]), lane_pad(p['gdn_dt_bias'][i]), tm=tm)
    gc_rows = gr.reshape(GDN_V_HEADS, t // CHUNK, LANES).transpose(1, 0, 2)
    o = _gdn(qk, v, z, bg, gc, gc_rows, p['gdn_norm_g'][i][None], batch=batch, seq=seq,
             rows=tl['gdn_rows'], heads=tl['gdn_heads'])
    return _out_ln(o, p['gdn_w_out'][i].astype(BF16), x, g[None], b[None], tm=min(256, tm))


def kernel(x, ffn_a_gate, ffn_a_up, ffn_a_down, ffn_b_gate, ffn_b_up, ffn_b_down, ln_g, ln_b, ab_w_in, gla_w_lr, gla_b_lr, gla_norm_g, s5_lam_re, s5_lam_im, s5_b_re, s5_b_im, s5_c_re, s5_c_im, s5_d, s5_log_step, s5_glu_w, s5_glu_b, ab_w_out, gdn_w_in, gdn_conv_w, gdn_a_log, gdn_dt_bias, gdn_norm_g, gdn_w_out):
    batch, seq, d = x.shape
    assert d == D_MODEL and seq % CHUNK == 0
    p = dict(ab_w_in=ab_w_in, gla_w_lr=gla_w_lr, gla_b_lr=gla_b_lr, gla_norm_g=gla_norm_g,
             s5_lam_re=s5_lam_re, s5_lam_im=s5_lam_im, s5_b_re=s5_b_re, s5_b_im=s5_b_im,
             s5_c_re=s5_c_re, s5_c_im=s5_c_im, s5_d=s5_d, s5_log_step=s5_log_step,
             s5_glu_w=s5_glu_w, s5_glu_b=s5_glu_b, ab_w_out=ab_w_out, gdn_w_in=gdn_w_in,
             gdn_conv_w=gdn_conv_w, gdn_a_log=gdn_a_log, gdn_dt_bias=gdn_dt_bias,
             gdn_norm_g=gdn_norm_g, gdn_w_out=gdn_w_out)
    tl = _tiles(batch, seq)
    h = x.reshape(batch * seq, d).astype(F32)
    for layer in range(DEPTH):
        h, hb = _ffn(h, ffn_a_gate, ffn_a_up, ffn_a_down, layer,
                     ln_g[layer, 0], ln_b[layer, 0], tl['ffn_tm'], True)
        i = layer // 2
        mixer = _gla_s5_layer if layer % 2 == 0 else _gdn_layer
        h = mixer(h, hb, p, i, ln_g[layer, 1], ln_b[layer, 1], batch, seq, tl)
        (h,) = _ffn(h, ffn_b_gate, ffn_b_up, ffn_b_down, layer,
                    ln_g[layer, 2], ln_b[layer, 2], tl['ffn_tm'], False)
    return h.reshape(batch, seq, d)
```

```python
import functools
import math

import jax
import jax.numpy as jnp
import numpy as np
from jax import lax
from jax.experimental import pallas as pl
from jax.experimental.pallas import tpu as pltpu

F32 = jnp.float32
BF16 = jnp.bfloat16

LANES = 128
SUBLANES = 8
VMEM_LIMIT_BYTES = 60 * 1024 * 1024
LN_ROWS = 64
CONV_SUB = 256

D_MODEL = 2048
DEPTH = 2
ALPHA = (2.0 * DEPTH) ** 0.25
MACARON = 0.5
LN_EPS = 1e-5
RMS_EPS = 1e-6
L2_EPS = 1e-6
D_FF = 5504

GLA_HEADS = 4
GLA_DK = 128
GLA_DV = 256
GLA_RANK = 16
GLA_TAU = 16.0
CHUNK = 64
GLA_KW = GLA_HEADS * GLA_DK
GLA_VW = GLA_HEADS * GLA_DV

S5_WIDTH = 1024
S5_GROUP = 16
S5_GROUPS = 64
S5_STATE = 64
S5_TILE_GROUPS = LANES // S5_GROUP
S5_TILES = S5_WIDTH // LANES
S5_TILE_STATES = S5_TILE_GROUPS * S5_STATE

AB_K = GLA_KW
AB_V = AB_K + GLA_KW
AB_G = AB_V + GLA_VW
AB_A = AB_G + GLA_VW
AB_U = AB_A + GLA_RANK
AB_IN = AB_U + S5_WIDTH

GDN_QK_HEADS = 16
GDN_V_HEADS = 32
GDN_DK = 128
GDN_DV = 128
GDN_KW = GDN_QK_HEADS * GDN_DK
GDN_VW = GDN_V_HEADS * GDN_DV
GDN_QKV = 2 * GDN_KW + GDN_VW
GDN_Z = GDN_QKV
GDN_B = GDN_Z + GDN_VW
GDN_A = GDN_B + GDN_V_HEADS
GDN_IN = GDN_A + GDN_V_HEADS


def _params(*sem):
    return pltpu.CompilerParams(dimension_semantics=sem, vmem_limit_bytes=VMEM_LIMIT_BYTES)


def _dot(a, b):
    return jnp.dot(a, b, preferred_element_type=F32)


def _dot_nt(a, b):
    return lax.dot_general(a, b, (((1,), (1,)), ((), ())), preferred_element_type=F32)


def _dot_tn(a, b):
    return lax.dot_general(a, b, (((0,), (0,)), ((), ())), preferred_element_type=F32)


def _layer_norm(y, g, b):
    mu = jnp.mean(y, axis=-1, keepdims=True)
    yc = y - mu
    var = jnp.mean(yc * yc, axis=-1, keepdims=True)
    return yc * lax.rsqrt(var + LN_EPS) * g + b


def _silu(v):
    return v * jax.nn.sigmoid(v)


def _softplus(v):
    return jnp.maximum(v, 0.0) + jnp.log1p(jnp.exp(-jnp.abs(v)))


def _split3(v):
    hi = v.astype(BF16)
    r1 = v - hi.astype(F32)
    mid = r1.astype(BF16)
    lo = (r1 - mid.astype(F32)).astype(BF16)
    return hi, mid, lo


def _tril_mask(n, strict=False):
    r = lax.broadcasted_iota(jnp.int32, (n, n), 0)
    c = lax.broadcasted_iota(jnp.int32, (n, n), 1)
    return (c < r) if strict else (c <= r)


def _chunk_cumsum(v, rows):
    r = lax.broadcasted_iota(jnp.int32, (rows, rows), 0)
    c = lax.broadcasted_iota(jnp.int32, (rows, rows), 1)
    tri = ((c <= r) & ((c // CHUNK) == (r // CHUNK))).astype(BF16)
    hi, mid, lo = _split3(v)
    return _dot(tri, hi) + _dot(tri, mid) + _dot(tri, lo)


def _ffn_body(x_ref, wg_ref, wu_ref, wd_ref, wgt_ref, wut_ref, wdt_ref, g_ref, b_ref, o_ref, *rest, nj):
    ob_ref, xb_ref = rest if len(rest) == 2 else (None, rest[0])
    j = pl.program_id(1)

    @pl.when(j == 0)
    def _():
        xb_ref[...] = x_ref[...].astype(BF16)
        o_ref[...] = jnp.zeros_like(o_ref)

    def down(wg, wu, wd):
        xb = xb_ref[...]
        h = (_silu(_dot(xb, wg.astype(BF16))) * _dot(xb, wu.astype(BF16))).astype(BF16)
        return _dot(h, wd.astype(BF16))

    @pl.when(j < nj - 1)
    def _():
        o_ref[...] += down(wg_ref[...], wu_ref[...], wd_ref[...])

    @pl.when(j == nj - 1)
    def _():
        o_ref[...] += down(wgt_ref[...], wut_ref[...], wdt_ref[...])
        g = g_ref[...]
        b = b_ref[...]

        def rows(r, carry):
            sl = pl.ds(pl.multiple_of(r * LN_ROWS, LN_ROWS), LN_ROWS)
            yn = _layer_norm(ALPHA * x_ref[sl, :] + MACARON * o_ref[sl, :], g, b)
            o_ref[sl, :] = yn
            if ob_ref is not None:
                ob_ref[sl, :] = yn.astype(BF16)
            return carry

        lax.fori_loop(0, o_ref.shape[0] // LN_ROWS, rows, 0)


def _ffn_ln(x, wg, wu, wd, layer, g, b, *, tm, tf, emit_bf16):
    t, d = x.shape
    f = wg.shape[2]
    n_full = f // tf
    f_tail = f - n_full * tf
    assert f_tail > 0 and f_tail % LANES == 0
    nj = n_full + 1
    last = n_full - 1
    const = lambda shape: pl.BlockSpec(shape, lambda i, j: (0, 0), pipeline_mode=pl.Buffered(1))
    tile = pl.BlockSpec((tm, d), lambda i, j: (i, 0))
    tile_once = pl.BlockSpec((tm, d), lambda i, j: (i, 0), pipeline_mode=pl.Buffered(1))
    out_specs = [tile, tile_once] if emit_bf16 else [tile]
    out_shape = [jax.ShapeDtypeStruct((t, d), F32), jax.ShapeDtypeStruct((t, d), BF16)][:len(out_specs)]
    return pl.pallas_call(
        functools.partial(_ffn_body, nj=nj),
        grid=(t // tm, nj),
        in_specs=[
            pl.BlockSpec((tm, d), lambda i, j: (i, 0)),
            pl.BlockSpec((None, d, tf), lambda i, j: (layer, 0, jnp.minimum(j, last))),
            pl.BlockSpec((None, d, tf), lambda i, j: (layer, 0, jnp.minimum(j, last))),
            pl.BlockSpec((None, tf, d), lambda i, j: (layer, jnp.minimum(j, last), 0)),
            const((d, f_tail)), const((d, f_tail)), const((f_tail, d)),
            const((1, d)), const((1, d)),
        ],
        out_specs=out_specs,
        out_shape=out_shape,
        scratch_shapes=[pltpu.VMEM((tm, d), BF16)],
        compiler_params=_params("arbitrary", "arbitrary"),
        name="ffn_ln",
    )(x, wg, wu, wd, wg[layer, :, f - f_tail:], wu[layer, :, f - f_tail:], wd[layer, f - f_tail:], g, b)


def _load_weight_tile(w_ref, wb_ref):
    w = w_ref[...]
    if w.shape != wb_ref.shape:
        w = w.T
    wb_ref[...] = w.astype(BF16)


def _proj_body(x_ref, w_ref, o_ref, wb_ref):
    @pl.when(pl.program_id(1) == 0)
    def _():
        _load_weight_tile(w_ref, wb_ref)

    o_ref[...] = _dot(x_ref[...], wb_ref[...])


def _weight_spec(w, k, tn, j0, layer):
    if w.ndim == 3:
        return pl.BlockSpec((None, tn, k), lambda j, i: (layer, j0 + j, 0))
    return pl.BlockSpec((k, tn), lambda j, i: (0, j0 + j))


def _proj(xb, w, *, tm, tn, col0=0, n=None, layer=0):
    t, k = xb.shape
    n = (w.shape[1] if w.ndim == 3 else w.shape[-1]) if n is None else n
    assert col0 % tn == 0 and n % tn == 0
    j0 = col0 // tn
    return pl.pallas_call(
        _proj_body,
        grid=(n // tn, t // tm),
        in_specs=[
            pl.BlockSpec((tm, k), lambda j, i: (i, 0)),
            _weight_spec(w, k, tn, j0, layer),
        ],
        out_specs=pl.BlockSpec((tm, tn), lambda j, i: (i, j)),
        out_shape=jax.ShapeDtypeStruct((t, n), F32),
        scratch_shapes=[pltpu.VMEM((k, tn), BF16)],
        compiler_params=_params("arbitrary", "arbitrary"),
        name="proj",
    )(xb, w)


def _conv_proj_body(x_ref, w_ref, cw_ref, o_ref, wb_ref, stage_ref, *, tm, tn, tiles_per_seq,
                    l2_heads, n_scaled_tiles, scale):
    jt = pl.program_id(0)
    i = pl.program_id(1)

    @pl.when(i == 0)
    def _():
        _load_weight_tile(w_ref, wb_ref)

    @pl.when(i % tiles_per_seq == 0)
    def _():
        stage_ref[0:SUBLANES, :] = jnp.zeros((SUBLANES, tn), F32)

    if l2_heads:
        fac = jnp.where(jt < n_scaled_tiles, jnp.float32(scale), jnp.float32(1.0))

    def post(v):
        v = _silu(v)
        if not l2_heads:
            return v
        out = []
        for h in range(CONV_SUB // LANES):
            vh = v[:, h * LANES:(h + 1) * LANES]
            ss = jnp.sum(vh * vh, axis=-1, keepdims=True)
            out.append(vh * lax.rsqrt(ss + L2_EPS) * fac)
        return jnp.concatenate(out, axis=1)

    subs = [slice(s * CONV_SUB, (s + 1) * CONV_SUB) for s in range(tn // CONV_SUB)]

    x = x_ref[...]
    raws = [_dot(x, wb_ref[:, cols]) for cols in subs]
    for cols, raw in zip(subs, raws):
        cw = cw_ref[:, cols]
        stage_ref[SUBLANES:SUBLANES + tm, cols] = raw
        acc = raw * cw[3:4]
        for d in (1, 2, 3):
            acc = acc + stage_ref[SUBLANES - d:SUBLANES - d + tm, cols] * cw[3 - d:4 - d]
        stage_ref[0:SUBLANES, cols] = raw[tm - SUBLANES:tm]
        o_ref[:, cols] = post(acc)


def _conv_proj(xb, w, cw, *, tm, tn, seq, col0, n, layer=0, l2_heads=False, n_scaled_tiles=0, scale=1.0):
    t, k = xb.shape
    nt = t // tm
    assert col0 % tn == 0 and n % tn == 0
    j0 = col0 // tn
    body = functools.partial(_conv_proj_body, tm=tm, tn=tn, tiles_per_seq=seq // tm,
                             l2_heads=l2_heads, n_scaled_tiles=n_scaled_tiles, scale=scale)
    return pl.pallas_call(
        body,
        grid=(n // tn, nt),
        in_specs=[
            pl.BlockSpec((tm, k), lambda j, i: (i, 0)),
            _weight_spec(w, k, tn, j0, layer),
            pl.BlockSpec((4, tn), lambda j, i: (0, j0 + j)),
        ],
        out_specs=pl.BlockSpec((tm, tn), lambda j, i: (i, j)),
        out_shape=jax.ShapeDtypeStruct((t, n), F32),
        scratch_shapes=[pltpu.VMEM((k, tn), BF16),
                        pltpu.VMEM((SUBLANES + tm, tn), F32)],
        compiler_params=_params("arbitrary", "arbitrary"),
        name="conv_proj",
    )(xb, w, cw)


def _gate_proj_body(x_ref, w_ref, alog_ref, dtb_ref, bg_ref, gc_ref, gr_ref, *, tm):
    raw = _dot(x_ref[...], w_ref[...])
    lane = lax.broadcasted_iota(jnp.int32, raw.shape, 1)
    beta = jax.nn.sigmoid(raw)
    g = -jnp.exp(alog_ref[...]) * _softplus(raw + dtb_ref[...])
    g = jnp.where((lane >= GDN_V_HEADS) & (lane < 2 * GDN_V_HEADS), g, 0.0)
    bg_ref[...] = jnp.where(lane < GDN_V_HEADS, beta, g)
    gc = _chunk_cumsum(g, tm)
    gc_ref[...] = gc
    width = (tm // CHUNK) * LANES
    t_idx = lax.broadcasted_iota(jnp.int32, (tm, width), 0)
    col = lax.broadcasted_iota(jnp.int32, (tm, width), 1)
    sel = (t_idx == (col // LANES) * CHUNK + (col % CHUNK)).astype(BF16)
    hi, mid, lo = _split3(gc)
    rows = _dot_tn(hi, sel) + _dot_tn(mid, sel) + _dot_tn(lo, sel)
    gr_ref[...] = rows[GDN_V_HEADS:2 * GDN_V_HEADS]


def _gate_proj(xb, w, alog, dtb, *, tm):
    t, k = xb.shape
    return pl.pallas_call(
        functools.partial(_gate_proj_body, tm=tm),
        grid=(t // tm,),
        in_specs=[
            pl.BlockSpec((tm, k), lambda i: (i, 0)),
            pl.BlockSpec((k, LANES), lambda i: (0, 0)),
            pl.BlockSpec((1, LANES), lambda i: (0, 0)),
            pl.BlockSpec((1, LANES), lambda i: (0, 0)),
        ],
        out_specs=[pl.BlockSpec((tm, LANES), lambda i: (i, 0))] * 2
        + [pl.BlockSpec((GDN_V_HEADS, (tm // CHUNK) * LANES), lambda i: (0, i))],
        out_shape=[jax.ShapeDtypeStruct((t, LANES), F32)] * 2
        + [jax.ShapeDtypeStruct((GDN_V_HEADS, (t // CHUNK) * LANES), F32)],
        compiler_params=_params("arbitrary"),
        name="gate_proj",
    )(xb, w, alog, dtb)


def _gla_body(q_ref, k_ref, v_ref, go_ref, a_ref, wlr_ref, blr_ref, ng_ref, o_ref, st_ref, *, rows):
    @pl.when(pl.program_id(1) == 0)
    def _():
        st_ref[...] = jnp.zeros_like(st_ref)

    causal = _tril_mask(CHUNK)
    ng = ng_ref[...]
    heads = range(GLA_HEADS)
    chunks = range(rows // CHUNK)
    head = lambda a, h: a[:, h * GLA_DK:(h + 1) * GLA_DK]

    z = _dot(a_ref[...].astype(BF16), wlr_ref[...]) + blr_ref[...]
    b = _chunk_cumsum(-_softplus(-z) / GLA_TAU, rows)
    k = k_ref[...]
    q_dec = (q_ref[...] * (GLA_DK ** -0.5) * jnp.exp(b)).astype(BF16)
    k_inv = (k * jnp.exp(-b)).astype(BF16)
    rows_of = lambda a, c: a[c * CHUNK:(c + 1) * CHUNK]
    b_last = [rows_of(b, c)[CHUNK - 1:CHUNK] for c in chunks]
    k_end = [(rows_of(k, c) * jnp.exp(b_last[c] - rows_of(b, c))).astype(BF16) for c in chunks]
    scores = [[jnp.where(causal, _dot_nt(head(rows_of(q_dec, c), h), head(rows_of(k_inv, c), h)),
                         0.0).astype(BF16) for h in heads] for c in chunks]
    for c in chunks:
        sl = slice(c * CHUNK, (c + 1) * CHUNK)
        sts = [st_ref[h] for h in heads]
        vs = [v_ref[sl, h * GLA_DV:(h + 1) * GLA_DV].astype(BF16) for h in heads]
        qd = rows_of(q_dec, c)
        outs = [_dot(scores[c][h], vs[h]) + _dot_nt(head(qd, h), sts[h].astype(BF16)) for h in heads]
        decay = jnp.exp(b_last[c])
        new = [sts[h] * head(decay, h) + _dot_tn(vs[h], head(k_end[c], h)) for h in heads]
        for h in heads:
            st_ref[h] = new[h]
        normed = [o * lax.rsqrt(jnp.mean(o * o, axis=-1, keepdims=True) + RMS_EPS) * ng for o in outs]
        o_ref[sl, :] = (jnp.concatenate(normed, axis=1) * _silu(go_ref[sl, :])).astype(BF16)


def _gla(proj, gate, wlr, blr, ng, *, batch, seq, rows):
    t = batch * seq
    nblk = seq // rows
    row = lambda b, n: b * nblk + n
    return pl.pallas_call(
        functools.partial(_gla_body, rows=rows),
        grid=(batch, nblk),
        in_specs=[
            pl.BlockSpec((rows, GLA_KW), lambda b, n: (row(b, n), 0)),
            pl.BlockSpec((rows, GLA_KW), lambda b, n: (row(b, n), 1)),
            pl.BlockSpec((rows, GLA_VW), lambda b, n: (row(b, n), AB_V // GLA_VW)),
            pl.BlockSpec((rows, GLA_VW), lambda b, n: (row(b, n), AB_G // GLA_VW)),
            pl.BlockSpec((rows, LANES), lambda b, n: (row(b, n), 0)),
            pl.BlockSpec((LANES, GLA_KW), lambda b, n: (0, 0)),
            pl.BlockSpec((1, GLA_KW), lambda b, n: (0, 0)),
            pl.BlockSpec((1, GLA_DV), lambda b, n: (0, 0)),
        ],
        out_specs=pl.BlockSpec((rows, GLA_VW), lambda b, n: (row(b, n), 0)),
        out_shape=jax.ShapeDtypeStruct((t, GLA_VW), BF16),
        scratch_shapes=[pltpu.VMEM((GLA_HEADS, GLA_DV, GLA_DK), F32)],
        compiler_params=_params("arbitrary", "arbitrary"),
        name="gla",
    )(proj, proj, proj, proj, gate, wlr, blr, ng)


def _s5_body(u_ref, bbd_ref, cbd_ref, pw_ref, d_ref, y_ref, xs_ref, cr_ref, ci_ref, *, rows):
    @pl.when(pl.program_id(2) == 0)
    def _():
        cr_ref[...] = jnp.zeros_like(cr_ref)
        ci_ref[...] = jnp.zeros_like(ci_ref)

    ns = S5_TILE_STATES
    u = u_ref[...]
    bu = _dot(u.astype(BF16), bbd_ref[0])
    levels = [(pw_ref[0, 2 * l], pw_ref[0, 2 * l + 1], 1 << l) for l in range(3)]
    qr, qi = pw_ref[0, 6], pw_ref[0, 7]
    cr = cr_ref[...]
    ci = ci_ref[...]
    for blk in range(rows // SUBLANES):
        sl = slice(blk * SUBLANES, (blk + 1) * SUBLANES)
        xr = bu[sl, :ns]
        xi = bu[sl, ns:]
        for pr, pi, s in levels:
            sr = pltpu.roll(xr, s, 0)
            si = pltpu.roll(xi, s, 0)
            xr, xi = xr + (pr * sr - pi * si), xi + (pr * si + pi * sr)
        xr, xi = xr + (qr * cr - qi * ci), xi + (qr * ci + qi * cr)
        cr = jnp.broadcast_to(xr[SUBLANES - 1:SUBLANES], (SUBLANES, ns))
        ci = jnp.broadcast_to(xi[SUBLANES - 1:SUBLANES], (SUBLANES, ns))
        xs_ref[sl, :ns] = xr
        xs_ref[sl, ns:] = xi
    cr_ref[...] = cr
    ci_ref[...] = ci
    y = _dot(xs_ref[...].astype(BF16), cbd_ref[0]) + d_ref[...] * u
    y3 = y * y * y
    y_ref[...] = y * (0.5 * (1.0 + jnp.tanh(np.float32(math.sqrt(2.0 / math.pi)) * (y + 0.044715 * y3))))


def _s5(proj, bbd, cbd, pw, d, *, batch, seq, rows, u_col0):
    t = batch * seq
    nblk = seq // rows
    ns = S5_TILE_STATES
    return pl.pallas_call(
        functools.partial(_s5_body, rows=rows),
        grid=(batch, S5_TILES, nblk),
        in_specs=[
            pl.BlockSpec((rows, LANES), lambda b, j, n: (b * nblk + n, u_col0 + j)),
            pl.BlockSpec((1, LANES, 2 * ns), lambda b, j, n: (j, 0, 0)),
            pl.BlockSpec((1, 2 * ns, LANES), lambda b, j, n: (j, 0, 0)),
            pl.BlockSpec((1, 8, SUBLANES, ns), lambda b, j, n: (j, 0, 0, 0)),
            pl.BlockSpec((1, LANES), lambda b, j, n: (0, j)),
        ],
        out_specs=pl.BlockSpec((rows, LANES), lambda b, j, n: (b * nblk + n, j)),
        out_shape=jax.ShapeDtypeStruct((t, S5_WIDTH), F32),
        scratch_shapes=[pltpu.VMEM((rows, 2 * ns), F32),
                        pltpu.VMEM((SUBLANES, ns), F32),
                        pltpu.VMEM((SUBLANES, ns), F32)],
        compiler_params=_params("arbitrary", "arbitrary", "arbitrary"),
        name="s5",
    )(proj, bbd, cbd, pw, d)


def _s5_tables(lam_re, lam_im, b_re, b_im, c_re, c_im, log_step):
    step = jnp.exp(log_step)[:, None]
    mag = jnp.exp(lam_re * step)
    a_re = mag * jnp.cos(lam_im * step)
    a_im = mag * jnp.sin(lam_im * step)
    inv_den = 1.0 / (lam_re * lam_re + lam_im * lam_im)
    f_re = ((a_re - 1.0) * lam_re + a_im * lam_im) * inv_den
    f_im = (a_im * lam_re - (a_re - 1.0) * lam_im) * inv_den
    bb_re = f_re[..., None] * b_re - f_im[..., None] * b_im
    bb_im = f_re[..., None] * b_im + f_im[..., None] * b_re
    tg, nt = S5_TILE_GROUPS, S5_TILES
    eye = jnp.eye(tg, dtype=F32)

    def in_table(bb):
        bb = bb.reshape(nt, tg, S5_STATE, S5_GROUP)
        return jnp.einsum('jgph,gk->jghkp', bb, eye).reshape(nt, LANES, S5_TILE_STATES)

    def out_table(cc):
        cc = cc.reshape(nt, tg, S5_GROUP, S5_STATE)
        return jnp.einsum('jghp,gk->jgpkh', cc, eye).reshape(nt, S5_TILE_STATES, LANES)

    bbd = jnp.concatenate([in_table(bb_re), in_table(bb_im)], axis=2).astype(BF16)
    cbd = jnp.concatenate([out_table(c_re), -out_table(c_im)], axis=1).astype(BF16)

    def cmul(x, y):
        return x[0] * y[0] - x[1] * y[1], x[0] * y[1] + x[1] * y[0]

    a1 = (a_re.reshape(nt, 1, S5_TILE_STATES), a_im.reshape(nt, 1, S5_TILE_STATES))
    powers = [a1]
    for _ in range(SUBLANES - 1):
        powers.append(cmul(powers[-1], a1))
    row = jnp.arange(SUBLANES)[None, :, None]
    tabs = []
    for s in (1, 2, 4):
        pr, pi = powers[s - 1]
        tabs.append(jnp.where(row >= s, pr, 0.0))
        tabs.append(jnp.where(row >= s, pi, 0.0))
    tabs.append(jnp.concatenate([p[0] for p in powers], axis=1))
    tabs.append(jnp.concatenate([p[1] for p in powers], axis=1))
    pw = jnp.stack([jnp.broadcast_to(tb, (nt, SUBLANES, S5_TILE_STATES)) for tb in tabs], axis=1)
    return bbd, cbd, pw.astype(F32)


def _mix0_out_body(og_ref, ys_ref, gw_ref, gb_ref, w1_ref, w2_ref, x_ref, g_ref, b_ref, o_ref):
    ys = ys_ref[...]
    gate = jax.nn.sigmoid(_dot(ys.astype(BF16), gw_ref[...]) + gb_ref[...])
    os5 = (ys * gate).astype(BF16)
    mix = _dot(og_ref[...], w1_ref[...]) + _dot(os5, w2_ref[...])
    o_ref[...] = _layer_norm(ALPHA * x_ref[...] + mix, g_ref[...], b_ref[...])


def _mix0_out(og, ys, gw, gb, w1, w2, x, g, b, *, tm):
    t, d = x.shape
    const = lambda shape: pl.BlockSpec(shape, lambda i: (0, 0))
    return pl.pallas_call(
        _mix0_out_body,
        grid=(t // tm,),
        in_specs=[
            pl.BlockSpec((tm, GLA_VW), lambda i: (i, 0)),
            pl.BlockSpec((tm, S5_WIDTH), lambda i: (i, 0)),
            const((S5_WIDTH, S5_WIDTH)), const((1, S5_WIDTH)),
            const((GLA_VW, d)), const((S5_WIDTH, d)),
            pl.BlockSpec((tm, d), lambda i: (i, 0)),
            const((1, d)), const((1, d)),
        ],
        out_specs=pl.BlockSpec((tm, d), lambda i: (i, 0)),
        out_shape=jax.ShapeDtypeStruct((t, d), F32),
        compiler_params=_params("arbitrary"),
        name="mix0_out",
    )(og, ys, gw, gb, w1, w2, x, g, b)


GDN_REP = GDN_V_HEADS // GDN_QK_HEADS


def _block_diag2(m, second):
    zero = jnp.zeros_like(m)
    return jnp.concatenate([jnp.where(second, zero, m), jnp.where(second, m, zero)], axis=0)


def _gdn_body(q_ref, k_ref, v_ref, z_ref, bg_ref, gc_ref, gr_ref, ng_ref, o_ref, st_ref, *, rows, heads):
    hg = pl.program_id(1)
    c2 = 2 * CHUNK
    vw = GDN_REP * GDN_DV

    @pl.when(pl.program_id(2) == 0)
    def _():
        st_ref[...] = jnp.zeros_like(st_ref)

    lane2 = lax.broadcasted_iota(jnp.int32, (CHUNK, c2), 1)
    row2 = lax.broadcasted_iota(jnp.int32, (CHUNK, c2), 0)
    second = lane2 >= CHUNK
    pos = jnp.where(second, lane2 - CHUNK, lane2)
    incl2 = pos <= row2
    strict2 = pos < row2
    eye2 = pos == row2
    lane_bg = lax.broadcasted_iota(jnp.int32, (CHUNK, LANES), 1)
    second_v = lax.broadcasted_iota(jnp.int32, (CHUNK, vw), 1) >= GDN_DV
    st_r = lax.broadcasted_iota(jnp.int32, (GDN_REP * GDN_DK, vw), 0) >= GDN_DK
    st_c = lax.broadcasted_iota(jnp.int32, (GDN_REP * GDN_DK, vw), 1) >= GDN_DV
    same_head = st_r == st_c
    ng = ng_ref[...]
    zero_rhs = jnp.zeros((CHUNK, GDN_DV + GDN_DK), BF16)

    def chunk(c, carry):
        r0 = pl.multiple_of(c * CHUNK, CHUNK)
        bg = bg_ref[pl.ds(r0, CHUNK), :]
        gcb = gc_ref[pl.ds(r0, CHUNK), :]

        def column(arr, idx):
            return jnp.sum(jnp.where(lane_bg == idx, arr, 0.0), axis=-1, keepdims=True)

        hs = range(heads)
        ks, qs, prods, cols, decays, xs, ps, pbds = [], [], [], [], [], [], [], []
        for g in hs:
            k = k_ref[pl.ds(r0, CHUNK), g * GDN_DK:(g + 1) * GDN_DK]
            q = q_ref[pl.ds(r0, CHUNK), g * GDN_DK:(g + 1) * GDN_DK]
            kb = k.astype(BF16)
            ks.append(k)
            qs.append(q)
            prods.append(_dot_nt(jnp.concatenate([kb, q.astype(BF16)], axis=0),
                                 jnp.concatenate([kb, kb], axis=0)))
        for g in hs:
            hq = hg * heads + g
            b0 = column(bg, GDN_REP * hq)
            b1 = column(bg, GDN_REP * hq + 1)
            g0 = column(gcb, GDN_V_HEADS + GDN_REP * hq)
            g1 = column(gcb, GDN_V_HEADS + GDN_REP * hq + 1)
            cols.append((b0, b1, g0, g1, g0[CHUNK - 1:CHUNK, :], g1[CHUNK - 1:CHUNK, :]))
            gc_row = jnp.where(second[0:1], gr_ref[c, pl.ds(GDN_REP * hq + 1, 1), :],
                               gr_ref[c, pl.ds(GDN_REP * hq, 1), :])
            decay2 = jnp.exp(jnp.where(incl2, jnp.where(second, g1, g0) - gc_row, -jnp.inf))
            a2 = jnp.where(strict2, prods[g][:CHUNK] * jnp.where(second, b1, b0) * decay2, 0.0)
            decays.append(decay2)
            xs.append(jnp.where(eye2, 1.0, 0.0) - a2)
            ps.append(a2.astype(BF16))
            pbds.append(_block_diag2(ps[g], second))
        power = 1
        while 2 * power < CHUNK:
            for g in hs:
                ps[g] = _dot(ps[g], pbds[g]).astype(BF16)
                pbds[g] = _block_diag2(ps[g], second)
            for g in hs:
                xs[g] = xs[g] + _dot(xs[g].astype(BF16), pbds[g])
            power *= 2
        sols, exps = [], []
        for g in hs:
            b0, b1, g0, g1, gl0, gl1 = cols[g]
            e0 = jnp.exp(g0)
            e1 = jnp.exp(g1)
            exps.append((e0, e1))
            v2 = v_ref[pl.ds(r0, CHUNK), g * vw:(g + 1) * vw]
            rhs0 = jnp.concatenate([v2[:, :GDN_DV] * b0, (ks[g] * b0) * e0], axis=1).astype(BF16)
            rhs1 = jnp.concatenate([v2[:, GDN_DV:] * b1, (ks[g] * b1) * e1], axis=1).astype(BF16)
            rhs_bd = jnp.concatenate([jnp.concatenate([rhs0, zero_rhs], axis=1),
                                      jnp.concatenate([zero_rhs, rhs1], axis=1)], axis=0)
            sols.append(_dot(xs[g].astype(BF16), rhs_bd))
        sts, wss = [], []
        for g in hs:
            sol = sols[g]
            e0, e1 = exps[g]
            w2 = jnp.concatenate([sol[:, GDN_DV:vw], sol[:, vw + GDN_DV:]], axis=1)
            qd2 = jnp.concatenate([qs[g] * e0, qs[g] * e1], axis=1)
            sts.append(st_ref[g])
            wss.append(_dot(jnp.concatenate([w2, qd2], axis=0).astype(BF16), sts[g].astype(BF16)))
        new_states, results = [], []
        for g in hs:
            b0, b1, g0, g1, gl0, gl1 = cols[g]
            sol = sols[g]
            u2 = jnp.concatenate([sol[:, :GDN_DV], sol[:, vw:vw + GDN_DV]], axis=1)
            v_new = (u2 - wss[g][:CHUNK]).astype(BF16)
            attn2 = (prods[g][CHUNK:] * decays[g]).astype(BF16)
            o2 = wss[g][CHUNK:] + _dot(attn2, _block_diag2(v_new, second_v))
            k_end2 = jnp.concatenate([ks[g] * jnp.exp(gl0 - g0), ks[g] * jnp.exp(gl1 - g1)],
                                     axis=1).astype(BF16)
            upd = _dot_tn(k_end2, v_new)
            new_states.append(jnp.where(st_r, jnp.exp(gl1), jnp.exp(gl0)) * sts[g]
                              + jnp.where(same_head, upd, 0.0))
            outs = []
            for hh in range(GDN_REP):
                o = o2[:, hh * GDN_DV:(hh + 1) * GDN_DV]
                outs.append(o * lax.rsqrt(jnp.mean(o * o, axis=-1, keepdims=True) + RMS_EPS) * ng)
            z2 = z_ref[pl.ds(r0, CHUNK), g * vw:(g + 1) * vw]
            results.append((jnp.concatenate(outs, axis=1) * _silu(z2)).astype(BF16))
        for g in hs:
            st_ref[g] = new_states[g]
            o_ref[pl.ds(r0, CHUNK), g * vw:(g + 1) * vw] = results[g]
        return carry

    lax.fori_loop(0, rows // CHUNK, chunk, 0, unroll=2)


def _gdn(qk, v, z, bg, gc, gc_rows, ng, *, batch, seq, rows, heads):
    t = batch * seq
    nblk = seq // rows
    cpb = rows // CHUNK
    row = lambda b, h, n: b * nblk + n
    vw = heads * GDN_REP * GDN_DV
    kw = heads * GDN_DK
    return pl.pallas_call(
        functools.partial(_gdn_body, rows=rows, heads=heads),
        grid=(batch, GDN_QK_HEADS // heads, nblk),
        in_specs=[
            pl.BlockSpec((rows, kw), lambda b, h, n: (row(b, h, n), h)),
            pl.BlockSpec((rows, kw), lambda b, h, n: (row(b, h, n), GDN_QK_HEADS // heads + h)),
            pl.BlockSpec((rows, vw), lambda b, h, n: (row(b, h, n), h)),
            pl.BlockSpec((rows, vw), lambda b, h, n: (row(b, h, n), h)),
            pl.BlockSpec((rows, LANES), lambda b, h, n: (row(b, h, n), 0)),
            pl.BlockSpec((rows, LANES), lambda b, h, n: (row(b, h, n), 0)),
            pl.BlockSpec((cpb, GDN_V_HEADS, GDN_REP * CHUNK), lambda b, h, n: (row(b, h, n), 0, 0)),
            pl.BlockSpec((1, GDN_DV), lambda b, h, n: (0, 0)),
        ],
        out_specs=pl.BlockSpec((rows, vw), lambda b, h, n: (row(b, h, n), h)),
        out_shape=jax.ShapeDtypeStruct((t, GDN_VW), BF16),
        scratch_shapes=[pltpu.VMEM((heads, GDN_REP * GDN_DK, GDN_REP * GDN_DV), F32)],
        compiler_params=_params("arbitrary", "arbitrary", "arbitrary"),
        name="gdn",
    )(qk, qk, v, z, bg, gc, gc_rows, ng)


def _out_ln_body(a_ref, w_ref, x_ref, g_ref, b_ref, o_ref):
    mix = _dot(a_ref[...], w_ref[...])
    o_ref[...] = _layer_norm(ALPHA * x_ref[...] + mix, g_ref[...], b_ref[...])


def _out_ln(a, w, x, g, b, *, tm):
    t, d = x.shape
    kdim = a.shape[1]
    const = lambda shape: pl.BlockSpec(shape, lambda i: (0, 0), pipeline_mode=pl.Buffered(1))
    return pl.pallas_call(
        _out_ln_body,
        grid=(t // tm,),
        in_specs=[
            pl.BlockSpec((tm, kdim), lambda i: (i, 0)),
            const((kdim, d)),
            pl.BlockSpec((tm, d), lambda i: (i, 0)),
            const((1, d)), const((1, d)),
        ],
        out_specs=pl.BlockSpec((tm, d), lambda i: (i, 0)),
        out_shape=jax.ShapeDtypeStruct((t, d), F32),
        compiler_params=_params("arbitrary"),
        name="out_ln",
    )(a, w, x, g, b)


def _tiles(batch, seq):
    tm = min(512, seq)
    return dict(tm=tm, ffn_tm=min(1024, seq), proj_tm=min(1024, seq), gla_rows=min(512, seq), s5_rows=min(2048, seq), gdn_rows=min(256, seq),
                gdn_heads=16)


def _ffn(x, wg, wu, wd, layer, g, b, tm, emit_bf16):
    return _ffn_ln(x, wg, wu, wd, layer, g[None], b[None], tm=tm, tf=256, emit_bf16=emit_bf16)


def _pad_cols(w, n):
    return jnp.pad(w, ((0, 0), (0, n - w.shape[1])))


def _gla_s5_layer(x, xb, p, i, g, b, batch, seq, tl):
    w_in = p['ab_w_in']
    w_gate = _pad_cols(w_in[i, :, AB_A:AB_U], LANES).astype(BF16)
    proj = _proj(xb, jnp.swapaxes(w_in, 1, 2), tm=tl['proj_tm'], tn=1024, col0=0, n=AB_A, layer=i)
    u = _proj(xb, w_in[i, :, AB_U:], tm=tl['proj_tm'], tn=1024)
    gate = _proj(xb, w_gate, tm=tl['tm'], tn=LANES)
    wlr = jnp.pad(p['gla_w_lr'][i], ((0, LANES - GLA_RANK), (0, 0))).astype(BF16)
    o_gla = _gla(proj, gate, wlr, p['gla_b_lr'][i][None], p['gla_norm_g'][i][None],
                 batch=batch, seq=seq, rows=tl['gla_rows'])
    bbd, cbd, pw = _s5_tables(p['s5_lam_re'][i], p['s5_lam_im'][i], p['s5_b_re'][i], p['s5_b_im'][i],
                              p['s5_c_re'][i], p['s5_c_im'][i], p['s5_log_step'][i])
    ys = _s5(u, bbd, cbd, pw, p['s5_d'][i][None], batch=batch, seq=seq, rows=tl['s5_rows'], u_col0=0)
    w_out = p['ab_w_out'][i].astype(BF16)
    return _mix0_out(o_gla, ys, p['s5_glu_w'][i].astype(BF16), p['s5_glu_b'][i][None],
                     w_out[:GLA_VW], w_out[GLA_VW:], x, g[None], b[None], tm=tl['tm'])


def _gdn_layer(x, xb, p, i, g, b, batch, seq, tl):
    t = batch * seq
    w_in = p['gdn_w_in']
    conv_w = p['gdn_conv_w'][i]
    tm = tl['tm']
    w_t = jnp.swapaxes(w_in, 1, 2)
    qk = _conv_proj(xb, w_t, conv_w, tm=tl['proj_tm'], tn=1024, seq=seq, col0=0, n=2 * GDN_KW, layer=i,
                    l2_heads=True, n_scaled_tiles=GDN_KW // 1024, scale=GDN_DK ** -0.5)
    v = _conv_proj(xb, w_t, conv_w, tm=tl['proj_tm'], tn=1024, seq=seq, col0=2 * GDN_KW, n=GDN_VW,
                   layer=i)
    z = _proj(xb, w_t, tm=tl['proj_tm'], tn=1024, col0=GDN_Z, n=GDN_VW, layer=i)
    w_gate = _pad_cols(w_in[i, :, GDN_B:GDN_IN], LANES).astype(BF16)
    lane_pad = lambda a: jnp.pad(a, (GDN_V_HEADS, LANES - 2 * GDN_V_HEADS))[None]
    bg, gc, gr = _gate_proj(xb, w_gate, lane_pad(p['gdn_a_log'][i]), lane_pad(p['gdn_dt_bias'][i]), tm=tm)
    gc_rows = gr.reshape(GDN_V_HEADS, t // CHUNK, LANES).transpose(1, 0, 2)
    o = _gdn(qk, v, z, bg, gc, gc_rows, p['gdn_norm_g'][i][None], batch=batch, seq=seq,
             rows=tl['gdn_rows'], heads=tl['gdn_heads'])
    return _out_ln(o, p['gdn_w_out'][i].astype(BF16), x, g[None], b[None], tm=min(256, tm))


def kernel(x, ffn_a_gate, ffn_a_up, ffn_a_down, ffn_b_gate, ffn_b_up, ffn_b_down, ln_g, ln_b, ab_w_in, gla_w_lr, gla_b_lr, gla_norm_g, s5_lam_re, s5_lam_im, s5_b_re, s5_b_im, s5_c_re, s5_c_im, s5_d, s5_log_step, s5_glu_w, s5_glu_b, ab_w_out, gdn_w_in, gdn_conv_w, gdn_a_log, gdn_dt_bias, gdn_norm_g, gdn_w_out):
    batch, seq, d = x.shape
    assert d == D_MODEL and seq % CHUNK == 0
    p = dict(ab_w_in=ab_w_in, gla_w_lr=gla_w_lr, gla_b_lr=gla_b_lr, gla_norm_g=gla_norm_g,
             s5_lam_re=s5_lam_re, s5_lam_im=s5_lam_im, s5_b_re=s5_b_re, s5_b_im=s5_b_im,
             s5_c_re=s5_c_re, s5_c_im=s5_c_im, s5_d=s5_d, s5_log_step=s5_log_step,
             s5_glu_w=s5_glu_w, s5_glu_b=s5_glu_b, ab_w_out=ab_w_out, gdn_w_in=gdn_w_in,
             gdn_conv_w=gdn_conv_w, gdn_a_log=gdn_a_log, gdn_dt_bias=gdn_dt_bias,
             gdn_norm_g=gdn_norm_g, gdn_w_out=gdn_w_out)
    tl = _tiles(batch, seq)
    h = x.reshape(batch * seq, d).astype(F32)
    for layer in range(DEPTH):
        h, hb = _ffn(h, ffn_a_gate, ffn_a_up, ffn_a_down, layer,
                     ln_g[layer, 0], ln_b[layer, 0], tl['ffn_tm'], True)
        i = layer // 2
        mixer = _gla_s5_layer if layer % 2 == 0 else _gdn_layer
        h = mixer(h, hb, p, i, ln_g[layer, 1], ln_b[layer, 1], batch, seq, tl)
        (h,) = _ffn(h, ffn_b_gate, ffn_b_up, ffn_b_down, layer,
                    ln_g[layer, 2], ln_b[layer, 2], tl['ffn_tm'], False)
    return h.reshape(batch, seq, d)
```
